```python
import math
import jax
import jax.numpy as jnp
from jax import lax
import numpy as np

D_MODEL = 1024
BATCH = 2
SEQ = 16384
DEPTH = 4

GRID_W = 64
CTX_LEN = 256
N_MIXERS = 3
NORM_EPS = 1e-6
ROPE_THETA = 10000.0

ATTN_HEAD_DIM = 128
ATTN_Q_HEADS = D_MODEL // ATTN_HEAD_DIM
ATTN_KV_HEADS = ATTN_Q_HEADS // 4
ATTN_GROUP = ATTN_Q_HEADS // ATTN_KV_HEADS
Q_BLOCK = 128

S5_GROUP_SIZE = 16
S5_GROUPS = D_MODEL // S5_GROUP_SIZE
S5_STATE = 64
S5_DT_MIN = 1e-3
S5_DT_MAX = 1e-1
S5_MIN_DECAY = 1e-4
SCAN_CHUNK = 128

RET_HEADS = 4
RET_QK_DIM = D_MODEL // RET_HEADS
RET_V_DIM = 2 * RET_QK_DIM
RET_CHUNK = 128

N_EXPERTS = 16
EXPERT_FF = 2816
EC_CAPACITY_FACTOR = 2

kernel_name = 'hybrid_interleaved_diffusion_trunk'


def rms_normalise(x):
    xf = x.astype(jnp.float32)
    return xf * lax.rsqrt(jnp.mean(xf * xf, axis=-1, keepdims=True) + NORM_EPS)


def rms_norm(x, g):
    return (rms_normalise(x) * g.astype(jnp.float32)).astype(x.dtype)


def modulate(h, shift, scale):
    return h * (1 + scale) + shift


def _flip(t, rev, axis):
    return jnp.flip(t, axis=axis) if rev else t


def axial_rope(n_tokens, head_dim):
    rows = n_tokens // GRID_W
    row = jnp.repeat(jnp.arange(rows, dtype=jnp.float32), GRID_W)
    col = jnp.tile(jnp.arange(GRID_W, dtype=jnp.float32), rows)
    n_freq = head_dim // 4
    inv_freq = ROPE_THETA ** (-jnp.arange(n_freq, dtype=jnp.float32) / n_freq)
    ang = jnp.concatenate([row[:, None] * inv_freq, col[:, None] * inv_freq], axis=-1)
    return jnp.cos(ang), jnp.sin(ang)


def apply_rope(x, cos, sin):
    half = x.shape[-1] // 2
    xf = x.astype(jnp.float32)
    x1, x2 = xf[..., :half], xf[..., half:]
    return jnp.concatenate([x1 * cos - x2 * sin, x2 * cos + x1 * sin], axis=-1).astype(x.dtype)


def gqa_mixer(h_ctx, h_lat, w_qkv, w_o, q_gain, k_gain, need_ctx):
    nq = ATTN_Q_HEADS * ATTN_HEAD_DIM
    nkv = ATTN_KV_HEADS * ATTN_HEAD_DIM
    scale = ATTN_HEAD_DIM ** -0.5

    def project(h):
        b, n, _ = h.shape
        qkv = h @ w_qkv
        q = qkv[..., :nq].reshape(b, n, ATTN_KV_HEADS, ATTN_GROUP, ATTN_HEAD_DIM).transpose(0, 2, 3, 1, 4)
        k = qkv[..., nq:nq + nkv].reshape(b, n, ATTN_KV_HEADS, ATTN_HEAD_DIM).transpose(0, 2, 1, 3)
        v = qkv[..., nq + nkv:].reshape(b, n, ATTN_KV_HEADS, ATTN_HEAD_DIM).transpose(0, 2, 1, 3)
        return rms_norm(q, q_gain), rms_norm(k, k_gain), v

    def attend(q, k, v):
        s = jnp.einsum('bkgqd,bksd->bkgqs', q, k, preferred_element_type=jnp.float32) * scale
        p = jax.nn.softmax(s, axis=-1).astype(v.dtype)
        return jnp.einsum('bkgqs,bksd->bkgqd', p, v)

    def merge_heads(o):
        b, _, _, t, _ = o.shape
        return o.transpose(0, 3, 1, 2, 4).reshape(b, t, nq)

    bsz, n_lat, _ = h_lat.shape
    cos, sin = axial_rope(n_lat, ATTN_HEAD_DIM)
    q_l, k_l, v_l = project(h_lat)
    q_l = apply_rope(q_l, cos, sin)
    k_l = apply_rope(k_l, cos, sin)
    q_c, k_c, v_c = project(h_ctx)
    k_all = jnp.concatenate([k_c, k_l], axis=2)
    v_all = jnp.concatenate([v_c, v_l], axis=2)
    n_blk = n_lat // Q_BLOCK
    q_blocks = q_l.reshape(bsz, ATTN_KV_HEADS, ATTN_GROUP, n_blk, Q_BLOCK, ATTN_HEAD_DIM).transpose(3, 0, 1, 2, 4, 5)
    o_blocks = lax.map(lambda qb: attend(qb, k_all, v_all), q_blocks)
    o_lat = o_blocks.transpose(1, 2, 3, 0, 4, 5).reshape(bsz, ATTN_KV_HEADS, ATTN_GROUP, n_lat, ATTN_HEAD_DIM)
    y_lat = merge_heads(o_lat) @ w_o
    y_ctx = merge_heads(attend(q_c, k_c, v_c)) @ w_o if need_ctx else None
    return y_ctx, y_lat


def s5_discretise(lam_re, lam_im, log_dt, b_re, b_im):
    lam_re = jnp.minimum(lam_re.astype(jnp.float32), -S5_MIN_DECAY)
    lam_im = lam_im.astype(jnp.float32)
    dt = jnp.exp(log_dt.astype(jnp.float32))[:, None]
    mag = jnp.exp(lam_re * dt)
    ang = lam_im * dt
    lb_re, lb_im = mag * jnp.cos(ang), mag * jnp.sin(ang)
    den = lam_re * lam_re + lam_im * lam_im
    f_re = ((lb_re - 1) * lam_re + lb_im * lam_im) / den
    f_im = (lb_im * lam_re - (lb_re - 1) * lam_im) / den
    b_re = b_re.astype(jnp.float32)
    b_im = b_im.astype(jnp.float32)
    bb_re = f_re[..., None] * b_re - f_im[..., None] * b_im
    bb_im = f_re[..., None] * b_im + f_im[..., None] * b_re
    return lb_re, lb_im, bb_re, bb_im


def _ssm_combine(e1, e2):
    a1r, a1i, b1r, b1i = e1
    a2r, a2i, b2r, b2i = e2
    return (a2r * a1r - a2i * a1i,
            a2r * a1i + a2i * a1r,
            a2r * b1r - a2i * b1i + b2r,
            a2r * b1i + a2i * b1r + b2i)


def s5_scan(u, lb_re, lb_im, bb_re, bb_im, c_re, c_im, h_re, h_im):
    bsz, n, _ = u.shape
    n_chunk = n // SCAN_CHUNK
    u_chunks = u.astype(jnp.float32).reshape(bsz, n_chunk, SCAN_CHUNK, S5_GROUPS, S5_GROUP_SIZE).transpose(1, 0, 2, 3, 4)
    full = (bsz, SCAN_CHUNK, S5_GROUPS, S5_STATE)
    a_re = jnp.broadcast_to(lb_re, full)
    a_im = jnp.broadcast_to(lb_im, full)
    c_re = c_re.astype(jnp.float32)
    c_im = c_im.astype(jnp.float32)

    def step(carry, u_blk):
        hr, hi = carry
        xr = jnp.einsum('blgi,gpi->blgp', u_blk, bb_re)
        xi = jnp.einsum('blgi,gpi->blgp', u_blk, bb_im)
        xr = xr.at[:, 0].add(lb_re * hr - lb_im * hi)
        xi = xi.at[:, 0].add(lb_re * hi + lb_im * hr)
        _, _, sr, si = lax.associative_scan(_ssm_combine, (a_re, a_im, xr, xi), axis=1)
        y = jnp.einsum('gip,blgp->blgi', c_re, sr) - jnp.einsum('gip,blgp->blgi', c_im, si)
        return (sr[:, -1], si[:, -1]), y

    (h_re, h_im), y = lax.scan(step, (h_re, h_im), u_chunks)
    return y.transpose(1, 0, 2, 3, 4).reshape(bsz, n, D_MODEL), h_re, h_im


def s5_mixer(h_ctx, h_lat, lam_re, lam_im, log_dt, b_re, b_im, c_re, c_im, d_skip, w_glu, b_glu, need_ctx):
    bsz = h_lat.shape[0]
    d_skip = d_skip.astype(jnp.float32)
    y_lat = d_skip * h_lat.astype(jnp.float32)
    y_ctx = d_skip * h_ctx.astype(jnp.float32)
    zeros = jnp.zeros((bsz, S5_GROUPS, S5_STATE), jnp.float32)
    for dr in range(2):
        rev = dr == 1
        disc = s5_discretise(lam_re[dr], lam_im[dr], log_dt[dr], b_re[dr], b_im[dr])
        yc, hr, hi = s5_scan(_flip(h_ctx, rev, 1), *disc, c_re[dr], c_im[dr], zeros, zeros)
        yl, _, _ = s5_scan(_flip(h_lat, rev, 1), *disc, c_re[dr], c_im[dr], hr, hi)
        y_ctx = y_ctx + _flip(yc, rev, 1)
        y_lat = y_lat + _flip(yl, rev, 1)

    def glu(y):
        z = jax.nn.gelu(y).astype(h_lat.dtype) @ w_glu + b_glu
        a, g = jnp.split(z, 2, axis=-1)
        return a * jax.nn.sigmoid(g)

    return (glu(y_ctx) if need_ctx else None), glu(y_lat)


def retention_chunked(q, k, v, log_gamma, s0):
    bsz, nh, n, _ = q.shape
    dv = v.shape[-1]
    t = RET_CHUNK
    n_chunk = n // t
    pos = jnp.arange(t, dtype=jnp.float32)
    lg = log_gamma[:, None]
    diff = pos[:, None] - pos[None, :]
    inner = jnp.where(diff >= 0, jnp.exp(lg[:, :, None] * jnp.maximum(diff, 0.0)), 0.0)
    q_decay = jnp.exp(lg * (pos + 1))[:, :, None]
    k_decay = jnp.exp(lg * (t - 1 - pos))[:, :, None]
    chunk_decay = jnp.exp(log_gamma * t)[:, None, None]

    def chunks(a):
        return a.reshape(bsz, nh, n_chunk, t, a.shape[-1]).transpose(2, 0, 1, 3, 4)

    def step(s, blk):
        qb, kb, vb = blk
        scores = jnp.einsum('bhid,bhjd->bhij', qb, kb) * inner
        o = jnp.einsum('bhij,bhjv->bhiv', scores, vb) + jnp.einsum('bhid,bhdv->bhiv', qb * q_decay, s)
        s = s * chunk_decay + jnp.einsum('bhjd,bhjv->bhdv', kb * k_decay, vb)
        return s, o

    s, o = lax.scan(step, s0, (chunks(q), chunks(k), chunks(v)))
    return o.transpose(1, 2, 0, 3, 4).reshape(bsz, nh, n, dv), s


def retention_mixer(h_ctx, h_lat, w_qkvg, w_o, log_decay, need_ctx):
    nqk = RET_HEADS * RET_QK_DIM
    nv = RET_HEADS * RET_V_DIM

    def project(h):
        b, n, _ = h.shape
        z = h @ w_qkvg

        def heads(a, hd):
            return a.reshape(b, n, RET_HEADS, hd).transpose(0, 2, 1, 3).astype(jnp.float32)

        q = heads(z[..., :nqk], RET_QK_DIM)
        k = heads(z[..., nqk:2 * nqk], RET_QK_DIM) * RET_QK_DIM ** -0.5
        v = heads(z[..., 2 * nqk:2 * nqk + nv], RET_V_DIM)
        g = z[..., 2 * nqk + nv:]
        return q, k, v, g

    cos, sin = axial_rope(h_lat.shape[1], RET_QK_DIM)
    q_l, k_l, v_l, g_l = project(h_lat)
    q_l = apply_rope(q_l, cos, sin)
    k_l = apply_rope(k_l, cos, sin)
    q_c, k_c, v_c, g_c = project(h_ctx)
    s0 = jnp.zeros((h_lat.shape[0], RET_HEADS, RET_QK_DIM, RET_V_DIM), jnp.float32)
    outs_c, outs_l = [], []
    for dr in range(2):
        rev = dr == 1
        log_gamma = -jnp.abs(log_decay[dr].astype(jnp.float32))
        oc, s_ctx = retention_chunked(_flip(q_c, rev, 2), _flip(k_c, rev, 2), _flip(v_c, rev, 2), log_gamma, s0)
        ol, _ = retention_chunked(_flip(q_l, rev, 2), _flip(k_l, rev, 2), _flip(v_l, rev, 2), log_gamma, s_ctx)
        outs_c.append(_flip(oc, rev, 2))
        outs_l.append(_flip(ol, rev, 2))

    def output(o, g):
        b, _, n, _ = o.shape
        o = rms_normalise(o).transpose(0, 2, 1, 3).reshape(b, n, nv).astype(g.dtype)
        return (jax.nn.silu(g) * o) @ w_o

    y_ctx = output(outs_c[0] + outs_c[1], g_c) if need_ctx else None
    return y_ctx, output(outs_l[0] + outs_l[1], g_l)


def expert_choice_ffn(h, w_router, w_up, w_down):
    bsz, n, _ = h.shape
    cap = EC_CAPACITY_FACTOR * n // N_EXPERTS
    affinity = jax.nn.softmax((h @ w_router).astype(jnp.float32), axis=-1)
    gate, idx = lax.top_k(affinity.transpose(0, 2, 1), cap)
    b_idx = jnp.arange(bsz)[:, None]

    def run_expert(args):
        wu, wd, ie, ge = args
        xe = h[b_idx, ie]
        a, u = jnp.split(xe @ wu, 2, axis=-1)
        return ((jax.nn.silu(a) * u) @ wd) * ge[..., None].astype(h.dtype)

    y = lax.map(run_expert, (w_up, w_down, idx.transpose(1, 0, 2), gate.transpose(1, 0, 2)))
    return jnp.zeros_like(h).at[jnp.arange(bsz)[:, None, None], idx].add(y.transpose(1, 0, 2, 3))


def setup_inputs(seed: int = 0) -> dict:
    key = jax.random.key(seed)
    ks = iter(jax.random.split(key, 40))
    f32 = jnp.float32

    def nrm(shape, scale):
        return jax.random.normal(next(ks), shape, f32) * scale

    n_a = len(range(0, DEPTH, N_MIXERS))
    n_b = len(range(1, DEPTH, N_MIXERS))
    n_c = len(range(2, DEPTH, N_MIXERS))
    d = D_MODEL
    qkv_w = (ATTN_Q_HEADS + 2 * ATTN_KV_HEADS) * ATTN_HEAD_DIM
    ret_w = 2 * RET_HEADS * RET_QK_DIM + 2 * RET_HEADS * RET_V_DIM
    lam_shape = (n_b, 2, S5_GROUPS, S5_STATE)
    state_n = jnp.arange(S5_STATE, dtype=f32)
    decay0 = jnp.log(1 - 2.0 ** (-5.0 - jnp.arange(RET_HEADS, dtype=f32)))
    return {
        'x': nrm((BATCH, SEQ, d), 1.0),
        'c': nrm((BATCH, d), 1.0),
        'ctx': nrm((BATCH, CTX_LEN, d), 1.0),
        'c_ctx': nrm((d,), 1.0),
        'mod_w': nrm((DEPTH, d, 6 * d), 0.5 * d ** -0.5),
        'mod_b': nrm((DEPTH, 6 * d), 0.02),
        'norm_g': 1.0 + nrm((DEPTH, 2, d), 0.02),
        'attn_w_qkv': nrm((n_a, d, qkv_w), d ** -0.5),
        'attn_w_o': nrm((n_a, ATTN_Q_HEADS * ATTN_HEAD_DIM, d), (ATTN_Q_HEADS * ATTN_HEAD_DIM) ** -0.5),
        'attn_q_gain': 1.0 + nrm((n_a, ATTN_HEAD_DIM), 0.02),
        'attn_k_gain': 1.0 + nrm((n_a, ATTN_HEAD_DIM), 0.02),
        's5_lambda_re': -0.5 + nrm(lam_shape, 0.01),
        's5_lambda_im': jnp.pi * state_n + nrm(lam_shape, 0.01),
        's5_log_dt': jax.random.uniform(next(ks), (n_b, 2, S5_GROUPS), f32, math.log(S5_DT_MIN), math.log(S5_DT_MAX)),
        's5_b_re': nrm((n_b, 2, S5_GROUPS, S5_STATE, S5_GROUP_SIZE), (2 * S5_GROUP_SIZE) ** -0.5),
        's5_b_im': nrm((n_b, 2, S5_GROUPS, S5_STATE, S5_GROUP_SIZE), (2 * S5_GROUP_SIZE) ** -0.5),
        's5_c_re': nrm((n_b, 2, S5_GROUPS, S5_GROUP_SIZE, S5_STATE), (2 * S5_STATE) ** -0.5),
        's5_c_im': nrm((n_b, 2, S5_GROUPS, S5_GROUP_SIZE, S5_STATE), (2 * S5_STATE) ** -0.5),
        's5_d': nrm((n_b, d), 1.0),
        's5_w_glu': nrm((n_b, d, 2 * d), d ** -0.5),
        's5_b_glu': nrm((n_b, 2 * d), 0.02),
        'ret_w_qkvg': nrm((n_c, d, ret_w), d ** -0.5),
        'ret_w_o': nrm((n_c, RET_HEADS * RET_V_DIM, d), (RET_HEADS * RET_V_DIM) ** -0.5),
        'ret_log_decay': decay0 * (1.0 + nrm((n_c, 2, RET_HEADS), 0.02)),
        'moe_w_router': nrm((DEPTH, d, N_EXPERTS), d ** -0.5),
        'moe_w_up': nrm((DEPTH, N_EXPERTS, d, 2 * EXPERT_FF), d ** -0.5),
        'moe_w_down': nrm((DEPTH, N_EXPERTS, EXPERT_FF, d), EXPERT_FF ** -0.5),
        'final_g': 1.0 + nrm((d,), 0.02),
    }


def reference(x, c, ctx, c_ctx, mod_w, mod_b, norm_g, attn_w_qkv, attn_w_o, attn_q_gain, attn_k_gain,
              s5_lambda_re, s5_lambda_im, s5_log_dt, s5_b_re, s5_b_im, s5_c_re, s5_c_im, s5_d, s5_w_glu, s5_b_glu,
              ret_w_qkvg, ret_w_o, ret_log_decay, moe_w_router, moe_w_up, moe_w_down, final_g):
    x_lat, x_ctx = x, ctx
    silu_c = jax.nn.silu(c)
    silu_cc = jax.nn.silu(c_ctx)
    for i in range(DEPTH):
        kind, j = i % N_MIXERS, i // N_MIXERS
        need_ctx = i < DEPTH - 1
        mod_lat = (silu_c @ mod_w[i] + mod_b[i])[:, None, :]
        mod_ctx = silu_cc @ mod_w[i] + mod_b[i]
        sh1, sc1, g1, sh2, sc2, g2 = jnp.split(mod_lat, 6, axis=-1)
        csh1, csc1, cg1, csh2, csc2, cg2 = jnp.split(mod_ctx, 6, axis=-1)
        h_lat = modulate(rms_norm(x_lat, norm_g[i, 0]), sh1, sc1)
        h_ctx = modulate(rms_norm(x_ctx, norm_g[i, 0]), csh1, csc1)
        if kind == 0:
            y_ctx, y_lat = gqa_mixer(h_ctx, h_lat, attn_w_qkv[j], attn_w_o[j], attn_q_gain[j], attn_k_gain[j], need_ctx)
        elif kind == 1:
            y_ctx, y_lat = s5_mixer(h_ctx, h_lat, s5_lambda_re[j], s5_lambda_im[j], s5_log_dt[j], s5_b_re[j], s5_b_im[j],
                                    s5_c_re[j], s5_c_im[j], s5_d[j], s5_w_glu[j], s5_b_glu[j], need_ctx)
        else:
            y_ctx, y_lat = retention_mixer(h_ctx, h_lat, ret_w_qkvg[j], ret_w_o[j], ret_log_decay[j], need_ctx)
        x_lat = x_lat + g1 * y_lat
        h_lat = modulate(rms_norm(x_lat, norm_g[i, 1]), sh2, sc2)
        x_lat = x_lat + g2 * expert_choice_ffn(h_lat, moe_w_router[i], moe_w_up[i], moe_w_down[i])
        if need_ctx:
            x_ctx = x_ctx + cg1 * y_ctx
            h_ctx = modulate(rms_norm(x_ctx, norm_g[i, 1]), csh2, csc2)
            x_ctx = x_ctx + cg2 * expert_choice_ffn(h_ctx, moe_w_router[i], moe_w_up[i], moe_w_down[i])
    return rms_norm(x_lat, final_g)
```

```python
import functools
import math

import jax
import jax.numpy as jnp
from jax import lax
from jax.experimental import pallas as pl
from jax.experimental.pallas import tpu as pltpu

F32 = jnp.float32
BF16 = jnp.bfloat16

GRID_W = 64
N_MIXERS = 3
NORM_EPS = 1e-6
ROPE_THETA = 10000.0
ATTN_HEAD_DIM = 128
ATTN_GROUP = 4
S5_GROUP_SIZE = 16
S5_STATE = 64
S5_MIN_DECAY = 1e-4
RET_HEADS = 4
N_EXPERTS = 16
EC_CAPACITY_FACTOR = 2

ROW_TILE = 256
S5_SUB = 16
VMEM_LIMIT = 48 * 1024 * 1024
HI = lax.Precision.HIGHEST


def _cparams(sem):
    return pltpu.CompilerParams(dimension_semantics=sem, vmem_limit_bytes=VMEM_LIMIT)


def _norm_mod(x, g, shift, scale):
    ms = jnp.mean(x * x, axis=-1, keepdims=True)
    return (x * lax.rsqrt(ms + NORM_EPS) * g) * (1.0 + scale) + shift


def _largest_divisor(n, cap, mult):
    best = None
    for t in range(mult, min(n, cap) + 1, mult):
        if n % t == 0:
            best = t
    assert best is not None, (n, cap, mult)
    return best


def _mod_kernel(c_ref, w_ref, b_ref, o_ref):
    c = c_ref[...]
    s = c * jax.nn.sigmoid(c)
    o_ref[...] = jnp.dot(s, w_ref[...], precision=HI, preferred_element_type=F32) + b_ref[...]


def _modulation(cvec, mod_w, mod_b):
    depth, d, n = mod_w.shape
    tn = _largest_divisor(n, 1536, 128)
    return pl.pallas_call(
        _mod_kernel,
        grid=(depth, n // tn),
        in_specs=[pl.BlockSpec((8, d), lambda l, j: (0, 0)),
                  pl.BlockSpec((None, d, tn), lambda l, j: (l, 0, j)),
                  pl.BlockSpec((None, 1, tn), lambda l, j: (l, 0, j))],
        out_specs=pl.BlockSpec((None, 8, tn), lambda l, j: (l, 0, j)),
        out_shape=jax.ShapeDtypeStruct((depth, 8, n), F32),
        compiler_params=_cparams(("arbitrary", "arbitrary")),
        name="modulation",
    )(cvec, mod_w, mod_b.reshape(depth, 1, n))


def _mod_spec(d):
    return pl.BlockSpec((None, None, 6, d), lambda b, i, *_: (b, jnp.minimum(i, 1), 0, 0))


def _attn_in_kernel(x_ref, mod_ref, g_ref, w_ref, qg_ref, kg_ref, cos_ref, sin_ref,
                    q_ref, k_ref, v_ref, *, n_q, n_kv):
    hd = ATTN_HEAD_DIM
    h = _norm_mod(x_ref[...], g_ref[...], mod_ref[0:1, :], mod_ref[1:2, :]).astype(BF16)
    qkv = jnp.dot(h, w_ref[...], preferred_element_type=F32)
    cos, sin = cos_ref[...], sin_ref[...]

    def norm_rope(t, gain):
        t = t * lax.rsqrt(jnp.mean(t * t, axis=-1, keepdims=True) + NORM_EPS) * gain
        return t * cos + pltpu.roll(t, hd // 2, axis=1) * sin

    scale = hd ** -0.5
    for j in range(n_q):
        q_ref[j] = (norm_rope(qkv[:, j * hd:(j + 1) * hd], qg_ref[...]) * scale).astype(BF16)
    for j in range(n_kv):
        c0 = (n_q + j) * hd
        k_ref[j] = norm_rope(qkv[:, c0:c0 + hd], kg_ref[...]).astype(BF16)
        c1 = (n_q + n_kv + j) * hd
        v_ref[j] = qkv[:, c1:c1 + hd].astype(BF16)


def _attn_in(x, modsel, g, w_qkv, q_gain, k_gain, cos2, sin2):
    b, t, d = x.shape
    hd = ATTN_HEAD_DIM
    n_tot = w_qkv.shape[1] // hd
    n_q = d // hd
    n_kv = (n_tot - n_q) // 2
    tm = ROW_TILE
    kern = functools.partial(_attn_in_kernel, n_q=n_q, n_kv=n_kv)
    return pl.pallas_call(
        kern,
        grid=(b, t // tm),
        in_specs=[pl.BlockSpec((None, tm, d), lambda b, i: (b, i, 0)),
                  _mod_spec(d),
                  pl.BlockSpec((1, d), lambda b, i: (0, 0)),
                  pl.BlockSpec(w_qkv.shape, lambda b, i: (0, 0)),
                  pl.BlockSpec((1, hd), lambda b, i: (0, 0)),
                  pl.BlockSpec((1, hd), lambda b, i: (0, 0)),
                  pl.BlockSpec((tm, hd), lambda b, i: (i, 0)),
                  pl.BlockSpec((tm, hd), lambda b, i: (i, 0))],
        out_specs=[pl.BlockSpec((None, n_q, tm, hd), lambda b, i: (b, 0, i, 0)),
                   pl.BlockSpec((None, n_kv, tm, hd), lambda b, i: (b, 0, i, 0)),
                   pl.BlockSpec((None, n_kv, tm, hd), lambda b, i: (b, 0, i, 0))],
        out_shape=[jax.ShapeDtypeStruct((b, n_q, t, hd), BF16),
                   jax.ShapeDtypeStruct((b, n_kv, t, hd), BF16),
                   jax.ShapeDtypeStruct((b, n_kv, t, hd), BF16)],
        compiler_params=_cparams(("arbitrary", "arbitrary")),
        name="attn_in",
    )(x, modsel, g.reshape(1, d), w_qkv, q_gain.reshape(1, hd), k_gain.reshape(1, hd), cos2, sin2)


def _flash_kernel(q_ref, k_ref, v_ref, o_ref, m_scr, l_scr, acc_scr, *, tq, n_ctx):
    hd = ATTN_HEAD_DIM
    qi = pl.program_id(2)
    ki = pl.program_id(3)
    nk = pl.num_programs(3)

    @pl.when(ki == 0)
    def _():
        m_scr[...] = jnp.full(m_scr.shape, -jnp.inf, F32)
        l_scr[...] = jnp.zeros(l_scr.shape, F32)
        acc_scr[...] = jnp.zeros(acc_scr.shape, F32)

    def step(ctx_only):
        q = q_ref[...].reshape(ATTN_GROUP * tq, hd)
        s = lax.dot_general(q, k_ref[...], (((1,), (1,)), ((), ())), preferred_element_type=F32)
        if ctx_only:
            col = lax.broadcasted_iota(jnp.int32, s.shape, 1)
            s = jnp.where(col < n_ctx, s, -jnp.inf)
        m_prev = m_scr[...]
        m_new = jnp.maximum(m_prev, jnp.max(s, axis=1, keepdims=True))
        alpha = jnp.exp(m_prev - m_new)
        p = jnp.exp(s - m_new)
        l_scr[...] = alpha * l_scr[...] + jnp.sum(p, axis=1, keepdims=True)
        acc_scr[...] = alpha * acc_scr[...] + jnp.dot(p.astype(BF16), v_ref[...], preferred_element_type=F32)
        m_scr[...] = m_new

    @pl.when(jnp.logical_and(qi == 0, ki == 0))
    def _():
        step(True)

    @pl.when(qi > 0)
    def _():
        step(False)

    @pl.when(ki == nk - 1)
    def _():
        o = acc_scr[...] / l_scr[...]
        for g in range(ATTN_GROUP):
            o_ref[:, g * hd:(g + 1) * hd] = o[g * tq:(g + 1) * tq, :].astype(o_ref.dtype)


def _flash(q, k, v, n_ctx):
    b, n_q, t, hd = q.shape
    n_kv = k.shape[1]
    tq = ROW_TILE
    assert n_ctx == tq and n_q == n_kv * ATTN_GROUP
    tk = _largest_divisor(t, 1280, 256)
    gw = ATTN_GROUP * hd
    kern = functools.partial(_flash_kernel, tq=tq, n_ctx=n_ctx)

    def kv_map(b, h, i, j):
        return (b, h, jnp.where(i == 0, 0, j), 0)

    return pl.pallas_call(
        kern,
        grid=(b, n_kv, t // tq, t // tk),
        in_specs=[pl.BlockSpec((None, ATTN_GROUP, tq, hd), lambda b, h, i, j: (b, h, i, 0)),
                  pl.BlockSpec((None, None, tk, hd), kv_map),
                  pl.BlockSpec((None, None, tk, hd), kv_map)],
        out_specs=pl.BlockSpec((None, tq, gw), lambda b, h, i, j: (b, i, h)),
        out_shape=jax.ShapeDtypeStruct((b, t, n_q * hd), BF16),
        scratch_shapes=[pltpu.VMEM((ATTN_GROUP * tq, 1), F32),
                        pltpu.VMEM((ATTN_GROUP * tq, 1), F32),
                        pltpu.VMEM((ATTN_GROUP * tq, hd), F32)],
        compiler_params=_cparams(("arbitrary", "arbitrary", "arbitrary", "arbitrary")),
        name="flash_attn",
    )(q, k, v)


def _post_tail(x_new, mod_ref, g2_ref, wr_ref, x_out, h_out, lg_out):
    x_out[...] = x_new
    h2 = _norm_mod(x_new, g2_ref[...], mod_ref[3:4, :], mod_ref[4:5, :])
    h_out[...] = h2.astype(BF16)
    lg_out[...] = lax.dot_general(wr_ref[...], h2, (((1,), (1,)), ((), ())),
                                  precision=HI, preferred_element_type=F32)


def _post_plain_kernel(o_ref, w_ref, x_ref, mod_ref, g2_ref, wr_ref, x_out, h_out, lg_out):
    y = jnp.dot(o_ref[...], w_ref[...], preferred_element_type=F32)
    _post_tail(x_ref[...] + mod_ref[2:3, :] * y, mod_ref, g2_ref, wr_ref, x_out, h_out, lg_out)


def _post_glu_kernel(y_ref, hs_ref, dsk_ref, w_ref, b_ref, x_ref, mod_ref, g2_ref, wr_ref,
                     x_out, h_out, lg_out):
    d = x_ref.shape[-1]
    yt = dsk_ref[...] * hs_ref[...].astype(F32) + y_ref[...]
    z = jnp.dot(jax.nn.gelu(yt).astype(BF16), w_ref[...], preferred_element_type=F32) + b_ref[...]
    y = z[:, :d] * jax.nn.sigmoid(z[:, d:])
    _post_tail(x_ref[...] + mod_ref[2:3, :] * y, mod_ref, g2_ref, wr_ref, x_out, h_out, lg_out)


def _post_call(kern, name, row_inputs, const_inputs, x, modsel, g2, w_router_t):
    b, t, d = x.shape
    tm = ROW_TILE
    ne = w_router_t.shape[0]
    row_specs = [pl.BlockSpec((None, tm, a.shape[-1]), lambda b, i: (b, i, 0)) for a in row_inputs]
    const_specs = [pl.BlockSpec(a.shape, lambda b, i: (0, 0)) for a in const_inputs]
    n_row = len(row_inputs)

    def wrapped(*refs):
        rows = refs[:n_row]
        consts = refs[n_row:n_row + len(const_inputs)]
        rest = refs[n_row + len(const_inputs):]
        kern(*rows, *consts, *rest)

    return pl.pallas_call(
        wrapped,
        grid=(b, t // tm),
        in_specs=row_specs + const_specs + [
            pl.BlockSpec((None, tm, d), lambda b, i: (b, i, 0)),
            _mod_spec(d),
            pl.BlockSpec((1, d), lambda b, i: (0, 0)),
            pl.BlockSpec((ne, d), lambda b, i: (0, 0))],
        out_specs=[pl.BlockSpec((None, tm, d), lambda b, i: (b, i, 0)),
                   pl.BlockSpec((None, tm, d), lambda b, i: (b, i, 0)),
                   pl.BlockSpec((None, ne, tm), lambda b, i: (b, 0, i))],
        out_shape=[jax.ShapeDtypeStruct((b, t, d), F32),
                   jax.ShapeDtypeStruct((b, t, d), BF16),
                   jax.ShapeDtypeStruct((b, ne, t), F32)],
        compiler_params=_cparams(("arbitrary", "arbitrary")),
        name=name,
    )(*row_inputs, *const_inputs, x, modsel, g2.reshape(1, d), w_router_t)


def _ffn_kernel(x_ref, wa_ref, wu_ref, wd_ref, gate_ref, o_ref):
    f = pl.program_id(2)

    @pl.when(f == 0)
    def _():
        o_ref[...] = jnp.zeros(o_ref.shape, F32)

    x = x_ref[...]
    a = jnp.dot(x, wa_ref[...], preferred_element_type=F32)
    u = jnp.dot(x, wu_ref[...], preferred_element_type=F32)
    hmid = (a * jax.nn.sigmoid(a) * u).astype(BF16)
    o_ref[...] += jnp.dot(hmid, wd_ref[...], preferred_element_type=F32)

    @pl.when(f == pl.num_programs(2) - 1)
    def _():
        o_ref[...] = o_ref[...] * gate_ref[...]


def _expert_ffn(xe, w_up, w_down, gate):
    ne, r, d = xe.shape
    ff = w_down.shape[1]
    tm = _largest_divisor(r, 1040, 16)
    tf = _largest_divisor(ff, 256, 128)
    nf = ff // tf
    return pl.pallas_call(
        _ffn_kernel,
        grid=(ne, r // tm, nf),
        in_specs=[pl.BlockSpec((None, tm, d), lambda e, m, f: (e, m, 0)),
                  pl.BlockSpec((None, d, tf), lambda e, m, f: (e, 0, f)),
                  pl.BlockSpec((None, d, tf), lambda e, m, f: (e, 0, nf + f)),
                  pl.BlockSpec((None, tf, d), lambda e, m, f: (e, f, 0)),
                  pl.BlockSpec((None, tm, 1), lambda e, m, f: (e, m, 0))],
        out_specs=pl.BlockSpec((None, tm, d), lambda e, m, f: (e, m, 0)),
        out_shape=jax.ShapeDtypeStruct((ne, r, d), F32),
        compiler_params=_cparams(("arbitrary", "arbitrary", "arbitrary")),
        name="expert_ffn",
    )(xe, w_up, w_up, w_down, gate)


def _moe(x_mid, h2, logits_t, g2, w_up, w_down, n_ctx):
    b, t, d = x_mid.shape
    ne = logits_t.shape[1]
    aff = jax.nn.softmax(logits_t, axis=1)
    base = (jnp.arange(b, dtype=jnp.int32) * t)[:, None, None]

    def routed(a, offset):
        cap = EC_CAPACITY_FACTOR * a.shape[-1] // ne
        gate, idx = lax.top_k(a, cap)
        flat = idx.astype(jnp.int32) + base + offset
        return (gate.transpose(1, 0, 2).reshape(ne, -1), flat.transpose(1, 0, 2).reshape(ne, -1))

    gate_l, idx_l = routed(aff[:, :, n_ctx:], n_ctx)
    gate_c, idx_c = routed(aff[:, :, :n_ctx], 0)
    gate = jnp.concatenate([gate_l, gate_c], axis=1)
    idx = jnp.concatenate([idx_l, idx_c], axis=1)
    h_flat = h2.reshape(b * t, d)
    xe = jnp.take(h_flat, idx, axis=0)
    y = _expert_ffn(xe, w_up, w_down, gate[..., None])
    moe = jnp.zeros((b * t, d), F32).at[idx.reshape(-1)].add(y.reshape(-1, d))
    moe = moe.reshape(b, t, d)
    gsel = jnp.concatenate([jnp.broadcast_to(g2[:, 0], (b, n_ctx, d)),
                            jnp.broadcast_to(g2[:, 1], (b, t - n_ctx, d))], axis=1)
    return x_mid + gsel * moe


def _norm_only_kernel(x_ref, mod_ref, g_ref, h_ref):
    h_ref[...] = _norm_mod(x_ref[...], g_ref[...], mod_ref[0:1, :], mod_ref[1:2, :]).astype(h_ref.dtype)


def _norm_only(x, modsel, g):
    b, t, d = x.shape
    tm = ROW_TILE
    return pl.pallas_call(
        _norm_only_kernel,
        grid=(b, t // tm),
        in_specs=[pl.BlockSpec((None, tm, d), lambda b, i: (b, i, 0)),
                  _mod_spec(d),
                  pl.BlockSpec((1, d), lambda b, i: (0, 0))],
        out_specs=pl.BlockSpec((None, tm, d), lambda b, i: (b, i, 0)),
        out_shape=jax.ShapeDtypeStruct((b, t, d), BF16),
        compiler_params=_cparams(("arbitrary", "arbitrary")),
        name="norm_mod",
    )(x, modsel, g.reshape(1, d))


def _s5_core_kernel(u_ref, win_ref, toep_ref, wout_ref, a_ref, y_ref, z_scr, h_scr, *, n_b, m, m_ctx):
    u = u_ref[...]
    half = z_scr.shape[1] // 2
    for dr in range(2):
        z_scr[...] = jnp.dot(u, win_ref[dr], preferred_element_type=F32)
        ar = a_ref[dr, 0:1, :]
        ai = a_ref[dr, 1:2, :]

        rid = lax.broadcasted_iota(jnp.int32, (8, half), 0)

        def visit(blk, carry, reverse=(dr == 1)):
            new = []
            for bb in range(n_b):
                sr, si = carry[2 * bb], carry[2 * bb + 1]
                base = pl.multiple_of(bb * m + blk * 8, 8)
                z8 = z_scr[pl.ds(base, 8), :]
                hr = jnp.zeros((8, half), F32)
                hi = jnp.zeros((8, half), F32)
                for r in (range(7, -1, -1) if reverse else range(8)):
                    hr = jnp.where(rid == r, sr, hr)
                    hi = jnp.where(rid == r, si, hi)
                    zr = z8[r:r + 1, 0:half]
                    zi = z8[r:r + 1, half:2 * half]
                    sr, si = ar * sr - ai * si + zr, ar * si + ai * sr + zi
                h_scr[pl.ds(base, 8), 0:half] = hr
                h_scr[pl.ds(base, 8), half:2 * half] = hi
                new += [sr, si]
            return tuple(new)

        zero = tuple(jnp.zeros((1, half), F32) for _ in range(2 * n_b))
        nb, nb_ctx = m // 8, m_ctx // 8
        if dr == 0:
            lax.fori_loop(0, nb, visit, zero)
        else:
            carry = lax.fori_loop(0, nb_ctx, lambda s, cr: visit(nb_ctx - 1 - s, cr), zero)
            lax.fori_loop(0, nb - nb_ctx, lambda s, cr: visit(nb - 1 - s, cr), carry)

        y = (jnp.dot(u, toep_ref[dr], preferred_element_type=F32)
             + jnp.dot(h_scr[...].astype(BF16), wout_ref[dr], preferred_element_type=F32))
        if dr == 0:
            y_ref[...] = y
        else:
            y_ref[...] += y


def _s5_weights(lam_re, lam_im, log_dt, b_re, b_im, c_re, c_im):
    sub = S5_SUB
    outs = []
    for dr in range(2):
        lre = jnp.minimum(lam_re[dr].astype(F32), -S5_MIN_DECAY)
        lim = lam_im[dr].astype(F32)
        dt = jnp.exp(log_dt[dr].astype(F32))[:, None]
        mag = jnp.exp(lre * dt)
        ang = lim * dt
        lbr, lbi = mag * jnp.cos(ang), mag * jnp.sin(ang)
        den = lre * lre + lim * lim
        f_re = ((lbr - 1) * lre + lbi * lim) / den
        f_im = (lbi * lre - (lbr - 1) * lim) / den
        bre, bim = b_re[dr].astype(F32), b_im[dr].astype(F32)
        bbr = f_re[..., None] * bre - f_im[..., None] * bim
        bbi = f_re[..., None] * bim + f_im[..., None] * bre
        cr, ci = c_re[dr].astype(F32), c_im[dr].astype(F32)
        pr, pi = [jnp.ones_like(lbr)], [jnp.zeros_like(lbr)]
        for _ in range(sub):
            pr.append(pr[-1] * lbr - pi[-1] * lbi)
            pi.append(pr[-2] * lbi + pi[-1] * lbr)
        pw_r, pw_i = jnp.stack(pr), jnp.stack(pi)
        pb_r = pw_r[..., None] * bbr - pw_i[..., None] * bbi
        pb_i = pw_r[..., None] * bbi + pw_i[..., None] * bbr
        cp_r = cr[None] * pw_r[:, :, None, :] - ci[None] * pw_i[:, :, None, :]
        cp_i = cr[None] * pw_i[:, :, None, :] + ci[None] * pw_r[:, :, None, :]
        kk = (jnp.einsum('gip,tgpj->tgij', cr, pb_r[:sub], precision=HI)
              - jnp.einsum('gip,tgpj->tgij', ci, pb_i[:sub], precision=HI))
        s_idx = jnp.arange(sub)
        lag = (s_idx[None, :] - s_idx[:, None]) if dr == 0 else (s_idx[:, None] - s_idx[None, :])
        kt = kk[jnp.clip(lag, 0, sub - 1)]
        kt = jnp.where((lag >= 0)[:, :, None, None, None], kt, 0.0)
        toep = kt.transpose(2, 0, 4, 1, 3)
        g_n = toep.shape[0]
        toep = toep.reshape(g_n, sub * S5_GROUP_SIZE, sub * S5_GROUP_SIZE)
        e_in = (sub - 1 - s_idx) if dr == 0 else s_idx
        win_r = pb_r[e_in].transpose(1, 0, 3, 2).reshape(g_n, sub * S5_GROUP_SIZE, S5_STATE)
        win_i = pb_i[e_in].transpose(1, 0, 3, 2).reshape(g_n, sub * S5_GROUP_SIZE, S5_STATE)
        e_out = (s_idx + 1) if dr == 0 else (sub - s_idx)
        wout_r = cp_r[e_out].transpose(1, 3, 0, 2).reshape(g_n, S5_STATE, sub * S5_GROUP_SIZE)
        wout_i = -cp_i[e_out].transpose(1, 3, 0, 2).reshape(g_n, S5_STATE, sub * S5_GROUP_SIZE)
        outs.append((toep, win_r, win_i, wout_r, wout_i, pw_r[sub], pw_i[sub]))

    def pair(x):
        return x.reshape((x.shape[0] // 2, 2) + x.shape[1:])

    def blockdiag(x):
        z = jnp.zeros_like(x[:, 0])
        top = jnp.concatenate([x[:, 0], z], axis=2)
        bot = jnp.concatenate([z, x[:, 1]], axis=2)
        return jnp.concatenate([top, bot], axis=1)

    toeps, wins, wouts, a16 = [], [], [], []
    for toep, win_r, win_i, wout_r, wout_i, ar, ai in outs:
        toeps.append(blockdiag(pair(toep)))
        wins.append(jnp.concatenate([blockdiag(pair(win_r)), blockdiag(pair(win_i))], axis=2))
        wouts.append(jnp.concatenate([blockdiag(pair(wout_r)), blockdiag(pair(wout_i))], axis=1))
        ar2 = pair(ar).reshape(-1, 1, 2 * S5_STATE)
        ai2 = pair(ai).reshape(-1, 1, 2 * S5_STATE)
        a16.append(jnp.concatenate([ar2, ai2], axis=1))
    stack = lambda xs, dt: jnp.stack(xs, axis=1).astype(dt)
    return stack(wins, BF16), stack(toeps, BF16), stack(wouts, BF16), stack(a16, F32)


def _s5_core(hs, weights, n_ctx):
    b, t, d = hs.shape
    sub, gs = S5_SUB, S5_GROUP_SIZE
    win, toep, wout, a16 = weights
    n_pair = win.shape[0]
    m = t // sub
    assert m % 8 == 0 and (n_ctx // sub) % 8 == 0
    cw = 2 * sub * gs
    u = hs.reshape(b, m, sub, n_pair, 2, gs).transpose(3, 0, 1, 4, 2, 5).reshape(n_pair, b * m, cw)
    kern = functools.partial(_s5_core_kernel, n_b=b, m=m, m_ctx=n_ctx // sub)
    sw = 4 * S5_STATE
    y = pl.pallas_call(
        kern,
        grid=(n_pair,),
        in_specs=[pl.BlockSpec((None, b * m, cw), lambda p: (p, 0, 0)),
                  pl.BlockSpec((None, 2, cw, sw), lambda p: (p, 0, 0, 0)),
                  pl.BlockSpec((None, 2, cw, cw), lambda p: (p, 0, 0, 0)),
                  pl.BlockSpec((None, 2, sw, cw), lambda p: (p, 0, 0, 0)),
                  pl.BlockSpec((None, 2, 2, sw // 2), lambda p: (p, 0, 0, 0))],
        out_specs=pl.BlockSpec((None, b * m, cw), lambda p: (p, 0, 0)),
        out_shape=jax.ShapeDtypeStruct((n_pair, b * m, cw), F32),
        scratch_shapes=[pltpu.VMEM((b * m, sw), F32), pltpu.VMEM((b * m, sw), F32)],
        compiler_params=_cparams(("arbitrary",)),
        name="s5_core",
    )(u, win, toep, wout, a16)
    return y.reshape(n_pair, b, m, 2, sub, gs).transpose(1, 2, 4, 0, 3, 5).reshape(b, t, d)


def _ret_in_kernel(x_ref, mod_ref, g_ref, w_ref, cos_ref, sin_ref, z_ref, h_scr, *, n_rope):
    n = pl.program_id(2)

    @pl.when(n == 0)
    def _():
        h_scr[...] = _norm_mod(x_ref[...], g_ref[...], mod_ref[0:1, :], mod_ref[1:2, :]).astype(BF16)

    z = jnp.dot(h_scr[...], w_ref[...], preferred_element_type=F32)
    roped = (n < n_rope).astype(F32)
    cos = 1.0 + roped * (cos_ref[...] - 1.0)
    sin = roped * sin_ref[...]
    half = cos.shape[1]
    for j in range(z.shape[1] // (2 * half)):
        c0 = 2 * j * half
        x1 = z[:, c0:c0 + half]
        x2 = z[:, c0 + half:c0 + 2 * half]
        z_ref[:, c0:c0 + half] = (x1 * cos - x2 * sin).astype(z_ref.dtype)
        z_ref[:, c0 + half:c0 + 2 * half] = (x2 * cos + x1 * sin).astype(z_ref.dtype)


def _ret_in(x, modsel, g, w, cos, sin, n_rope_cols):
    b, t, d = x.shape
    n_out = w.shape[1]
    tm, tn = ROW_TILE, 512
    half = cos.shape[1]
    kern = functools.partial(_ret_in_kernel, n_rope=n_rope_cols // tn)
    return pl.pallas_call(
        kern,
        grid=(b, t // tm, n_out // tn),
        in_specs=[pl.BlockSpec((None, tm, d), lambda b, i, n: (b, i, 0)),
                  _mod_spec(d),
                  pl.BlockSpec((1, d), lambda b, i, n: (0, 0)),
                  pl.BlockSpec((d, tn), lambda b, i, n: (0, n)),
                  pl.BlockSpec((tm, half), lambda b, i, n: (i, 0)),
                  pl.BlockSpec((tm, half), lambda b, i, n: (i, 0))],
        out_specs=pl.BlockSpec((None, tm, tn), lambda b, i, n: (b, i, n)),
        out_shape=jax.ShapeDtypeStruct((b, t, n_out), BF16),
        scratch_shapes=[pltpu.VMEM((tm, d), BF16)],
        compiler_params=_cparams(("arbitrary", "arbitrary", "arbitrary")),
        name="ret_in",
    )(x, modsel, g.reshape(1, d), w, cos, sin)


def _ret_core_kernel(lg_ref, q_ref, k_ref, v_ref, *rest, rev, qk_scale):
    if rev:
        of_ref, g_ref, o_ref, s_scr = rest
    else:
        o_ref, s_scr = rest
    hh = pl.program_id(1)
    c = pl.program_id(2)
    tc = q_ref.shape[0]

    @pl.when(c == 0)
    def _():
        s_scr[...] = jnp.zeros(s_scr.shape, F32)

    lg = jnp.full((1, 1), lg_ref[hh], F32)
    row = lax.broadcasted_iota(jnp.int32, (tc, tc), 0)
    col = lax.broadcasted_iota(jnp.int32, (tc, tc), 1)
    diff = (col - row) if rev else (row - col)
    inner = jnp.where(diff >= 0, jnp.exp(lg * jnp.maximum(diff, 0).astype(F32)), 0.0) * qk_scale
    pos = lax.broadcasted_iota(jnp.int32, (tc, 1), 0).astype(F32)
    if rev:
        q_dec = jnp.exp(lg * (tc - pos))
        k_dec = jnp.exp(lg * pos) * qk_scale
    else:
        q_dec = jnp.exp(lg * (pos + 1.0))
        k_dec = jnp.exp(lg * (tc - 1.0 - pos)) * qk_scale
    q = q_ref[...]
    k = k_ref[...]
    v = v_ref[...]
    s = lax.dot_general(q, k, (((1,), (1,)), ((), ())), preferred_element_type=F32) * inner
    state = s_scr[...]
    o = (jnp.dot(s.astype(BF16), v, preferred_element_type=F32)
         + jnp.dot((q.astype(F32) * q_dec).astype(BF16), state.astype(BF16), preferred_element_type=F32))
    kd_t = (k.astype(F32) * k_dec).T.astype(BF16)
    s_scr[...] = state * jnp.exp(lg * tc) + jnp.dot(kd_t, v, preferred_element_type=F32)
    if rev:
        tot = of_ref[...] + o
        nrm = tot * lax.rsqrt(jnp.mean(tot * tot, axis=-1, keepdims=True) + NORM_EPS)
        gg = g_ref[...].astype(F32)
        o_ref[...] = (gg * jax.nn.sigmoid(gg) * nrm).astype(o_ref.dtype)
    else:
        o_ref[...] = o


def _ret_core(z, log_gamma, o_fwd, rev):
    b, t, _ = z.shape
    nh = RET_HEADS
    dqk = z.shape[2] // (6 * nh)
    dv = 2 * dqk
    tc = ROW_TILE
    nc = t // tc

    if rev:
        def cmap(c):
            return jnp.where(c == 0, 0, nc - c)
    else:
        def cmap(c):
            return c

    kern = functools.partial(_ret_core_kernel, rev=rev, qk_scale=dqk ** -0.5)
    in_specs = [pl.BlockSpec((None, tc, dqk), lambda b, h, c, lg: (b, cmap(c), h)),
                pl.BlockSpec((None, tc, dqk), lambda b, h, c, lg: (b, cmap(c), nh + h)),
                pl.BlockSpec((None, tc, dv), lambda b, h, c, lg: (b, cmap(c), nh + h))]
    args = [z, z, z]
    if rev:
        in_specs += [pl.BlockSpec((None, tc, dv), lambda b, h, c, lg: (b, cmap(c), h)),
                     pl.BlockSpec((None, tc, dv), lambda b, h, c, lg: (b, cmap(c), 2 * nh + h))]
        args += [o_fwd, z]
    out_dtype = BF16 if rev else F32
    return pl.pallas_call(
        kern,
        grid_spec=pltpu.PrefetchScalarGridSpec(
            num_scalar_prefetch=1,
            grid=(b, nh, nc),
            in_specs=in_specs,
            out_specs=pl.BlockSpec((None, tc, dv), lambda b, h, c, lg: (b, cmap(c), h)),
            scratch_shapes=[pltpu.VMEM((dqk, dv), F32)]),
        out_shape=jax.ShapeDtypeStruct((b, t, nh * dv), out_dtype),
        compiler_params=_cparams(("arbitrary", "arbitrary", "arbitrary")),
        name="ret_core_bwd" if rev else "ret_core_fwd",
    )(log_gamma, *args)


def _final_kernel(x_ref, g_ref, o_ref):
    x = x_ref[...]
    o_ref[...] = x * lax.rsqrt(jnp.mean(x * x, axis=-1, keepdims=True) + NORM_EPS) * g_ref[...]


def _final_norm(x, g, n_ctx):
    b, t, d = x.shape
    tm = ROW_TILE
    skip = n_ctx // tm
    return pl.pallas_call(
        _final_kernel,
        grid=(b, (t - n_ctx) // tm),
        in_specs=[pl.BlockSpec((None, tm, d), lambda b, i: (b, i + skip, 0)),
                  pl.BlockSpec((1, d), lambda b, i: (0, 0))],
        out_specs=pl.BlockSpec((None, tm, d), lambda b, i: (b, i, 0)),
        out_shape=jax.ShapeDtypeStruct((b, t - n_ctx, d), F32),
        compiler_params=_cparams(("arbitrary", "arbitrary")),
        name="final_norm",
    )(x, g.reshape(1, d))


def _rope_tables(n_ctx, n_lat, head_dim):
    rows = n_lat // GRID_W
    row = jnp.repeat(jnp.arange(rows, dtype=F32), GRID_W)
    col = jnp.tile(jnp.arange(GRID_W, dtype=F32), rows)
    n_freq = head_dim // 4
    inv_freq = ROPE_THETA ** (-jnp.arange(n_freq, dtype=F32) / n_freq)
    ang = jnp.concatenate([row[:, None] * inv_freq, col[:, None] * inv_freq], axis=-1)
    cos = jnp.concatenate([jnp.ones((n_ctx, head_dim // 2), F32), jnp.cos(ang)], axis=0)
    sin = jnp.concatenate([jnp.zeros((n_ctx, head_dim // 2), F32), jnp.sin(ang)], axis=0)
    return cos, sin


def kernel(x, c, ctx, c_ctx, mod_w, mod_b, norm_g, attn_w_qkv, attn_w_o, attn_q_gain, attn_k_gain,
           s5_lambda_re, s5_lambda_im, s5_log_dt, s5_b_re, s5_b_im, s5_c_re, s5_c_im, s5_d, s5_w_glu, s5_b_glu,
           ret_w_qkvg, ret_w_o, ret_log_decay, moe_w_router, moe_w_up, moe_w_down, final_g):
    bsz, n_lat, d = x.shape
    n_ctx = ctx.shape[1]
    depth = mod_w.shape[0]
    assert n_ctx == ROW_TILE and n_lat % ROW_TILE == 0 and bsz <= 7

    xs = jnp.concatenate([ctx, x], axis=1).astype(F32)
    cvec = jnp.zeros((8, d), F32).at[:bsz].set(c).at[bsz].set(c_ctx)
    mods = _modulation(cvec, mod_w, mod_b)

    cos_a, sin_a = _rope_tables(n_ctx, n_lat, ATTN_HEAD_DIM)
    cos2 = jnp.concatenate([cos_a, cos_a], axis=1)
    sin2 = jnp.concatenate([-sin_a, sin_a], axis=1)
    cos_r, sin_r = _rope_tables(n_ctx, n_lat, d // RET_HEADS)

    for i in range(depth):
        kind, j = i % N_MIXERS, i // N_MIXERS
        m6 = mods[i].reshape(8, 6, d)
        modsel = jnp.stack([jnp.broadcast_to(m6[bsz], (bsz, 6, d)), m6[:bsz]], axis=1)
        wr_t = moe_w_router[i].T.astype(F32)
        if kind == 0:
            q, k, v = _attn_in(xs, modsel, norm_g[i, 0], attn_w_qkv[j].astype(BF16),
                               attn_q_gain[j], attn_k_gain[j], cos2, sin2)
            o = _flash(q, k, v, n_ctx)
            x_mid, h2, lg = _post_call(_post_plain_kernel, "attn_out", [o], [attn_w_o[j].astype(BF16)],
                                       xs, modsel, norm_g[i, 1], wr_t)
        elif kind == 1:
            hs = _norm_only(xs, modsel, norm_g[i, 0])
            w5 = _s5_weights(s5_lambda_re[j], s5_lambda_im[j], s5_log_dt[j], s5_b_re[j], s5_b_im[j],
                             s5_c_re[j], s5_c_im[j])
            y = _s5_core(hs, w5, n_ctx)
            x_mid, h2, lg = _post_call(_post_glu_kernel, "s5_out", [y, hs],
                                       [s5_d[j].reshape(1, d).astype(F32), s5_w_glu[j].astype(BF16),
                                        s5_b_glu[j].reshape(1, -1).astype(F32)],
                                       xs, modsel, norm_g[i, 1], wr_t)
        else:
            z = _ret_in(xs, modsel, norm_g[i, 0], ret_w_qkvg[j].astype(BF16), cos_r, sin_r, 2 * d)
            o_f = None
            for dr in range(2):
                log_gamma = -jnp.abs(ret_log_decay[j, dr].astype(F32))
                o_f = _ret_core(z, log_gamma, o_f, rev=(dr == 1))
            x_mid, h2, lg = _post_call(_post_plain_kernel, "ret_out", [o_f], [ret_w_o[j].astype(BF16)],
                                       xs, modsel, norm_g[i, 1], wr_t)
        g2 = modsel[:, :, 5:6, :]
        xs = _moe(x_mid, h2, lg, g2, moe_w_up[i].astype(BF16), moe_w_down[i].astype(BF16), n_ctx)

    return _final_norm(xs, final_g, n_ctx).astype(x.dtype)
```

```python
import functools
import math

import jax
import jax.numpy as jnp
from jax import lax
from jax.experimental import pallas as pl
from jax.experimental.pallas import tpu as pltpu

F32 = jnp.float32
BF16 = jnp.bfloat16

GRID_W = 64
N_MIXERS = 3
NORM_EPS = 1e-6
ROPE_THETA = 10000.0
ATTN_HEAD_DIM = 128
ATTN_GROUP = 4
S5_GROUP_SIZE = 16
S5_STATE = 64
S5_MIN_DECAY = 1e-4
RET_HEADS = 4
N_EXPERTS = 16
EC_CAPACITY_FACTOR = 2

LANES = 128
ROW_TILE = 256
S5_SUB = 8
VMEM_LIMIT = 48 * 1024 * 1024
HI = lax.Precision.HIGHEST


def _cparams(sem):
    return pltpu.CompilerParams(dimension_semantics=sem, vmem_limit_bytes=VMEM_LIMIT)


def _norm_mod(x, g, shift, scale):
    ms = jnp.mean(x * x, axis=-1, keepdims=True)
    return (x * lax.rsqrt(ms + NORM_EPS) * g) * (1.0 + scale) + shift


def _largest_divisor(n, cap, mult):
    best = None
    for t in range(mult, min(n, cap) + 1, mult):
        if n % t == 0:
            best = t
    assert best is not None, (n, cap, mult)
    return best


def _mod_kernel(c_ref, w_ref, b_ref, o_ref):
    c = c_ref[...]
    s = c * jax.nn.sigmoid(c)
    o_ref[...] = jnp.dot(s, w_ref[...], precision=HI, preferred_element_type=F32) + b_ref[...]


def _modulation(cvec, mod_w, mod_b):
    depth, d, n = mod_w.shape
    tn = _largest_divisor(n, 1536, 128)
    return pl.pallas_call(
        _mod_kernel,
        grid=(depth, n // tn),
        in_specs=[pl.BlockSpec((8, d), lambda l, j: (0, 0)),
                  pl.BlockSpec((None, d, tn), lambda l, j: (l, 0, j)),
                  pl.BlockSpec((None, 1, tn), lambda l, j: (l, 0, j))],
        out_specs=pl.BlockSpec((None, 8, tn), lambda l, j: (l, 0, j)),
        out_shape=jax.ShapeDtypeStruct((depth, 8, n), F32),
        compiler_params=_cparams(("arbitrary", "arbitrary")),
        name="modulation",
    )(cvec, mod_w, mod_b.reshape(depth, 1, n))


def _mod_spec(d):
    return pl.BlockSpec((None, None, 6, d), lambda b, i, *_: (b, jnp.minimum(i, 1), 0, 0))


def _attn_in_kernel(x_ref, mod_ref, g_ref, w_ref, qg_ref, kg_ref, cos_ref, sin_ref,
                    q_ref, k_ref, v_ref, *, n_q, n_kv):
    hd = ATTN_HEAD_DIM
    h = _norm_mod(x_ref[...], g_ref[...], mod_ref[0:1, :], mod_ref[1:2, :]).astype(BF16)
    qkv = jnp.dot(h, w_ref[...], preferred_element_type=F32)
    cos, sin = cos_ref[...], sin_ref[...]

    def norm_rope(t, gain):
        t = t * lax.rsqrt(jnp.mean(t * t, axis=-1, keepdims=True) + NORM_EPS) * gain
        return t * cos + pltpu.roll(t, hd // 2, axis=1) * sin

    scale = hd ** -0.5 * math.log2(math.e)
    for j in range(n_q):
        q_ref[j] = (norm_rope(qkv[:, j * hd:(j + 1) * hd], qg_ref[...]) * scale).astype(BF16)
    for j in range(n_kv):
        c0 = (n_q + j) * hd
        k_ref[j] = norm_rope(qkv[:, c0:c0 + hd], kg_ref[...]).astype(BF16)
        c1 = (n_q + n_kv + j) * hd
        v_ref[j] = qkv[:, c1:c1 + hd].astype(BF16)


def _attn_in(x, modsel, g, w_qkv, q_gain, k_gain, cos2, sin2):
    b, t, d = x.shape
    hd = ATTN_HEAD_DIM
    n_tot = w_qkv.shape[1] // hd
    n_q = d // hd
    n_kv = (n_tot - n_q) // 2
    tm = ROW_TILE
    kern = functools.partial(_attn_in_kernel, n_q=n_q, n_kv=n_kv)
    return pl.pallas_call(
        kern,
        grid=(b, t // tm),
        in_specs=[pl.BlockSpec((None, tm, d), lambda b, i: (b, i, 0)),
                  _mod_spec(d),
                  pl.BlockSpec((1, d), lambda b, i: (0, 0)),
                  pl.BlockSpec(w_qkv.shape, lambda b, i: (0, 0)),
                  pl.BlockSpec((1, hd), lambda b, i: (0, 0)),
                  pl.BlockSpec((1, hd), lambda b, i: (0, 0)),
                  pl.BlockSpec((tm, hd), lambda b, i: (i, 0)),
                  pl.BlockSpec((tm, hd), lambda b, i: (i, 0))],
        out_specs=[pl.BlockSpec((None, n_q, tm, hd), lambda b, i: (b, 0, i, 0)),
                   pl.BlockSpec((None, n_kv, tm, hd), lambda b, i: (b, 0, i, 0)),
                   pl.BlockSpec((None, n_kv, tm, hd), lambda b, i: (b, 0, i, 0))],
        out_shape=[jax.ShapeDtypeStruct((b, n_q, t, hd), BF16),
                   jax.ShapeDtypeStruct((b, n_kv, t, hd), BF16),
                   jax.ShapeDtypeStruct((b, n_kv, t, hd), BF16)],
        compiler_params=_cparams(("arbitrary", "arbitrary")),
        name="attn_in",
    )(x, modsel, g.reshape(1, d), w_qkv, q_gain.reshape(1, hd), k_gain.reshape(1, hd), cos2, sin2)


def _flash_kernel(q_ref, k_ref, v_ref, o_ref, s0_scr, s1_scr, m_scr, l_scr, acc_scr, *, tq, n_ctx):
    hd = ATTN_HEAD_DIM
    qi = pl.program_id(2)
    j = pl.program_id(3)
    nk = pl.num_programs(3) - 1
    m_init = -1e30

    @pl.when(j == 0)
    def _():
        m_scr[...] = jnp.full(m_scr.shape, m_init, F32)
        l_scr[...] = jnp.zeros(l_scr.shape, F32)
        acc_scr[...] = jnp.zeros(acc_scr.shape, F32)
        s1_scr[...] = jnp.full(s1_scr.shape, -jnp.inf, F32)

    def softmax_update(s, v):
        m_prev = m_scr[...]
        m_new = jnp.maximum(m_prev, jnp.max(s, axis=1, keepdims=True))
        alpha = jnp.exp2(m_prev - m_new)
        p = jnp.exp2(s - m_new)
        l_scr[...] = alpha * l_scr[...] + jnp.sum(p, axis=1, keepdims=True)
        acc_scr[...] = alpha * acc_scr[...] + jnp.dot(p.astype(BF16), v, preferred_element_type=F32)
        m_scr[...] = m_new

    def scores(k):
        q = q_ref[...].reshape(ATTN_GROUP * tq, hd)
        return lax.dot_general(q, k, (((1,), (1,)), ((), ())), preferred_element_type=F32)

    def pipelined(s_new, s_old):
        s_new[...] = scores(k_ref[...])
        softmax_update(s_old[...], v_ref[...])

    latent = qi > 0
    odd = jnp.bitwise_and(j, 1)

    @pl.when(jnp.logical_and(latent, odd == 0))
    def _():
        pipelined(s0_scr, s1_scr)

    @pl.when(jnp.logical_and(latent, odd == 1))
    def _():
        pipelined(s1_scr, s0_scr)

    @pl.when(jnp.logical_and(qi == 0, j == 0))
    def _():
        softmax_update(scores(k_ref[0:n_ctx, :]), v_ref[0:n_ctx, :])

    @pl.when(j == nk)
    def _():
        o = acc_scr[...] / l_scr[...]
        for g in range(ATTN_GROUP):
            o_ref[:, g * hd:(g + 1) * hd] = o[g * tq:(g + 1) * tq, :].astype(o_ref.dtype)


def _flash(q, k, v, n_ctx):
    b, n_q, t, hd = q.shape
    n_kv = k.shape[1]
    tq = ROW_TILE
    assert n_ctx == tq and n_q == n_kv * ATTN_GROUP
    tk = _largest_divisor(t, 1280, 256)
    nk = t // tk
    gw = ATTN_GROUP * hd
    rows = ATTN_GROUP * tq
    kern = functools.partial(_flash_kernel, tq=tq, n_ctx=n_ctx)

    def k_map(b, h, i, j):
        return (b, h, jnp.where(i == 0, 0, jnp.minimum(j, nk - 1)), 0)

    def v_map(b, h, i, j):
        return (b, h, jnp.where(i == 0, 0, jnp.maximum(j - 1, 0)), 0)

    return pl.pallas_call(
        kern,
        grid=(b, n_kv, t // tq, nk + 1),
        in_specs=[pl.BlockSpec((None, ATTN_GROUP, tq, hd), lambda b, h, i, j: (b, h, i, 0)),
                  pl.BlockSpec((None, None, tk, hd), k_map),
                  pl.BlockSpec((None, None, tk, hd), v_map)],
        out_specs=pl.BlockSpec((None, tq, gw), lambda b, h, i, j: (b, i, h)),
        out_shape=jax.ShapeDtypeStruct((b, t, n_q * hd), BF16),
        scratch_shapes=[pltpu.VMEM((rows, tk), F32),
                        pltpu.VMEM((rows, tk), F32),
                        pltpu.VMEM((rows, 1), F32),
                        pltpu.VMEM((rows, 1), F32),
                        pltpu.VMEM((rows, hd), F32)],
        compiler_params=_cparams(("arbitrary", "arbitrary", "arbitrary", "arbitrary")),
        name="flash_attn",
    )(q, k, v)


def _post_tail(x_new, mod_ref, g2_ref, wr_ref, x_out, h_out, lg_out):
    x_out[...] = x_new
    h2 = _norm_mod(x_new, g2_ref[...], mod_ref[3:4, :], mod_ref[4:5, :])
    h_out[...] = h2.astype(BF16)
    lg_out[...] = lax.dot_general(wr_ref[...], h2, (((1,), (1,)), ((), ())),
                                  precision=HI, preferred_element_type=F32)


def _post_plain_kernel(o_ref, w_ref, x_ref, mod_ref, g2_ref, wr_ref, x_out, h_out, lg_out):
    y = jnp.dot(o_ref[...], w_ref[...], preferred_element_type=F32)
    _post_tail(x_ref[...] + mod_ref[2:3, :] * y, mod_ref, g2_ref, wr_ref, x_out, h_out, lg_out)


def _post_glu_kernel(y_ref, hs_ref, dsk_ref, w_ref, b_ref, x_ref, mod_ref, g2_ref, wr_ref,
                     x_out, h_out, lg_out):
    d = x_ref.shape[-1]
    slabs = [hs_ref[o].astype(F32) for o in range(hs_ref.shape[0])]
    ys = [y_ref[0, o].astype(F32) + y_ref[1, o].astype(F32) for o in range(hs_ref.shape[0])]
    yt = dsk_ref[...] * jnp.concatenate(slabs, axis=1) + jnp.concatenate(ys, axis=1)
    z = jnp.dot(jax.nn.gelu(yt).astype(BF16), w_ref[...], preferred_element_type=F32) + b_ref[...]
    y = z[:, :d] * jax.nn.sigmoid(z[:, d:])
    _post_tail(x_ref[...] + mod_ref[2:3, :] * y, mod_ref, g2_ref, wr_ref, x_out, h_out, lg_out)


def _post_call(kern, name, row_inputs, row_specs, const_inputs, x, modsel, g2, w_router_t):
    b, t, d = x.shape
    tm = ROW_TILE
    ne = w_router_t.shape[0]
    const_specs = [pl.BlockSpec(a.shape, lambda b, i: (0, 0)) for a in const_inputs]
    return pl.pallas_call(
        kern,
        grid=(b, t // tm),
        in_specs=row_specs + const_specs + [
            pl.BlockSpec((None, tm, d), lambda b, i: (b, i, 0)),
            _mod_spec(d),
            pl.BlockSpec((1, d), lambda b, i: (0, 0)),
            pl.BlockSpec((ne, d), lambda b, i: (0, 0))],
        out_specs=[pl.BlockSpec((None, tm, d), lambda b, i: (b, i, 0)),
                   pl.BlockSpec((None, tm, d), lambda b, i: (b, i, 0)),
                   pl.BlockSpec((None, ne, tm), lambda b, i: (b, 0, i))],
        out_shape=[jax.ShapeDtypeStruct((b, t, d), F32),
                   jax.ShapeDtypeStruct((b, t, d), BF16),
                   jax.ShapeDtypeStruct((b, ne, t), F32)],
        compiler_params=_cparams(("arbitrary", "arbitrary")),
        name=name,
    )(*row_inputs, *const_inputs, x, modsel, g2.reshape(1, d), w_router_t)


def _ffn_kernel(x_ref, wa_ref, wu_ref, wd_ref, gate_ref, o_ref):
    f = pl.program_id(2)

    @pl.when(f == 0)
    def _():
        o_ref[...] = jnp.zeros(o_ref.shape, F32)

    x = x_ref[...]
    a = jnp.dot(x, wa_ref[...], preferred_element_type=F32)
    u = jnp.dot(x, wu_ref[...], preferred_element_type=F32)
    hmid = (a * jax.nn.sigmoid(a) * u).astype(BF16)
    o_ref[...] += jnp.dot(hmid, wd_ref[...], preferred_element_type=F32)

    @pl.when(f == pl.num_programs(2) - 1)
    def _():
        o_ref[...] = o_ref[...] * gate_ref[...]


def _expert_ffn(xe, w_up, w_down, gate):
    ne, r, d = xe.shape
    ff = w_down.shape[1]
    tm = _largest_divisor(r, 1040, 16)
    tf = _largest_divisor(ff, 256, 128)
    nf = ff // tf
    return pl.pallas_call(
        _ffn_kernel,
        grid=(ne, r // tm, nf),
        in_specs=[pl.BlockSpec((None, tm, d), lambda e, m, f: (e, m, 0)),
                  pl.BlockSpec((None, d, tf), lambda e, m, f: (e, 0, f)),
                  pl.BlockSpec((None, d, tf), lambda e, m, f: (e, 0, nf + f)),
                  pl.BlockSpec((None, tf, d), lambda e, m, f: (e, f, 0)),
                  pl.BlockSpec((None, tm, 1), lambda e, m, f: (e, m, 0))],
        out_specs=pl.BlockSpec((None, tm, d), lambda e, m, f: (e, m, 0)),
        out_shape=jax.ShapeDtypeStruct((ne, r, d), F32),
        compiler_params=_cparams(("arbitrary", "arbitrary", "arbitrary")),
        name="expert_ffn",
    )(xe, w_up, w_up, w_down, gate)


def _moe(x_mid, h2, logits_t, g2, w_up, w_down, n_ctx):
    b, t, d = x_mid.shape
    ne = logits_t.shape[1]
    aff = jax.nn.softmax(logits_t, axis=1)
    base = (jnp.arange(b, dtype=jnp.int32) * t)[:, None, None]

    def routed(a, offset):
        cap = EC_CAPACITY_FACTOR * a.shape[-1] // ne
        gate, idx = lax.top_k(a, cap)
        flat = idx.astype(jnp.int32) + base + offset
        return (gate.transpose(1, 0, 2).reshape(ne, -1), flat.transpose(1, 0, 2).reshape(ne, -1))

    gate_l, idx_l = routed(aff[:, :, n_ctx:], n_ctx)
    gate_c, idx_c = routed(aff[:, :, :n_ctx], 0)
    gate = jnp.concatenate([gate_l, gate_c], axis=1)
    idx = jnp.concatenate([idx_l, idx_c], axis=1)
    h_flat = h2.reshape(b * t, d)
    xe = jnp.take(h_flat, idx, axis=0)
    y = _expert_ffn(xe, w_up, w_down, gate[..., None])
    moe = jnp.zeros((b * t, d), F32).at[idx.reshape(-1)].add(y.reshape(-1, d))
    moe = moe.reshape(b, t, d)
    gsel = jnp.concatenate([jnp.broadcast_to(g2[:, 0], (b, n_ctx, d)),
                            jnp.broadcast_to(g2[:, 1], (b, t - n_ctx, d))], axis=1)
    return x_mid + gsel * moe


def _norm_slab_kernel(x_ref, mod_ref, g_ref, h_ref):
    h = _norm_mod(x_ref[...], g_ref[...], mod_ref[0:1, :], mod_ref[1:2, :]).astype(h_ref.dtype)
    for o in range(h_ref.shape[0]):
        h_ref[o] = h[:, o * LANES:(o + 1) * LANES]


def _norm_slabs(x, modsel, g):
    b, t, d = x.shape
    tm = ROW_TILE
    n_slab = d // LANES
    return pl.pallas_call(
        _norm_slab_kernel,
        grid=(b, t // tm),
        in_specs=[pl.BlockSpec((None, tm, d), lambda b, i: (b, i, 0)),
                  _mod_spec(d),
                  pl.BlockSpec((1, d), lambda b, i: (0, 0))],
        out_specs=pl.BlockSpec((None, n_slab, tm, LANES), lambda b, i: (b, 0, i, 0)),
        out_shape=jax.ShapeDtypeStruct((b, n_slab, t, LANES), BF16),
        compiler_params=_cparams(("arbitrary", "arbitrary")),
        name="norm_mod",
    )(x, modsel, g.reshape(1, d))


def _s5_core_kernel(u_ref, win_ref, toep_ref, wout_ref, a_ref, y_ref, zh_scr, *, m_ctx):
    dr = pl.program_id(2)
    m = u_ref.shape[0]
    rc = _largest_divisor(m, 512, 16)
    half = zh_scr.shape[1] // 2
    for r0 in range(0, m, rc):
        zh_scr[r0:r0 + rc, :] = jnp.dot(u_ref[r0:r0 + rc, :], win_ref[...], preferred_element_type=F32)
    ar = a_ref[0:1, :]
    ai = a_ref[1:2, :]
    rid = lax.broadcasted_iota(jnp.int32, (8, half), 0)

    def visit(blk, carry, reverse):
        sr, si = carry
        base = pl.multiple_of(blk * 8, 8)
        z8 = zh_scr[pl.ds(base, 8), :]
        hr = jnp.zeros((8, half), F32)
        hi = jnp.zeros((8, half), F32)
        for r in (range(7, -1, -1) if reverse else range(8)):
            hr = jnp.where(rid == r, sr, hr)
            hi = jnp.where(rid == r, si, hi)
            zr = z8[r:r + 1, 0:half]
            zi = z8[r:r + 1, half:2 * half]
            sr, si = ar * sr - ai * si + zr, ar * si + ai * sr + zi
        zh_scr[pl.ds(base, 8), 0:half] = hr
        zh_scr[pl.ds(base, 8), half:2 * half] = hi
        return sr, si

    zero = (jnp.zeros((1, half), F32), jnp.zeros((1, half), F32))
    nb, nb_ctx = m // 8, m_ctx // 8

    @pl.when(dr == 0)
    def _():
        lax.fori_loop(0, nb, lambda s, cr: visit(s, cr, False), zero)

    @pl.when(dr == 1)
    def _():
        carry = lax.fori_loop(0, nb_ctx, lambda s, cr: visit(nb_ctx - 1 - s, cr, True), zero)
        lax.fori_loop(0, nb - nb_ctx, lambda s, cr: visit(nb - 1 - s, cr, True), carry)

    for r0 in range(0, m, rc):
        y = (jnp.dot(u_ref[r0:r0 + rc, :], toep_ref[...], preferred_element_type=F32)
             + jnp.dot(zh_scr[r0:r0 + rc, :].astype(BF16), wout_ref[...], preferred_element_type=F32))
        y_ref[r0:r0 + rc, :] = y.astype(y_ref.dtype)


def _s5_weights(lam_re, lam_im, log_dt, b_re, b_im, c_re, c_im):
    sub = S5_SUB
    gs = S5_GROUP_SIZE
    gps = LANES // gs
    outs = []
    for dr in range(2):
        lre = jnp.minimum(lam_re[dr].astype(F32), -S5_MIN_DECAY)
        lim = lam_im[dr].astype(F32)
        dt = jnp.exp(log_dt[dr].astype(F32))[:, None]
        mag = jnp.exp(lre * dt)
        ang = lim * dt
        lbr, lbi = mag * jnp.cos(ang), mag * jnp.sin(ang)
        den = lre * lre + lim * lim
        f_re = ((lbr - 1) * lre + lbi * lim) / den
        f_im = (lbi * lre - (lbr - 1) * lim) / den
        bre, bim = b_re[dr].astype(F32), b_im[dr].astype(F32)
        bbr = f_re[..., None] * bre - f_im[..., None] * bim
        bbi = f_re[..., None] * bim + f_im[..., None] * bre
        cr, ci = c_re[dr].astype(F32), c_im[dr].astype(F32)
        pr, pi = [jnp.ones_like(lbr)], [jnp.zeros_like(lbr)]
        for _ in range(sub):
            pr.append(pr[-1] * lbr - pi[-1] * lbi)
            pi.append(pr[-2] * lbi + pi[-1] * lbr)
        pw_r, pw_i = jnp.stack(pr), jnp.stack(pi)
        pb_r = pw_r[..., None] * bbr - pw_i[..., None] * bbi
        pb_i = pw_r[..., None] * bbi + pw_i[..., None] * bbr
        cp_r = cr[None] * pw_r[:, :, None, :] - ci[None] * pw_i[:, :, None, :]
        cp_i = cr[None] * pw_i[:, :, None, :] + ci[None] * pw_r[:, :, None, :]
        kk = (jnp.einsum('gip,tgpj->tgij', cr, pb_r[:sub], precision=HI)
              - jnp.einsum('gip,tgpj->tgij', ci, pb_i[:sub], precision=HI))
        s_idx = jnp.arange(sub)
        lag = (s_idx[None, :] - s_idx[:, None]) if dr == 0 else (s_idx[:, None] - s_idx[None, :])
        kt = kk[jnp.clip(lag, 0, sub - 1)]
        kt = jnp.where((lag >= 0)[:, :, None, None, None], kt, 0.0)
        g_n = kt.shape[2]
        n_slab = g_n // gps
        eye = jnp.eye(gps, dtype=F32)
        toep = jnp.einsum('stogij,gh->osgjthi', kt.reshape(sub, sub, n_slab, gps, gs, gs), eye)
        toep = toep.reshape(n_slab, sub * LANES, sub * LANES)
        e_in = (sub - 1 - s_idx) if dr == 0 else s_idx

        def in_map(pb):
            w = jnp.einsum('sogpj,gh->osgjhp', pb[e_in].reshape(sub, n_slab, gps, S5_STATE, gs), eye)
            return w.reshape(n_slab, sub * LANES, gps * S5_STATE)

        win = jnp.concatenate([in_map(pb_r), in_map(pb_i)], axis=2)
        e_out = (s_idx + 1) if dr == 0 else (sub - s_idx)

        def out_map(cp):
            w = jnp.einsum('togip,gh->ogpthi', cp[e_out].reshape(sub, n_slab, gps, gs, S5_STATE), eye)
            return w.reshape(n_slab, gps * S5_STATE, sub * LANES)

        wout = jnp.concatenate([out_map(cp_r), -out_map(cp_i)], axis=1)
        a_sub = jnp.stack([pw_r[sub].reshape(n_slab, gps * S5_STATE),
                           pw_i[sub].reshape(n_slab, gps * S5_STATE)], axis=1)
        outs.append((win, toep, wout, a_sub))

    stack = lambda k, dt: jnp.stack([o[k] for o in outs], axis=1).astype(dt)
    return stack(0, BF16), stack(1, BF16), stack(2, BF16), stack(3, F32)


def _s5_core(hs, weights, n_ctx):
    b, n_slab, t, _ = hs.shape
    sub = S5_SUB
    win, toep, wout, a_sub = weights
    m = t // sub
    m_ctx = n_ctx // sub
    assert m % 8 == 0 and m_ctx % 8 == 0
    cw = sub * LANES
    sw = win.shape[-1]
    u = hs.reshape(b, n_slab, m, cw)
    kern = functools.partial(_s5_core_kernel, m_ctx=m_ctx)
    y = pl.pallas_call(
        kern,
        grid=(n_slab, b, 2),
        in_specs=[pl.BlockSpec((None, None, m, cw), lambda o, b, d: (b, o, 0, 0)),
                  pl.BlockSpec((None, None, cw, sw), lambda o, b, d: (o, d, 0, 0)),
                  pl.BlockSpec((None, None, cw, cw), lambda o, b, d: (o, d, 0, 0)),
                  pl.BlockSpec((None, None, sw, cw), lambda o, b, d: (o, d, 0, 0)),
                  pl.BlockSpec((None, None, 2, sw // 2), lambda o, b, d: (o, d, 0, 0))],
        out_specs=pl.BlockSpec((None, None, None, m, cw), lambda o, b, d: (d, b, o, 0, 0)),
        out_shape=jax.ShapeDtypeStruct((2, b, n_slab, m, cw), BF16),
        scratch_shapes=[pltpu.VMEM((m, sw), F32)],
        compiler_params=_cparams(("arbitrary", "arbitrary", "arbitrary")),
        name="s5_core",
    )(u, win, toep, wout, a_sub)
    return y.reshape(2, b, n_slab, t, LANES)


def _ret_in_kernel(x_ref, mod_ref, g_ref, w_ref, cos_ref, sin_ref, z_ref, *, n_rope_cols):
    h = _norm_mod(x_ref[...], g_ref[...], mod_ref[0:1, :], mod_ref[1:2, :]).astype(BF16)
    cos, sin = cos_ref[...], sin_ref[...]
    half = cos.shape[1]
    tn = 4 * half
    for n0 in range(0, w_ref.shape[1], tn):
        z = jnp.dot(h, w_ref[:, n0:n0 + tn], preferred_element_type=F32)
        if n0 < n_rope_cols:
            for c0 in range(0, tn, 2 * half):
                x1 = z[:, c0:c0 + half]
                x2 = z[:, c0 + half:c0 + 2 * half]
                z_ref[:, n0 + c0:n0 + c0 + half] = (x1 * cos - x2 * sin).astype(z_ref.dtype)
                z_ref[:, n0 + c0 + half:n0 + c0 + 2 * half] = (x2 * cos + x1 * sin).astype(z_ref.dtype)
        else:
            z_ref[:, n0:n0 + tn] = z.astype(z_ref.dtype)


def _ret_in(x, modsel, g, w, cos, sin, n_rope_cols):
    b, t, d = x.shape
    n_out = w.shape[1]
    tm = ROW_TILE
    half = cos.shape[1]
    assert n_out % (4 * half) == 0 and n_rope_cols % (4 * half) == 0
    kern = functools.partial(_ret_in_kernel, n_rope_cols=n_rope_cols)
    return pl.pallas_call(
        kern,
        grid=(b, t // tm),
        in_specs=[pl.BlockSpec((None, tm, d), lambda b, i: (b, i, 0)),
                  _mod_spec(d),
                  pl.BlockSpec((1, d), lambda b, i: (0, 0)),
                  pl.BlockSpec((d, n_out), lambda b, i: (0, 0), pipeline_mode=pl.Buffered(1)),
                  pl.BlockSpec((tm, half), lambda b, i: (i, 0)),
                  pl.BlockSpec((tm, half), lambda b, i: (i, 0))],
        out_specs=pl.BlockSpec((None, tm, n_out), lambda b, i: (b, i, 0)),
        out_shape=jax.ShapeDtypeStruct((b, t, n_out), BF16),
        compiler_params=_cparams(("arbitrary", "arbitrary")),
        name="ret_in",
    )(x, modsel, g.reshape(1, d), w, cos, sin)


def _ret_core_kernel(lg_ref, q_ref, k_ref, v_ref, *rest, rev, qk_scale):
    if rev:
        of_ref, g_ref, o_ref, s_scr = rest
    else:
        o_ref, s_scr = rest
    hh = pl.program_id(1)
    c = pl.program_id(2)
    tc = q_ref.shape[0]

    @pl.when(c == 0)
    def _():
        s_scr[...] = jnp.zeros(s_scr.shape, F32)

    lg = jnp.full((1, 1), lg_ref[hh], F32)
    row = lax.broadcasted_iota(jnp.int32, (tc, tc), 0)
    col = lax.broadcasted_iota(jnp.int32, (tc, tc), 1)
    diff = (col - row) if rev else (row - col)
    inner = jnp.where(diff >= 0, jnp.exp(lg * jnp.maximum(diff, 0).astype(F32)), 0.0) * qk_scale
    pos = lax.broadcasted_iota(jnp.int32, (tc, 1), 0).astype(F32)
    if rev:
        q_dec = jnp.exp(lg * (tc - pos))
        k_dec = jnp.exp(lg * pos) * qk_scale
    else:
        q_dec = jnp.exp(lg * (pos + 1.0))
        k_dec = jnp.exp(lg * (tc - 1.0 - pos)) * qk_scale
    q = q_ref[...]
    k = k_ref[...]
    v = v_ref[...]
    s = lax.dot_general(q, k, (((1,), (1,)), ((), ())), preferred_element_type=F32) * inner
    state = s_scr[...]
    o = (jnp.dot(s.astype(BF16), v, preferred_element_type=F32)
         + jnp.dot((q.astype(F32) * q_dec).astype(BF16), state.astype(BF16), preferred_element_type=F32))
    kd_t = (k.astype(F32) * k_dec).T.astype(BF16)
    s_scr[...] = state * jnp.exp(lg * tc) + jnp.dot(kd_t, v, preferred_element_type=F32)
    if rev:
        tot = of_ref[...] + o
        nrm = tot * lax.rsqrt(jnp.mean(tot * tot, axis=-1, keepdims=True) + NORM_EPS)
        gg = g_ref[...].astype(F32)
        o_ref[...] = (gg * jax.nn.sigmoid(gg) * nrm).astype(o_ref.dtype)
    else:
        o_ref[...] = o


def _ret_core(z, log_gamma, o_fwd, rev):
    b, t, _ = z.shape
    nh = RET_HEADS
    dqk = z.shape[2] // (6 * nh)
    dv = 2 * dqk
    tc = ROW_TILE
    nc = t // tc

    if rev:
        def cmap(c):
            return jnp.where(c == 0, 0, nc - c)
    else:
        def cmap(c):
            return c

    kern = functools.partial(_ret_core_kernel, rev=rev, qk_scale=dqk ** -0.5)
    in_specs = [pl.BlockSpec((None, tc, dqk), lambda b, h, c, lg: (b, cmap(c), h)),
                pl.BlockSpec((None, tc, dqk), lambda b, h, c, lg: (b, cmap(c), nh + h)),
                pl.BlockSpec((None, tc, dv), lambda b, h, c, lg: (b, cmap(c), nh + h))]
    args = [z, z, z]
    if rev:
        in_specs += [pl.BlockSpec((None, tc, dv), lambda b, h, c, lg: (b, cmap(c), h)),
                     pl.BlockSpec((None, tc, dv), lambda b, h, c, lg: (b, cmap(c), 2 * nh + h))]
        args += [o_fwd, z]
    out_dtype = BF16 if rev else F32
    return pl.pallas_call(
        kern,
        grid_spec=pltpu.PrefetchScalarGridSpec(
            num_scalar_prefetch=1,
            grid=(b, nh, nc),
            in_specs=in_specs,
            out_specs=pl.BlockSpec((None, tc, dv), lambda b, h, c, lg: (b, cmap(c), h)),
            scratch_shapes=[pltpu.VMEM((dqk, dv), F32)]),
        out_shape=jax.ShapeDtypeStruct((b, t, nh * dv), out_dtype),
        compiler_params=_cparams(("arbitrary", "arbitrary", "arbitrary")),
        name="ret_core_bwd" if rev else "ret_core_fwd",
    )(log_gamma, *args)


def _final_kernel(x_ref, g_ref, o_ref):
    x = x_ref[...]
    o_ref[...] = x * lax.rsqrt(jnp.mean(x * x, axis=-1, keepdims=True) + NORM_EPS) * g_ref[...]


def _final_norm(x, g, n_ctx):
    b, t, d = x.shape
    tm = ROW_TILE
    skip = n_ctx // tm
    return pl.pallas_call(
        _final_kernel,
        grid=(b, (t - n_ctx) // tm),
        in_specs=[pl.BlockSpec((None, tm, d), lambda b, i: (b, i + skip, 0)),
                  pl.BlockSpec((1, d), lambda b, i: (0, 0))],
        out_specs=pl.BlockSpec((None, tm, d), lambda b, i: (b, i, 0)),
        out_shape=jax.ShapeDtypeStruct((b, t - n_ctx, d), F32),
        compiler_params=_cparams(("arbitrary", "arbitrary")),
        name="final_norm",
    )(x, g.reshape(1, d))


def _rope_tables(n_ctx, n_lat, head_dim):
    rows = n_lat // GRID_W
    row = jnp.repeat(jnp.arange(rows, dtype=F32), GRID_W)
    col = jnp.tile(jnp.arange(GRID_W, dtype=F32), rows)
    n_freq = head_dim // 4
    inv_freq = ROPE_THETA ** (-jnp.arange(n_freq, dtype=F32) / n_freq)
    ang = jnp.concatenate([row[:, None] * inv_freq, col[:, None] * inv_freq], axis=-1)
    cos = jnp.concatenate([jnp.ones((n_ctx, head_dim // 2), F32), jnp.cos(ang)], axis=0)
    sin = jnp.concatenate([jnp.zeros((n_ctx, head_dim // 2), F32), jnp.sin(ang)], axis=0)
    return cos, sin


def kernel(x, c, ctx, c_ctx, mod_w, mod_b, norm_g, attn_w_qkv, attn_w_o, attn_q_gain, attn_k_gain,
           s5_lambda_re, s5_lambda_im, s5_log_dt, s5_b_re, s5_b_im, s5_c_re, s5_c_im, s5_d, s5_w_glu, s5_b_glu,
           ret_w_qkvg, ret_w_o, ret_log_decay, moe_w_router, moe_w_up, moe_w_down, final_g):
    bsz, n_lat, d = x.shape
    n_ctx = ctx.shape[1]
    depth = mod_w.shape[0]
    assert n_ctx == ROW_TILE and n_lat % ROW_TILE == 0 and bsz <= 7

    xs = jnp.concatenate([ctx, x], axis=1).astype(F32)
    cvec = jnp.zeros((8, d), F32).at[:bsz].set(c).at[bsz].set(c_ctx)
    mods = _modulation(cvec, mod_w, mod_b)

    cos_a, sin_a = _rope_tables(n_ctx, n_lat, ATTN_HEAD_DIM)
    cos2 = jnp.concatenate([cos_a, cos_a], axis=1)
    sin2 = jnp.concatenate([-sin_a, sin_a], axis=1)
    cos_r, sin_r = _rope_tables(n_ctx, n_lat, d // RET_HEADS)

    tm = ROW_TILE

    def row_spec(width):
        return pl.BlockSpec((None, tm, width), lambda b, i: (b, i, 0))

    for i in range(depth):
        kind, j = i % N_MIXERS, i // N_MIXERS
        m6 = mods[i].reshape(8, 6, d)
        modsel = jnp.stack([jnp.broadcast_to(m6[bsz], (bsz, 6, d)), m6[:bsz]], axis=1)
        wr_t = moe_w_router[i].T.astype(F32)
        if kind == 0:
            q, k, v = _attn_in(xs, modsel, norm_g[i, 0], attn_w_qkv[j].astype(BF16),
                               attn_q_gain[j], attn_k_gain[j], cos2, sin2)
            o = _flash(q, k, v, n_ctx)
            x_mid, h2, lg = _post_call(_post_plain_kernel, "attn_out", [o], [row_spec(o.shape[-1])],
                                       [attn_w_o[j].astype(BF16)], xs, modsel, norm_g[i, 1], wr_t)
        elif kind == 1:
            hs = _norm_slabs(xs, modsel, norm_g[i, 0])
            w5 = _s5_weights(s5_lambda_re[j], s5_lambda_im[j], s5_log_dt[j], s5_b_re[j], s5_b_im[j],
                             s5_c_re[j], s5_c_im[j])
            y = _s5_core(hs, w5, n_ctx)
            n_slab = d // LANES
            slab_specs = [pl.BlockSpec((2, None, n_slab, tm, LANES), lambda b, i: (0, b, 0, i, 0)),
                          pl.BlockSpec((None, n_slab, tm, LANES), lambda b, i: (b, 0, i, 0))]
            x_mid, h2, lg = _post_call(_post_glu_kernel, "s5_out", [y, hs], slab_specs,
                                       [s5_d[j].reshape(1, d).astype(F32), s5_w_glu[j].astype(BF16),
                                        s5_b_glu[j].reshape(1, -1).astype(F32)],
                                       xs, modsel, norm_g[i, 1], wr_t)
        else:
            z = _ret_in(xs, modsel, norm_g[i, 0], ret_w_qkvg[j].astype(BF16), cos_r, sin_r, 2 * d)
            o_f = None
            for dr in range(2):
                log_gamma = -jnp.abs(ret_log_decay[j, dr].astype(F32))
                o_f = _ret_core(z, log_gamma, o_f, rev=(dr == 1))
            x_mid, h2, lg = _post_call(_post_plain_kernel, "ret_out", [o_f], [row_spec(o_f.shape[-1])],
                                       [ret_w_o[j].astype(BF16)], xs, modsel, norm_g[i, 1], wr_t)
        g2 = modsel[:, :, 5:6, :]
        xs = _moe(x_mid, h2, lg, g2, moe_w_up[i].astype(BF16), moe_w_down[i].astype(BF16), n_ctx)

    return _final_norm(xs, final_g, n_ctx).astype(x.dtype)
```

```python
import functools
import math

import jax
import jax.numpy as jnp
from jax import lax
from jax.experimental import pallas as pl
from jax.experimental.pallas import tpu as pltpu

F32 = jnp.float32
BF16 = jnp.bfloat16

GRID_W = 64
N_MIXERS = 3
NORM_EPS = 1e-6
ROPE_THETA = 10000.0
ATTN_HEAD_DIM = 128
ATTN_GROUP = 4
S5_GROUP_SIZE = 16
S5_STATE = 64
S5_MIN_DECAY = 1e-4
RET_HEADS = 4
N_EXPERTS = 16
EC_CAPACITY_FACTOR = 2

LANES = 128
ROW_ALIGN_BF16 = 16
ROW_TILE = 256
S5_SUB = 8
VMEM_LIMIT = 48 * 1024 * 1024
HI = lax.Precision.HIGHEST


def _cparams(sem):
    return pltpu.CompilerParams(dimension_semantics=sem, vmem_limit_bytes=VMEM_LIMIT)


def _norm_mod(x, g, shift, scale):
    ms = jnp.mean(x * x, axis=-1, keepdims=True)
    return (x * lax.rsqrt(ms + NORM_EPS) * g) * (1.0 + scale) + shift


def _largest_divisor(n, cap, mult):
    best = None
    for t in range(mult, min(n, cap) + 1, mult):
        if n % t == 0:
            best = t
    assert best is not None, (n, cap, mult)
    return best


def _mod_kernel(c_ref, w_ref, b_ref, o_ref):
    c = c_ref[...]
    s = c * jax.nn.sigmoid(c)
    o_ref[...] = jnp.dot(s, w_ref[...], precision=HI, preferred_element_type=F32) + b_ref[...]


def _modulation(cvec, mod_w, mod_b):
    depth, d, n = mod_w.shape
    tn = _largest_divisor(n, 1536, 128)
    return pl.pallas_call(
        _mod_kernel,
        grid=(depth, n // tn),
        in_specs=[pl.BlockSpec((8, d), lambda l, j: (0, 0)),
                  pl.BlockSpec((None, d, tn), lambda l, j: (l, 0, j)),
                  pl.BlockSpec((None, 1, tn), lambda l, j: (l, 0, j))],
        out_specs=pl.BlockSpec((None, 8, tn), lambda l, j: (l, 0, j)),
        out_shape=jax.ShapeDtypeStruct((depth, 8, n), F32),
        compiler_params=_cparams(("arbitrary", "arbitrary")),
        name="modulation",
    )(cvec, mod_w, mod_b.reshape(depth, 1, n))


def _mod_spec(d):
    return pl.BlockSpec((None, None, 6, d), lambda b, i, *_: (b, jnp.minimum(i, 1), 0, 0))


def _attn_in_kernel(x_ref, mod_ref, g_ref, w_ref, qg_ref, kg_ref, cos_ref, sin_ref,
                    q_ref, k_ref, v_ref, *, n_q, n_kv):
    hd = ATTN_HEAD_DIM
    h = _norm_mod(x_ref[...], g_ref[...], mod_ref[0:1, :], mod_ref[1:2, :]).astype(BF16)
    qkv = jnp.dot(h, w_ref[...], preferred_element_type=F32)
    cos, sin = cos_ref[...], sin_ref[...]

    def norm_rope(t, gain):
        t = t * lax.rsqrt(jnp.mean(t * t, axis=-1, keepdims=True) + NORM_EPS) * gain
        return t * cos + pltpu.roll(t, hd // 2, axis=1) * sin

    scale = hd ** -0.5 * math.log2(math.e)
    for j in range(n_q):
        q_ref[j] = (norm_rope(qkv[:, j * hd:(j + 1) * hd], qg_ref[...]) * scale).astype(BF16)
    lane = lax.broadcasted_iota(jnp.int32, (qkv.shape[0], hd), 1)
    ones_col = jnp.where(lane == 0, 1.0, 0.0).astype(BF16)
    for j in range(n_kv):
        c0 = (n_q + j) * hd
        k_ref[j] = norm_rope(qkv[:, c0:c0 + hd], kg_ref[...]).astype(BF16)
        c1 = (n_q + n_kv + j) * hd
        v_ref[j, :, 0:hd] = qkv[:, c1:c1 + hd].astype(BF16)
        v_ref[j, :, hd:2 * hd] = ones_col


def _attn_in(x, modsel, g, w_qkv, q_gain, k_gain, cos2, sin2):
    b, t, d = x.shape
    hd = ATTN_HEAD_DIM
    n_tot = w_qkv.shape[1] // hd
    n_q = d // hd
    n_kv = (n_tot - n_q) // 2
    tm = ROW_TILE
    kern = functools.partial(_attn_in_kernel, n_q=n_q, n_kv=n_kv)
    return pl.pallas_call(
        kern,
        grid=(b, t // tm),
        in_specs=[pl.BlockSpec((None, tm, d), lambda b, i: (b, i, 0)),
                  _mod_spec(d),
                  pl.BlockSpec((1, d), lambda b, i: (0, 0)),
                  pl.BlockSpec(w_qkv.shape, lambda b, i: (0, 0)),
                  pl.BlockSpec((1, hd), lambda b, i: (0, 0)),
                  pl.BlockSpec((1, hd), lambda b, i: (0, 0)),
                  pl.BlockSpec((tm, hd), lambda b, i: (i, 0)),
                  pl.BlockSpec((tm, hd), lambda b, i: (i, 0))],
        out_specs=[pl.BlockSpec((None, n_q, tm, hd), lambda b, i: (b, 0, i, 0)),
                   pl.BlockSpec((None, n_kv, tm, hd), lambda b, i: (b, 0, i, 0)),
                   pl.BlockSpec((None, n_kv, tm, 2 * hd), lambda b, i: (b, 0, i, 0))],
        out_shape=[jax.ShapeDtypeStruct((b, n_q, t, hd), BF16),
                   jax.ShapeDtypeStruct((b, n_kv, t, hd), BF16),
                   jax.ShapeDtypeStruct((b, n_kv, t, 2 * hd), BF16)],
        compiler_params=_cparams(("arbitrary", "arbitrary")),
        name="attn_in",
    )(x, modsel, g.reshape(1, d), w_qkv, q_gain.reshape(1, hd), k_gain.reshape(1, hd), cos2, sin2)


def _flash_kernel(q_ref, k_ref, v_ref, o_ref, s0_scr, s1_scr, m_scr, acc_scr, *, tq, n_ctx):
    hd = ATTN_HEAD_DIM
    qi = pl.program_id(2)
    j = pl.program_id(3)
    nk = pl.num_programs(3) - 1
    m_init = -1e30

    @pl.when(j == 0)
    def _():
        m_scr[...] = jnp.full(m_scr.shape, m_init, F32)
        acc_scr[...] = jnp.zeros(acc_scr.shape, F32)
        s1_scr[...] = jnp.full(s1_scr.shape, -jnp.inf, F32)

    def softmax_update(s, v):
        m_prev = m_scr[...]
        m_new = jnp.maximum(m_prev, jnp.max(s, axis=1, keepdims=True))
        alpha = jnp.exp2(m_prev - m_new)
        p = jnp.exp2((s - m_new).astype(BF16))
        acc_scr[...] = alpha * acc_scr[...] + jnp.dot(p, v, preferred_element_type=F32)
        m_scr[...] = m_new

    def scores(k):
        q = q_ref[...].reshape(ATTN_GROUP * tq, hd)
        return lax.dot_general(q, k, (((1,), (1,)), ((), ())), preferred_element_type=F32)

    def pipelined(s_new, s_old):
        s_new[...] = scores(k_ref[...])
        softmax_update(s_old[...], v_ref[...])

    latent = qi > 0
    odd = jnp.bitwise_and(j, 1)

    @pl.when(jnp.logical_and(latent, odd == 0))
    def _():
        pipelined(s0_scr, s1_scr)

    @pl.when(jnp.logical_and(latent, odd == 1))
    def _():
        pipelined(s1_scr, s0_scr)

    @pl.when(jnp.logical_and(qi == 0, j == 0))
    def _():
        softmax_update(scores(k_ref[0:n_ctx, :]), v_ref[0:n_ctx, :])

    @pl.when(j == nk)
    def _():
        o = acc_scr[:, 0:hd] / acc_scr[:, hd:hd + 1]
        for g in range(ATTN_GROUP):
            o_ref[:, g * hd:(g + 1) * hd] = o[g * tq:(g + 1) * tq, :].astype(o_ref.dtype)


def _flash(q, k, v, n_ctx):
    b, n_q, t, hd = q.shape
    n_kv = k.shape[1]
    tq = ROW_TILE
    assert n_ctx == tq and n_q == n_kv * ATTN_GROUP
    tk = _largest_divisor(t, 1280, 256)
    nk = t // tk
    gw = ATTN_GROUP * hd
    rows = ATTN_GROUP * tq
    kern = functools.partial(_flash_kernel, tq=tq, n_ctx=n_ctx)

    def k_map(b, h, i, j):
        return (b, h, jnp.where(i == 0, 0, jnp.minimum(j, nk - 1)), 0)

    def v_map(b, h, i, j):
        return (b, h, jnp.where(i == 0, 0, jnp.maximum(j - 1, 0)), 0)

    return pl.pallas_call(
        kern,
        grid=(b, n_kv, t // tq, nk + 1),
        in_specs=[pl.BlockSpec((None, ATTN_GROUP, tq, hd), lambda b, h, i, j: (b, h, i, 0)),
                  pl.BlockSpec((None, None, tk, hd), k_map),
                  pl.BlockSpec((None, None, tk, 2 * hd), v_map)],
        out_specs=pl.BlockSpec((None, tq, gw), lambda b, h, i, j: (b, i, h)),
        out_shape=jax.ShapeDtypeStruct((b, t, n_q * hd), BF16),
        scratch_shapes=[pltpu.VMEM((rows, tk), F32),
                        pltpu.VMEM((rows, tk), F32),
                        pltpu.VMEM((rows, 1), F32),
                        pltpu.VMEM((rows, 2 * hd), F32)],
        compiler_params=_cparams(("arbitrary", "arbitrary", "arbitrary", "arbitrary")),
        name="flash_attn",
    )(q, k, v)


def _post_tail(x_new, mod_ref, g2_ref, wr_ref, x_out, h_out, lg_out):
    x_out[...] = x_new
    h2 = _norm_mod(x_new, g2_ref[...], mod_ref[3:4, :], mod_ref[4:5, :])
    h_out[...] = h2.astype(BF16)
    lg_out[...] = lax.dot_general(wr_ref[...], h2, (((1,), (1,)), ((), ())),
                                  precision=HI, preferred_element_type=F32)


def _post_plain_kernel(o_ref, w_ref, x_ref, mod_ref, g2_ref, wr_ref, x_out, h_out, lg_out):
    y = jnp.dot(o_ref[...], w_ref[...], preferred_element_type=F32)
    _post_tail(x_ref[...] + mod_ref[2:3, :] * y, mod_ref, g2_ref, wr_ref, x_out, h_out, lg_out)


def _post_glu_kernel(y_ref, hs_ref, dsk_ref, w_ref, b_ref, x_ref, mod_ref, g2_ref, wr_ref,
                     x_out, h_out, lg_out):
    d = x_ref.shape[-1]
    slabs = [hs_ref[o].astype(F32) for o in range(hs_ref.shape[0])]
    ys = [y_ref[0, o].astype(F32) + y_ref[1, o].astype(F32) for o in range(hs_ref.shape[0])]
    yt = dsk_ref[...] * jnp.concatenate(slabs, axis=1) + jnp.concatenate(ys, axis=1)
    z = jnp.dot(jax.nn.gelu(yt).astype(BF16), w_ref[...], preferred_element_type=F32) + b_ref[...]
    y = z[:, :d] * jax.nn.sigmoid(z[:, d:])
    _post_tail(x_ref[...] + mod_ref[2:3, :] * y, mod_ref, g2_ref, wr_ref, x_out, h_out, lg_out)


def _post_call(kern, name, row_inputs, row_specs, const_inputs, x, modsel, g2, w_router_t):
    b, t, d = x.shape
    tm = ROW_TILE
    ne = w_router_t.shape[0]
    const_specs = [pl.BlockSpec(a.shape, lambda b, i: (0, 0)) for a in const_inputs]
    return pl.pallas_call(
        kern,
        grid=(b, t // tm),
        in_specs=row_specs + const_specs + [
            pl.BlockSpec((None, tm, d), lambda b, i: (b, i, 0)),
            _mod_spec(d),
            pl.BlockSpec((1, d), lambda b, i: (0, 0)),
            pl.BlockSpec((ne, d), lambda b, i: (0, 0))],
        out_specs=[pl.BlockSpec((None, tm, d), lambda b, i: (b, i, 0)),
                   pl.BlockSpec((None, tm, d), lambda b, i: (b, i, 0)),
                   pl.BlockSpec((None, ne, tm), lambda b, i: (b, 0, i))],
        out_shape=[jax.ShapeDtypeStruct((b, t, d), F32),
                   jax.ShapeDtypeStruct((b, t, d), BF16),
                   jax.ShapeDtypeStruct((b, ne, t), F32)],
        compiler_params=_cparams(("arbitrary", "arbitrary")),
        name=name,
    )(*row_inputs, *const_inputs, x, modsel, g2.reshape(1, d), w_router_t)


def _ffn_kernel(x_ref, wa_ref, wu_ref, wd_ref, gate_ref, o_ref, acc_scr):
    f = pl.program_id(2)

    @pl.when(f == 0)
    def _():
        acc_scr[...] = jnp.zeros(acc_scr.shape, F32)

    x = x_ref[...]
    a = jnp.dot(x, wa_ref[...], preferred_element_type=F32)
    u = jnp.dot(x, wu_ref[...], preferred_element_type=F32)
    hmid = (a * jax.nn.sigmoid(a) * u).astype(BF16)
    acc_scr[...] += jnp.dot(hmid, wd_ref[...], preferred_element_type=F32)

    @pl.when(f == pl.num_programs(2) - 1)
    def _():
        o_ref[...] = (acc_scr[...] * gate_ref[...]).astype(o_ref.dtype)


def _expert_ffn(xe, w_up, w_down, gate):
    ne, r, d = xe.shape
    ff = w_down.shape[1]
    tm = _largest_divisor(r, 1040, 16)
    tf = _largest_divisor(ff, 256, 128)
    nf = ff // tf
    return pl.pallas_call(
        _ffn_kernel,
        grid=(ne, r // tm, nf),
        in_specs=[pl.BlockSpec((None, tm, d), lambda e, m, f: (e, m, 0)),
                  pl.BlockSpec((None, d, tf), lambda e, m, f: (e, 0, f)),
                  pl.BlockSpec((None, d, tf), lambda e, m, f: (e, 0, nf + f)),
                  pl.BlockSpec((None, tf, d), lambda e, m, f: (e, f, 0)),
                  pl.BlockSpec((None, tm, 1), lambda e, m, f: (e, m, 0))],
        out_specs=pl.BlockSpec((None, tm, d), lambda e, m, f: (e, m, 0)),
        out_shape=jax.ShapeDtypeStruct((ne, r, d), BF16),
        scratch_shapes=[pltpu.VMEM((tm, d), F32)],
        compiler_params=_cparams(("arbitrary", "arbitrary", "arbitrary")),
        name="expert_ffn",
    )(xe, w_up, w_up, w_down, gate)


def _combine_kernel(start_ref, rel_ref, x_ref, mod_ref, *refs):
    y_refs, o_ref = refs[:-1], refs[-1]
    tm = x_ref.shape[0]
    win = y_refs[0].shape[1]
    rel = rel_ref[...]
    lane = lax.broadcasted_iota(jnp.int32, (tm, win), 1)
    acc = jnp.zeros(x_ref.shape, F32)
    for e, y_ref in enumerate(y_refs):
        onehot = jnp.where(rel[:, e:e + 1] == lane, 1.0, 0.0).astype(BF16)
        acc = acc + jnp.dot(onehot, y_ref[0], preferred_element_type=F32)
    o_ref[...] = x_ref[...] + mod_ref[5:6, :] * acc


def _combine(x_mid, modsel, y, rel_t, start):
    b, t, d = x_mid.shape
    ne, r, _ = y.shape
    tm = ROW_TILE
    win = tm + ROW_ALIGN_BF16
    y_specs = [pl.BlockSpec((pl.Element(1), pl.Element(win), pl.Element(d)),
                            lambda b, i, st, e=e: (e, st[b, e, i] * ROW_ALIGN_BF16, 0))
               for e in range(ne)]
    return pl.pallas_call(
        _combine_kernel,
        grid_spec=pltpu.PrefetchScalarGridSpec(
            num_scalar_prefetch=1,
            grid=(b, t // tm),
            in_specs=[pl.BlockSpec((None, tm, ne), lambda b, i, st: (b, i, 0)),
                      pl.BlockSpec((None, tm, d), lambda b, i, st: (b, i, 0)),
                      _mod_spec(d)] + y_specs,
            out_specs=pl.BlockSpec((None, tm, d), lambda b, i, st: (b, i, 0))),
        out_shape=jax.ShapeDtypeStruct((b, t, d), F32),
        compiler_params=_cparams(("arbitrary", "arbitrary")),
        name="moe_combine",
    )(start, rel_t, x_mid, modsel, *([y] * ne))


def _route_segment(aff, cap):
    gate_u, idx_u = lax.top_k(aff, cap)
    kth = gate_u[..., -1:]
    gt = aff > kth
    eq = aff == kth
    need = cap - jnp.sum(gt, axis=-1, keepdims=True, dtype=jnp.int32)
    mask = gt | (eq & (jnp.cumsum(eq.astype(jnp.int32), axis=-1) <= need))
    m32 = mask.astype(jnp.int32)
    pos = jnp.cumsum(m32, axis=-1) - m32
    idx = jnp.sort(idx_u.astype(jnp.int32), axis=-1)
    gate = jnp.take_along_axis(aff, idx, axis=-1)
    return mask, pos, idx, gate


def _moe(x_mid, h2, logits_t, modsel, w_up, w_down, n_ctx):
    b, t, d = x_mid.shape
    ne = logits_t.shape[1]
    tm = ROW_TILE
    win = tm + ROW_ALIGN_BF16
    aff = jax.nn.softmax(logits_t, axis=1)
    cap_l = EC_CAPACITY_FACTOR * (t - n_ctx) // ne
    cap_c = EC_CAPACITY_FACTOR * n_ctx // ne
    r = b * (cap_l + cap_c)
    assert r % ROW_ALIGN_BF16 == 0 and r >= win
    mask_l, pos_l, idx_l, gate_l = _route_segment(aff[:, :, n_ctx:], cap_l)
    mask_c, pos_c, idx_c, gate_c = _route_segment(aff[:, :, :n_ctx], cap_c)
    bi = jnp.arange(b, dtype=jnp.int32)[:, None, None]

    def per_expert(a):
        return a.transpose(1, 0, 2).reshape(ne, -1)

    idx = jnp.concatenate([per_expert(idx_l + bi * t + n_ctx), per_expert(idx_c + bi * t)], axis=1)
    gate = jnp.concatenate([per_expert(gate_l), per_expert(gate_c)], axis=1)
    xe = jnp.take(h2.reshape(b * t, d), idx, axis=0)
    y = _expert_ffn(xe, w_up, w_down, gate[..., None])

    row = jnp.concatenate([pos_c + b * cap_l + bi * cap_c, pos_l + bi * cap_l], axis=2)
    mask = jnp.concatenate([mask_c, mask_l], axis=2)
    start = jnp.minimum(row[:, :, ::tm] // ROW_ALIGN_BF16 * ROW_ALIGN_BF16, r - win)
    rel = jnp.where(mask, row - jnp.repeat(start, tm, axis=2), -1)
    return _combine(x_mid, modsel, y, rel.transpose(0, 2, 1), start // ROW_ALIGN_BF16)


def _norm_slab_kernel(x_ref, mod_ref, g_ref, h_ref):
    h = _norm_mod(x_ref[...], g_ref[...], mod_ref[0:1, :], mod_ref[1:2, :]).astype(h_ref.dtype)
    for o in range(h_ref.shape[0]):
        h_ref[o] = h[:, o * LANES:(o + 1) * LANES]


def _norm_slabs(x, modsel, g):
    b, t, d = x.shape
    tm = ROW_TILE
    n_slab = d // LANES
    return pl.pallas_call(
        _norm_slab_kernel,
        grid=(b, t // tm),
        in_specs=[pl.BlockSpec((None, tm, d), lambda b, i: (b, i, 0)),
                  _mod_spec(d),
                  pl.BlockSpec((1, d), lambda b, i: (0, 0))],
        out_specs=pl.BlockSpec((None, n_slab, tm, LANES), lambda b, i: (b, 0, i, 0)),
        out_shape=jax.ShapeDtypeStruct((b, n_slab, t, LANES), BF16),
        compiler_params=_cparams(("arbitrary", "arbitrary")),
        name="norm_mod",
    )(x, modsel, g.reshape(1, d))


def _s5_core_kernel(u_ref, win_ref, toep_ref, wout_ref, a_ref, y_ref, zh_scr, *, m_ctx):
    dr = pl.program_id(2)
    m = u_ref.shape[0]
    rc = _largest_divisor(m, 512, 16)
    half = zh_scr.shape[1] // 2
    for r0 in range(0, m, rc):
        zh_scr[r0:r0 + rc, :] = jnp.dot(u_ref[r0:r0 + rc, :], win_ref[...], preferred_element_type=F32)
    ar = a_ref[0:1, :]
    ai = a_ref[1:2, :]
    rid = lax.broadcasted_iota(jnp.int32, (8, half), 0)

    def visit(blk, carry, reverse):
        sr, si = carry
        base = pl.multiple_of(blk * 8, 8)
        z8 = zh_scr[pl.ds(base, 8), :]
        hr = jnp.zeros((8, half), F32)
        hi = jnp.zeros((8, half), F32)
        for r in (range(7, -1, -1) if reverse else range(8)):
            hr = jnp.where(rid == r, sr, hr)
            hi = jnp.where(rid == r, si, hi)
            zr = z8[r:r + 1, 0:half]
            zi = z8[r:r + 1, half:2 * half]
            sr, si = ar * sr - ai * si + zr, ar * si + ai * sr + zi
        zh_scr[pl.ds(base, 8), 0:half] = hr
        zh_scr[pl.ds(base, 8), half:2 * half] = hi
        return sr, si

    zero = (jnp.zeros((1, half), F32), jnp.zeros((1, half), F32))
    nb, nb_ctx = m // 8, m_ctx // 8

    @pl.when(dr == 0)
    def _():
        lax.fori_loop(0, nb, lambda s, cr: visit(s, cr, False), zero)

    @pl.when(dr == 1)
    def _():
        carry = lax.fori_loop(0, nb_ctx, lambda s, cr: visit(nb_ctx - 1 - s, cr, True), zero)
        lax.fori_loop(0, nb - nb_ctx, lambda s, cr: visit(nb - 1 - s, cr, True), carry)

    for r0 in range(0, m, rc):
        y = (jnp.dot(u_ref[r0:r0 + rc, :], toep_ref[...], preferred_element_type=F32)
             + jnp.dot(zh_scr[r0:r0 + rc, :].astype(BF16), wout_ref[...], preferred_element_type=F32))
        y_ref[r0:r0 + rc, :] = y.astype(y_ref.dtype)


def _s5_weights(lam_re, lam_im, log_dt, b_re, b_im, c_re, c_im):
    sub = S5_SUB
    gs = S5_GROUP_SIZE
    gps = LANES // gs
    outs = []
    for dr in range(2):
        lre = jnp.minimum(lam_re[dr].astype(F32), -S5_MIN_DECAY)
        lim = lam_im[dr].astype(F32)
        dt = jnp.exp(log_dt[dr].astype(F32))[:, None]
        mag = jnp.exp(lre * dt)
        ang = lim * dt
        lbr, lbi = mag * jnp.cos(ang), mag * jnp.sin(ang)
        den = lre * lre + lim * lim
        f_re = ((lbr - 1) * lre + lbi * lim) / den
        f_im = (lbi * lre - (lbr - 1) * lim) / den
        bre, bim = b_re[dr].astype(F32), b_im[dr].astype(F32)
        bbr = f_re[..., None] * bre - f_im[..., None] * bim
        bbi = f_re[..., None] * bim + f_im[..., None] * bre
        cr, ci = c_re[dr].astype(F32), c_im[dr].astype(F32)
        pr, pi = [jnp.ones_like(lbr)], [jnp.zeros_like(lbr)]
        for _ in range(sub):
            pr.append(pr[-1] * lbr - pi[-1] * lbi)
            pi.append(pr[-2] * lbi + pi[-1] * lbr)
        pw_r, pw_i = jnp.stack(pr), jnp.stack(pi)
        pb_r = pw_r[..., None] * bbr - pw_i[..., None] * bbi
        pb_i = pw_r[..., None] * bbi + pw_i[..., None] * bbr
        cp_r = cr[None] * pw_r[:, :, None, :] - ci[None] * pw_i[:, :, None, :]
        cp_i = cr[None] * pw_i[:, :, None, :] + ci[None] * pw_r[:, :, None, :]
        kk = (jnp.einsum('gip,tgpj->tgij', cr, pb_r[:sub], precision=HI)
              - jnp.einsum('gip,tgpj->tgij', ci, pb_i[:sub], precision=HI))
        s_idx = jnp.arange(sub)
        lag = (s_idx[None, :] - s_idx[:, None]) if dr == 0 else (s_idx[:, None] - s_idx[None, :])
        kt = kk[jnp.clip(lag, 0, sub - 1)]
        kt = jnp.where((lag >= 0)[:, :, None, None, None], kt, 0.0)
        g_n = kt.shape[2]
        n_slab = g_n // gps
        eye = jnp.eye(gps, dtype=F32)
        toep = jnp.einsum('stogij,gh->osgjthi', kt.reshape(sub, sub, n_slab, gps, gs, gs), eye)
        toep = toep.reshape(n_slab, sub * LANES, sub * LANES)
        e_in = (sub - 1 - s_idx) if dr == 0 else s_idx

        def in_map(pb):
            w = jnp.einsum('sogpj,gh->osgjhp', pb[e_in].reshape(sub, n_slab, gps, S5_STATE, gs), eye)
            return w.reshape(n_slab, sub * LANES, gps * S5_STATE)

        win = jnp.concatenate([in_map(pb_r), in_map(pb_i)], axis=2)
        e_out = (s_idx + 1) if dr == 0 else (sub - s_idx)

        def out_map(cp):
            w = jnp.einsum('togip,gh->ogpthi', cp[e_out].reshape(sub, n_slab, gps, gs, S5_STATE), eye)
            return w.reshape(n_slab, gps * S5_STATE, sub * LANES)

        wout = jnp.concatenate([out_map(cp_r), -out_map(cp_i)], axis=1)
        a_sub = jnp.stack([pw_r[sub].reshape(n_slab, gps * S5_STATE),
                           pw_i[sub].reshape(n_slab, gps * S5_STATE)], axis=1)
        outs.append((win, toep, wout, a_sub))

    stack = lambda k, dt: jnp.stack([o[k] for o in outs], axis=1).astype(dt)
    return stack(0, BF16), stack(1, BF16), stack(2, BF16), stack(3, F32)


def _s5_core(hs, weights, n_ctx):
    b, n_slab, t, _ = hs.shape
    sub = S5_SUB
    win, toep, wout, a_sub = weights
    m = t // sub
    m_ctx = n_ctx // sub
    assert m % 8 == 0 and m_ctx % 8 == 0
    cw = sub * LANES
    sw = win.shape[-1]
    u = hs.reshape(b, n_slab, m, cw)
    kern = functools.partial(_s5_core_kernel, m_ctx=m_ctx)
    y = pl.pallas_call(
        kern,
        grid=(n_slab, b, 2),
        in_specs=[pl.BlockSpec((None, None, m, cw), lambda o, b, d: (b, o, 0, 0)),
                  pl.BlockSpec((None, None, cw, sw), lambda o, b, d: (o, d, 0, 0)),
                  pl.BlockSpec((None, None, cw, cw), lambda o, b, d: (o, d, 0, 0)),
                  pl.BlockSpec((None, None, sw, cw), lambda o, b, d: (o, d, 0, 0)),
                  pl.BlockSpec((None, None, 2, sw // 2), lambda o, b, d: (o, d, 0, 0))],
        out_specs=pl.BlockSpec((None, None, None, m, cw), lambda o, b, d: (d, b, o, 0, 0)),
        out_shape=jax.ShapeDtypeStruct((2, b, n_slab, m, cw), BF16),
        scratch_shapes=[pltpu.VMEM((m, sw), F32)],
        compiler_params=_cparams(("arbitrary", "arbitrary", "arbitrary")),
        name="s5_core",
    )(u, win, toep, wout, a_sub)
    return y.reshape(2, b, n_slab, t, LANES)


def _ret_in_kernel(x_ref, mod_ref, g_ref, w_ref, cos_ref, sin_ref, z_ref, *, n_rope_cols):
    h = _norm_mod(x_ref[...], g_ref[...], mod_ref[0:1, :], mod_ref[1:2, :]).astype(BF16)
    cos, sin = cos_ref[...], sin_ref[...]
    half = cos.shape[1]
    tn = 4 * half
    for n0 in range(0, w_ref.shape[1], tn):
        z = jnp.dot(h, w_ref[:, n0:n0 + tn], preferred_element_type=F32)
        if n0 < n_rope_cols:
            for c0 in range(0, tn, 2 * half):
                x1 = z[:, c0:c0 + half]
                x2 = z[:, c0 + half:c0 + 2 * half]
                z_ref[:, n0 + c0:n0 + c0 + half] = (x1 * cos - x2 * sin).astype(z_ref.dtype)
                z_ref[:, n0 + c0 + half:n0 + c0 + 2 * half] = (x2 * cos + x1 * sin).astype(z_ref.dtype)
        else:
            z_ref[:, n0:n0 + tn] = z.astype(z_ref.dtype)


def _ret_in(x, modsel, g, w, cos, sin, n_rope_cols):
    b, t, d = x.shape
    n_out = w.shape[1]
    tm = ROW_TILE
    half = cos.shape[1]
    assert n_out % (4 * half) == 0 and n_rope_cols % (4 * half) == 0
    kern = functools.partial(_ret_in_kernel, n_rope_cols=n_rope_cols)
    return pl.pallas_call(
        kern,
        grid=(b, t // tm),
        in_specs=[pl.BlockSpec((None, tm, d), lambda b, i: (b, i, 0)),
                  _mod_spec(d),
                  pl.BlockSpec((1, d), lambda b, i: (0, 0)),
                  pl.BlockSpec((d, n_out), lambda b, i: (0, 0), pipeline_mode=pl.Buffered(1)),
                  pl.BlockSpec((tm, half), lambda b, i: (i, 0)),
                  pl.BlockSpec((tm, half), lambda b, i: (i, 0))],
        out_specs=pl.BlockSpec((None, tm, n_out), lambda b, i: (b, i, 0)),
        out_shape=jax.ShapeDtypeStruct((b, t, n_out), BF16),
        compiler_params=_cparams(("arbitrary", "arbitrary")),
        name="ret_in",
    )(x, modsel, g.reshape(1, d), w, cos, sin)


def _ret_core_kernel(lg_ref, q_ref, k_ref, v_ref, *rest, rev, qk_scale):
    if rev:
        of_ref, g_ref, o_ref, s_scr = rest
    else:
        o_ref, s_scr = rest
    hh = pl.program_id(1)
    c = pl.program_id(2)
    tc = q_ref.shape[0]

    @pl.when(c == 0)
    def _():
        s_scr[...] = jnp.zeros(s_scr.shape, F32)

    lg = jnp.full((1, 1), lg_ref[hh], F32)
    row = lax.broadcasted_iota(jnp.int32, (tc, tc), 0)
    col = lax.broadcasted_iota(jnp.int32, (tc, tc), 1)
    diff = (col - row) if rev else (row - col)
    inner = jnp.where(diff >= 0, jnp.exp(lg * jnp.maximum(diff, 0).astype(F32)), 0.0) * qk_scale
    pos = lax.broadcasted_iota(jnp.int32, (tc, 1), 0).astype(F32)
    if rev:
        q_dec = jnp.exp(lg * (tc - pos))
        k_dec = jnp.exp(lg * pos) * qk_scale
    else:
        q_dec = jnp.exp(lg * (pos + 1.0))
        k_dec = jnp.exp(lg * (tc - 1.0 - pos)) * qk_scale
    q = q_ref[...]
    k = k_ref[...]
    v = v_ref[...]
    s = lax.dot_general(q, k, (((1,), (1,)), ((), ())), preferred_element_type=F32) * inner
    state = s_scr[...]
    o = (jnp.dot(s.astype(BF16), v, preferred_element_type=F32)
         + jnp.dot((q.astype(F32) * q_dec).astype(BF16), state.astype(BF16), preferred_element_type=F32))
    kd_t = (k.astype(F32) * k_dec).T.astype(BF16)
    s_scr[...] = state * jnp.exp(lg * tc) + jnp.dot(kd_t, v, preferred_element_type=F32)
    if rev:
        tot = of_ref[...] + o
        nrm = tot * lax.rsqrt(jnp.mean(tot * tot, axis=-1, keepdims=True) + NORM_EPS)
        gg = g_ref[...].astype(F32)
        o_ref[...] = (gg * jax.nn.sigmoid(gg) * nrm).astype(o_ref.dtype)
    else:
        o_ref[...] = o


def _ret_core(z, log_gamma, o_fwd, rev):
    b, t, _ = z.shape
    nh = RET_HEADS
    dqk = z.shape[2] // (6 * nh)
    dv = 2 * dqk
    tc = ROW_TILE
    nc = t // tc

    if rev:
        def cmap(c):
            return jnp.where(c == 0, 0, nc - c)
    else:
        def cmap(c):
            return c

    kern = functools.partial(_ret_core_kernel, rev=rev, qk_scale=dqk ** -0.5)
    in_specs = [pl.BlockSpec((None, tc, dqk), lambda b, h, c, lg: (b, cmap(c), h)),
                pl.BlockSpec((None, tc, dqk), lambda b, h, c, lg: (b, cmap(c), nh + h)),
                pl.BlockSpec((None, tc, dv), lambda b, h, c, lg: (b, cmap(c), nh + h))]
    args = [z, z, z]
    if rev:
        in_specs += [pl.BlockSpec((None, tc, dv), lambda b, h, c, lg: (b, cmap(c), h)),
                     pl.BlockSpec((None, tc, dv), lambda b, h, c, lg: (b, cmap(c), 2 * nh + h))]
        args += [o_fwd, z]
    out_dtype = BF16 if rev else F32
    return pl.pallas_call(
        kern,
        grid_spec=pltpu.PrefetchScalarGridSpec(
            num_scalar_prefetch=1,
            grid=(b, nh, nc),
            in_specs=in_specs,
            out_specs=pl.BlockSpec((None, tc, dv), lambda b, h, c, lg: (b, cmap(c), h)),
            scratch_shapes=[pltpu.VMEM((dqk, dv), F32)]),
        out_shape=jax.ShapeDtypeStruct((b, t, nh * dv), out_dtype),
        compiler_params=_cparams(("arbitrary", "arbitrary", "arbitrary")),
        name="ret_core_bwd" if rev else "ret_core_fwd",
    )(log_gamma, *args)


def _final_kernel(x_ref, g_ref, o_ref):
    x = x_ref[...]
    o_ref[...] = x * lax.rsqrt(jnp.mean(x * x, axis=-1, keepdims=True) + NORM_EPS) * g_ref[...]


def _final_norm(x, g, n_ctx):
    b, t, d = x.shape
    tm = ROW_TILE
    skip = n_ctx // tm
    return pl.pallas_call(
        _final_kernel,
        grid=(b, (t - n_ctx) // tm),
        in_specs=[pl.BlockSpec((None, tm, d), lambda b, i: (b, i + skip, 0)),
                  pl.BlockSpec((1, d), lambda b, i: (0, 0))],
        out_specs=pl.BlockSpec((None, tm, d), lambda b, i: (b, i, 0)),
        out_shape=jax.ShapeDtypeStruct((b, t - n_ctx, d), F32),
        compiler_params=_cparams(("arbitrary", "arbitrary")),
        name="final_norm",
    )(x, g.reshape(1, d))


def _rope_tables(n_ctx, n_lat, head_dim):
    rows = n_lat // GRID_W
    row = jnp.repeat(jnp.arange(rows, dtype=F32), GRID_W)
    col = jnp.tile(jnp.arange(GRID_W, dtype=F32), rows)
    n_freq = head_dim // 4
    inv_freq = ROPE_THETA ** (-jnp.arange(n_freq, dtype=F32) / n_freq)
    ang = jnp.concatenate([row[:, None] * inv_freq, col[:, None] * inv_freq], axis=-1)
    cos = jnp.concatenate([jnp.ones((n_ctx, head_dim // 2), F32), jnp.cos(ang)], axis=0)
    sin = jnp.concatenate([jnp.zeros((n_ctx, head_dim // 2), F32), jnp.sin(ang)], axis=0)
    return cos, sin


def kernel(x, c, ctx, c_ctx, mod_w, mod_b, norm_g, attn_w_qkv, attn_w_o, attn_q_gain, attn_k_gain,
           s5_lambda_re, s5_lambda_im, s5_log_dt, s5_b_re, s5_b_im, s5_c_re, s5_c_im, s5_d, s5_w_glu, s5_b_glu,
           ret_w_qkvg, ret_w_o, ret_log_decay, moe_w_router, moe_w_up, moe_w_down, final_g):
    bsz, n_lat, d = x.shape
    n_ctx = ctx.shape[1]
    depth = mod_w.shape[0]
    assert n_ctx == ROW_TILE and n_lat % ROW_TILE == 0 and bsz <= 7

    xs = jnp.concatenate([ctx, x], axis=1).astype(F32)
    cvec = jnp.zeros((8, d), F32).at[:bsz].set(c).at[bsz].set(c_ctx)
    mods = _modulation(cvec, mod_w, mod_b)

    cos_a, sin_a = _rope_tables(n_ctx, n_lat, ATTN_HEAD_DIM)
    cos2 = jnp.concatenate([cos_a, cos_a], axis=1)
    sin2 = jnp.concatenate([-sin_a, sin_a], axis=1)
    cos_r, sin_r = _rope_tables(n_ctx, n_lat, d // RET_HEADS)

    tm = ROW_TILE

    def row_spec(width):
        return pl.BlockSpec((None, tm, width), lambda b, i: (b, i, 0))

    for i in range(depth):
        kind, j = i % N_MIXERS, i // N_MIXERS
        m6 = mods[i].reshape(8, 6, d)
        modsel = jnp.stack([jnp.broadcast_to(m6[bsz], (bsz, 6, d)), m6[:bsz]], axis=1)
        wr_t = moe_w_router[i].T.astype(F32)
        if kind == 0:
            q, k, v = _attn_in(xs, modsel, norm_g[i, 0], attn_w_qkv[j].astype(BF16),
                               attn_q_gain[j], attn_k_gain[j], cos2, sin2)
            o = _flash(q, k, v, n_ctx)
            x_mid, h2, lg = _post_call(_post_plain_kernel, "attn_out", [o], [row_spec(o.shape[-1])],
                                       [attn_w_o[j].astype(BF16)], xs, modsel, norm_g[i, 1], wr_t)
        elif kind == 1:
            hs = _norm_slabs(xs, modsel, norm_g[i, 0])
            w5 = _s5_weights(s5_lambda_re[j], s5_lambda_im[j], s5_log_dt[j], s5_b_re[j], s5_b_im[j],
                             s5_c_re[j], s5_c_im[j])
            y = _s5_core(hs, w5, n_ctx)
            n_slab = d // LANES
            slab_specs = [pl.BlockSpec((2, None, n_slab, tm, LANES), lambda b, i: (0, b, 0, i, 0)),
                          pl.BlockSpec((None, n_slab, tm, LANES), lambda b, i: (b, 0, i, 0))]
            x_mid, h2, lg = _post_call(_post_glu_kernel, "s5_out", [y, hs], slab_specs,
                                       [s5_d[j].reshape(1, d).astype(F32), s5_w_glu[j].astype(BF16),
                                        s5_b_glu[j].reshape(1, -1).astype(F32)],
                                       xs, modsel, norm_g[i, 1], wr_t)
        else:
            z = _ret_in(xs, modsel, norm_g[i, 0], ret_w_qkvg[j].astype(BF16), cos_r, sin_r, 2 * d)
            o_f = None
            for dr in range(2):
                log_gamma = -jnp.abs(ret_log_decay[j, dr].astype(F32))
                o_f = _ret_core(z, log_gamma, o_f, rev=(dr == 1))
            x_mid, h2, lg = _post_call(_post_plain_kernel, "ret_out", [o_f], [row_spec(o_f.shape[-1])],
                                       [ret_w_o[j].astype(BF16)], xs, modsel, norm_g[i, 1], wr_t)
        xs = _moe(x_mid, h2, lg, modsel, moe_w_up[i].astype(BF16), moe_w_down[i].astype(BF16), n_ctx)

    return _final_norm(xs, final_g, n_ctx).astype(x.dtype)
```

```python
import functools
import math

import jax
import jax.numpy as jnp
from jax import lax
from jax.experimental import pallas as pl
from jax.experimental.pallas import tpu as pltpu

F32 = jnp.float32
BF16 = jnp.bfloat16

GRID_W = 64
N_MIXERS = 3
NORM_EPS = 1e-6
ROPE_THETA = 10000.0
ATTN_HEAD_DIM = 128
ATTN_GROUP = 4
S5_GROUP_SIZE = 16
S5_STATE = 64
S5_MIN_DECAY = 1e-4
RET_HEADS = 4
N_EXPERTS = 16
EC_CAPACITY_FACTOR = 2

LANES = 128
ROW_ALIGN_BF16 = 16
ROW_TILE = 256
S5_SUB = 8
VMEM_LIMIT = 48 * 1024 * 1024
HI = lax.Precision.HIGHEST


def _cparams(sem):
    return pltpu.CompilerParams(dimension_semantics=sem, vmem_limit_bytes=VMEM_LIMIT)


def _norm_mod(x, g, shift, scale):
    ms = jnp.mean(x * x, axis=-1, keepdims=True)
    return (x * lax.rsqrt(ms + NORM_EPS) * g) * (1.0 + scale) + shift


def _largest_divisor(n, cap, mult):
    best = None
    for t in range(mult, min(n, cap) + 1, mult):
        if n % t == 0:
            best = t
    assert best is not None, (n, cap, mult)
    return best


def _mod_kernel(c_ref, w_ref, b_ref, o_ref):
    c = c_ref[...]
    s = c * jax.nn.sigmoid(c)
    o_ref[...] = jnp.dot(s, w_ref[...], precision=HI, preferred_element_type=F32) + b_ref[...]


def _modulation(cvec, mod_w, mod_b):
    depth, d, n = mod_w.shape
    tn = _largest_divisor(n, 1536, 128)
    return pl.pallas_call(
        _mod_kernel,
        grid=(depth, n // tn),
        in_specs=[pl.BlockSpec((8, d), lambda l, j: (0, 0)),
                  pl.BlockSpec((None, d, tn), lambda l, j: (l, 0, j)),
                  pl.BlockSpec((None, 1, tn), lambda l, j: (l, 0, j))],
        out_specs=pl.BlockSpec((None, 8, tn), lambda l, j: (l, 0, j)),
        out_shape=jax.ShapeDtypeStruct((depth, 8, n), F32),
        compiler_params=_cparams(("arbitrary", "arbitrary")),
        name="modulation",
    )(cvec, mod_w, mod_b.reshape(depth, 1, n))


def _mod_spec(d):
    return pl.BlockSpec((None, None, 6, d), lambda b, i, *_: (b, jnp.minimum(i, 1), 0, 0))


def _attn_in_kernel(x_ref, mod_ref, g_ref, w_ref, qg_ref, kg_ref, cos_ref, sin_ref,
                    q_ref, k_ref, v_ref, *, n_q, n_kv):
    hd = ATTN_HEAD_DIM
    h = _norm_mod(x_ref[...], g_ref[...], mod_ref[0:1, :], mod_ref[1:2, :]).astype(BF16)
    qkv = jnp.dot(h, w_ref[...], preferred_element_type=F32)
    cos, sin = cos_ref[...], sin_ref[...]

    def norm_rope(t, gain):
        t = t * lax.rsqrt(jnp.mean(t * t, axis=-1, keepdims=True) + NORM_EPS) * gain
        return t * cos + pltpu.roll(t, hd // 2, axis=1) * sin

    scale = hd ** -0.5 * math.log2(math.e)
    for j in range(n_q):
        q_ref[j] = (norm_rope(qkv[:, j * hd:(j + 1) * hd], qg_ref[...]) * scale).astype(BF16)
    lane = lax.broadcasted_iota(jnp.int32, (qkv.shape[0], hd), 1)
    ones_col = jnp.where(lane == 0, 1.0, 0.0).astype(BF16)
    for j in range(n_kv):
        c0 = (n_q + j) * hd
        k_ref[j] = norm_rope(qkv[:, c0:c0 + hd], kg_ref[...]).astype(BF16)
        c1 = (n_q + n_kv + j) * hd
        v_ref[j, :, 0:hd] = qkv[:, c1:c1 + hd].astype(BF16)
        v_ref[j, :, hd:2 * hd] = ones_col


def _attn_in(x, modsel, g, w_qkv, q_gain, k_gain, cos2, sin2):
    b, t, d = x.shape
    hd = ATTN_HEAD_DIM
    n_tot = w_qkv.shape[1] // hd
    n_q = d // hd
    n_kv = (n_tot - n_q) // 2
    tm = ROW_TILE
    kern = functools.partial(_attn_in_kernel, n_q=n_q, n_kv=n_kv)
    return pl.pallas_call(
        kern,
        grid=(b, t // tm),
        in_specs=[pl.BlockSpec((None, tm, d), lambda b, i: (b, i, 0)),
                  _mod_spec(d),
                  pl.BlockSpec((1, d), lambda b, i: (0, 0)),
                  pl.BlockSpec(w_qkv.shape, lambda b, i: (0, 0)),
                  pl.BlockSpec((1, hd), lambda b, i: (0, 0)),
                  pl.BlockSpec((1, hd), lambda b, i: (0, 0)),
                  pl.BlockSpec((tm, hd), lambda b, i: (i, 0)),
                  pl.BlockSpec((tm, hd), lambda b, i: (i, 0))],
        out_specs=[pl.BlockSpec((None, n_q, tm, hd), lambda b, i: (b, 0, i, 0)),
                   pl.BlockSpec((None, n_kv, tm, hd), lambda b, i: (b, 0, i, 0)),
                   pl.BlockSpec((None, n_kv, tm, 2 * hd), lambda b, i: (b, 0, i, 0))],
        out_shape=[jax.ShapeDtypeStruct((b, n_q, t, hd), BF16),
                   jax.ShapeDtypeStruct((b, n_kv, t, hd), BF16),
                   jax.ShapeDtypeStruct((b, n_kv, t, 2 * hd), BF16)],
        compiler_params=_cparams(("arbitrary", "arbitrary")),
        name="attn_in",
    )(x, modsel, g.reshape(1, d), w_qkv, q_gain.reshape(1, hd), k_gain.reshape(1, hd), cos2, sin2)


def _flash_kernel(q_ref, k_ref, v_ref, o_ref, s0_scr, s1_scr, m_scr, acc_scr, *, tq, n_ctx):
    hd = ATTN_HEAD_DIM
    qi = pl.program_id(2)
    j = pl.program_id(3)
    nk = pl.num_programs(3) - 1
    m_init = -1e30

    @pl.when(j == 0)
    def _():
        m_scr[...] = jnp.full(m_scr.shape, m_init, F32)
        acc_scr[...] = jnp.zeros(acc_scr.shape, F32)
        s1_scr[...] = jnp.full(s1_scr.shape, -jnp.inf, F32)

    def softmax_update(s, v):
        m_prev = m_scr[...]
        m_new = jnp.maximum(m_prev, jnp.max(s, axis=1, keepdims=True))
        alpha = jnp.exp2(m_prev - m_new)
        p = jnp.exp2((s - m_new).astype(BF16))
        acc_scr[...] = alpha * acc_scr[...] + jnp.dot(p, v, preferred_element_type=F32)
        m_scr[...] = m_new

    def scores(k):
        q = q_ref[...].reshape(ATTN_GROUP * tq, hd)
        return lax.dot_general(q, k, (((1,), (1,)), ((), ())), preferred_element_type=F32)

    def pipelined(s_new, s_old):
        s_new[...] = scores(k_ref[...])
        softmax_update(s_old[...], v_ref[...])

    latent = qi > 0
    odd = jnp.bitwise_and(j, 1)

    @pl.when(jnp.logical_and(latent, odd == 0))
    def _():
        pipelined(s0_scr, s1_scr)

    @pl.when(jnp.logical_and(latent, odd == 1))
    def _():
        pipelined(s1_scr, s0_scr)

    @pl.when(jnp.logical_and(qi == 0, j == 0))
    def _():
        softmax_update(scores(k_ref[0:n_ctx, :]), v_ref[0:n_ctx, :])

    @pl.when(j == nk)
    def _():
        o = acc_scr[:, 0:hd] / acc_scr[:, hd:hd + 1]
        for g in range(ATTN_GROUP):
            o_ref[:, g * hd:(g + 1) * hd] = o[g * tq:(g + 1) * tq, :].astype(o_ref.dtype)


def _flash(q, k, v, n_ctx):
    b, n_q, t, hd = q.shape
    n_kv = k.shape[1]
    tq = ROW_TILE
    assert n_ctx == tq and n_q == n_kv * ATTN_GROUP
    tk = _largest_divisor(t, 1280, 256)
    nk = t // tk
    gw = ATTN_GROUP * hd
    rows = ATTN_GROUP * tq
    kern = functools.partial(_flash_kernel, tq=tq, n_ctx=n_ctx)

    def k_map(b, h, i, j):
        return (b, h, jnp.where(i == 0, 0, jnp.minimum(j, nk - 1)), 0)

    def v_map(b, h, i, j):
        return (b, h, jnp.where(i == 0, 0, jnp.maximum(j - 1, 0)), 0)

    return pl.pallas_call(
        kern,
        grid=(b, n_kv, t // tq, nk + 1),
        in_specs=[pl.BlockSpec((None, ATTN_GROUP, tq, hd), lambda b, h, i, j: (b, h, i, 0)),
                  pl.BlockSpec((None, None, tk, hd), k_map),
                  pl.BlockSpec((None, None, tk, 2 * hd), v_map)],
        out_specs=pl.BlockSpec((None, tq, gw), lambda b, h, i, j: (b, i, h)),
        out_shape=jax.ShapeDtypeStruct((b, t, n_q * hd), BF16),
        scratch_shapes=[pltpu.VMEM((rows, tk), F32),
                        pltpu.VMEM((rows, tk), F32),
                        pltpu.VMEM((rows, 1), F32),
                        pltpu.VMEM((rows, 2 * hd), F32)],
        compiler_params=_cparams(("arbitrary", "arbitrary", "arbitrary", "arbitrary")),
        name="flash_attn",
    )(q, k, v)


def _post_tail(x_new, mod_ref, g2_ref, wr_ref, x_out, h_out, lg_out):
    x_out[...] = x_new
    h2 = _norm_mod(x_new, g2_ref[...], mod_ref[3:4, :], mod_ref[4:5, :])
    h_out[...] = h2
    lg_out[...] = lax.dot_general(wr_ref[...], h2, (((1,), (1,)), ((), ())),
                                  precision=HI, preferred_element_type=F32)


def _post_plain_kernel(o_ref, w_ref, x_ref, mod_ref, g2_ref, wr_ref, x_out, h_out, lg_out):
    y = jnp.dot(o_ref[...], w_ref[...], preferred_element_type=F32)
    _post_tail(x_ref[...] + mod_ref[2:3, :] * y, mod_ref, g2_ref, wr_ref, x_out, h_out, lg_out)


def _post_glu_kernel(y_ref, hs_ref, dsk_ref, w_ref, b_ref, x_ref, mod_ref, g2_ref, wr_ref,
                     x_out, h_out, lg_out):
    d = x_ref.shape[-1]
    slabs = [hs_ref[o].astype(F32) for o in range(hs_ref.shape[0])]
    ys = [y_ref[0, o].astype(F32) + y_ref[1, o].astype(F32) for o in range(hs_ref.shape[0])]
    yt = dsk_ref[...] * jnp.concatenate(slabs, axis=1) + jnp.concatenate(ys, axis=1)
    z = jnp.dot(jax.nn.gelu(yt).astype(BF16), w_ref[...], preferred_element_type=F32) + b_ref[...]
    y = z[:, :d] * jax.nn.sigmoid(z[:, d:])
    _post_tail(x_ref[...] + mod_ref[2:3, :] * y, mod_ref, g2_ref, wr_ref, x_out, h_out, lg_out)


def _post_call(kern, name, row_inputs, row_specs, const_inputs, x, modsel, g2, w_router_t):
    b, t, d = x.shape
    tm = ROW_TILE
    ne = w_router_t.shape[0]
    const_specs = [pl.BlockSpec(a.shape, lambda b, i: (0, 0)) for a in const_inputs]
    return pl.pallas_call(
        kern,
        grid=(b, t // tm),
        in_specs=row_specs + const_specs + [
            pl.BlockSpec((None, tm, d), lambda b, i: (b, i, 0)),
            _mod_spec(d),
            pl.BlockSpec((1, d), lambda b, i: (0, 0)),
            pl.BlockSpec((ne, d), lambda b, i: (0, 0))],
        out_specs=[pl.BlockSpec((None, tm, d), lambda b, i: (b, i, 0)),
                   pl.BlockSpec((None, tm, d), lambda b, i: (b, i, 0)),
                   pl.BlockSpec((None, ne, tm), lambda b, i: (b, 0, i))],
        out_shape=[jax.ShapeDtypeStruct((b, t, d), F32),
                   jax.ShapeDtypeStruct((b, t, d), F32),
                   jax.ShapeDtypeStruct((b, ne, t), F32)],
        compiler_params=_cparams(("arbitrary", "arbitrary")),
        name=name,
    )(*row_inputs, *const_inputs, x, modsel, g2.reshape(1, d), w_router_t)


def _ffn_kernel(idx_ref, h_hbm, wa_ref, wu_ref, wd_ref, gate_ref, o_ref,
                xf_scr, xb_scr, acc_scr, sem, *, rows_pad, per_step):
    e, m, f = pl.program_id(0), pl.program_id(1), pl.program_id(2)
    nm, nf = pl.num_programs(1), pl.num_programs(2)
    tm = xb_scr.shape[0]
    lin = e * nm + m
    slot = jnp.bitwise_and(lin, 1)
    last_tile = pl.num_programs(0) * nm - 1

    def row_copy(token, s, r):
        return pltpu.make_async_copy(h_hbm.at[pl.ds(token, 1), :], xf_scr.at[s, pl.ds(r, 1), :], sem.at[s])

    def wait_rows(s):
        pltpu.make_async_copy(h_hbm.at[pl.ds(0, rows_pad), :], xf_scr.at[s], sem.at[s]).wait()

    @pl.when(jnp.logical_and(lin == 0, f == 0))
    def _():
        def body(r, c):
            row_copy(idx_ref[r], 0, r).start()
            return c
        lax.fori_loop(0, rows_pad, body, 0)

    @pl.when(f == 0)
    def _():
        wait_rows(slot)
        xb_scr[...] = xf_scr[slot, 0:tm, :].astype(BF16)
        acc_scr[...] = jnp.zeros(acc_scr.shape, F32)

    base = (lin + 1) * rows_pad + f * per_step
    for k in range(per_step):
        row_copy(idx_ref[base + k], 1 - slot, f * per_step + k).start()

    x = xb_scr[...]
    a = jnp.dot(x, wa_ref[...], preferred_element_type=F32)
    u = jnp.dot(x, wu_ref[...], preferred_element_type=F32)
    hmid = (a * jax.nn.sigmoid(a) * u).astype(BF16)
    acc_scr[...] += jnp.dot(hmid, wd_ref[...], preferred_element_type=F32)

    @pl.when(f == nf - 1)
    def _():
        o_ref[...] = (acc_scr[...] * gate_ref[...]).astype(o_ref.dtype)

    @pl.when(jnp.logical_and(lin == last_tile, f == nf - 1))
    def _():
        wait_rows(1 - slot)


def _expert_ffn(h_tokens, idx, w_up, w_down, gate):
    ne, r = idx.shape
    d = h_tokens.shape[1]
    ff = w_down.shape[1]
    tm = _largest_divisor(r, 1040, 16)
    tf = _largest_divisor(ff, 256, 128)
    nm, nf = r // tm, ff // tf
    per_step = pl.cdiv(pl.cdiv(tm, nf), 8) * 8
    rows_pad = per_step * nf
    tiles = idx.reshape(ne * nm, tm)
    tiles = jnp.pad(tiles, ((0, 1), (0, rows_pad - tm)))
    kern = functools.partial(_ffn_kernel, rows_pad=rows_pad, per_step=per_step)
    return pl.pallas_call(
        kern,
        grid_spec=pltpu.PrefetchScalarGridSpec(
            num_scalar_prefetch=1,
            grid=(ne, nm, nf),
            in_specs=[pl.BlockSpec(memory_space=pl.ANY),
                      pl.BlockSpec((None, d, tf), lambda e, m, f, ix: (e, 0, f)),
                      pl.BlockSpec((None, d, tf), lambda e, m, f, ix: (e, 0, nf + f)),
                      pl.BlockSpec((None, tf, d), lambda e, m, f, ix: (e, f, 0)),
                      pl.BlockSpec((None, tm, 1), lambda e, m, f, ix: (e, m, 0))],
            out_specs=pl.BlockSpec((None, tm, d), lambda e, m, f, ix: (e, m, 0)),
            scratch_shapes=[pltpu.VMEM((2, rows_pad, d), F32),
                            pltpu.VMEM((tm, d), BF16),
                            pltpu.VMEM((tm, d), F32),
                            pltpu.SemaphoreType.DMA((2,))]),
        out_shape=jax.ShapeDtypeStruct((ne, r, d), BF16),
        compiler_params=_cparams(("arbitrary", "arbitrary", "arbitrary")),
        name="expert_ffn",
    )(tiles.reshape(-1), h_tokens, w_up, w_up, w_down, gate)


def _combine_kernel(start_ref, rel_ref, x_ref, mod_ref, *refs):
    y_refs, o_ref = refs[:-1], refs[-1]
    tm = x_ref.shape[0]
    win = y_refs[0].shape[1]
    rel = rel_ref[...]
    lane = lax.broadcasted_iota(jnp.int32, (tm, win), 1)
    acc = jnp.zeros(x_ref.shape, F32)
    for e, y_ref in enumerate(y_refs):
        onehot = jnp.where(rel[:, e:e + 1] == lane, 1.0, 0.0).astype(BF16)
        acc = acc + jnp.dot(onehot, y_ref[0], preferred_element_type=F32)
    o_ref[...] = x_ref[...] + mod_ref[5:6, :] * acc


def _combine(x_mid, modsel, y, rel_t, start):
    b, t, d = x_mid.shape
    ne, r, _ = y.shape
    tm = ROW_TILE
    win = tm + ROW_ALIGN_BF16
    y_specs = [pl.BlockSpec((pl.Element(1), pl.Element(win), pl.Element(d)),
                            lambda b, i, st, e=e: (e, st[b, e, i] * ROW_ALIGN_BF16, 0))
               for e in range(ne)]
    return pl.pallas_call(
        _combine_kernel,
        grid_spec=pltpu.PrefetchScalarGridSpec(
            num_scalar_prefetch=1,
            grid=(b, t // tm),
            in_specs=[pl.BlockSpec((None, tm, ne), lambda b, i, st: (b, i, 0)),
                      pl.BlockSpec((None, tm, d), lambda b, i, st: (b, i, 0)),
                      _mod_spec(d)] + y_specs,
            out_specs=pl.BlockSpec((None, tm, d), lambda b, i, st: (b, i, 0))),
        out_shape=jax.ShapeDtypeStruct((b, t, d), F32),
        compiler_params=_cparams(("arbitrary", "arbitrary")),
        name="moe_combine",
    )(start, rel_t, x_mid, modsel, *([y] * ne))


def _route_segment(aff, cap):
    gate_u, idx_u = lax.top_k(aff, cap)
    kth = gate_u[..., -1:]
    gt = aff > kth
    eq = aff == kth
    need = cap - jnp.sum(gt, axis=-1, keepdims=True, dtype=jnp.int32)
    mask = gt | (eq & (jnp.cumsum(eq.astype(jnp.int32), axis=-1) <= need))
    m32 = mask.astype(jnp.int32)
    pos = jnp.cumsum(m32, axis=-1) - m32
    idx = jnp.sort(idx_u.astype(jnp.int32), axis=-1)
    gate = jnp.take_along_axis(aff, idx, axis=-1)
    return mask, pos, idx, gate


def _moe(x_mid, h2, logits_t, modsel, w_up, w_down, n_ctx):
    b, t, d = x_mid.shape
    ne = logits_t.shape[1]
    tm = ROW_TILE
    win = tm + ROW_ALIGN_BF16
    aff = jax.nn.softmax(logits_t, axis=1)
    cap_l = EC_CAPACITY_FACTOR * (t - n_ctx) // ne
    cap_c = EC_CAPACITY_FACTOR * n_ctx // ne
    r = b * (cap_l + cap_c)
    assert r % ROW_ALIGN_BF16 == 0 and r >= win
    mask_l, pos_l, idx_l, gate_l = _route_segment(aff[:, :, n_ctx:], cap_l)
    mask_c, pos_c, idx_c, gate_c = _route_segment(aff[:, :, :n_ctx], cap_c)
    bi = jnp.arange(b, dtype=jnp.int32)[:, None, None]

    def per_expert(a):
        return a.transpose(1, 0, 2).reshape(ne, -1)

    idx = jnp.concatenate([per_expert(idx_l + bi * t + n_ctx), per_expert(idx_c + bi * t)], axis=1)
    gate = jnp.concatenate([per_expert(gate_l), per_expert(gate_c)], axis=1)
    y = _expert_ffn(h2.reshape(b * t, d), idx, w_up, w_down, gate[..., None])

    row = jnp.concatenate([pos_c + b * cap_l + bi * cap_c, pos_l + bi * cap_l], axis=2)
    mask = jnp.concatenate([mask_c, mask_l], axis=2)
    start = jnp.minimum(row[:, :, ::tm] // ROW_ALIGN_BF16 * ROW_ALIGN_BF16, r - win)
    rel = jnp.where(mask, row - jnp.repeat(start, tm, axis=2), -1)
    return _combine(x_mid, modsel, y, rel.transpose(0, 2, 1), start // ROW_ALIGN_BF16)


def _norm_slab_kernel(x_ref, mod_ref, g_ref, h_ref):
    h = _norm_mod(x_ref[...], g_ref[...], mod_ref[0:1, :], mod_ref[1:2, :]).astype(h_ref.dtype)
    for o in range(h_ref.shape[0]):
        h_ref[o] = h[:, o * LANES:(o + 1) * LANES]


def _norm_slabs(x, modsel, g):
    b, t, d = x.shape
    tm = ROW_TILE
    n_slab = d // LANES
    return pl.pallas_call(
        _norm_slab_kernel,
        grid=(b, t // tm),
        in_specs=[pl.BlockSpec((None, tm, d), lambda b, i: (b, i, 0)),
                  _mod_spec(d),
                  pl.BlockSpec((1, d), lambda b, i: (0, 0))],
        out_specs=pl.BlockSpec((None, n_slab, tm, LANES), lambda b, i: (b, 0, i, 0)),
        out_shape=jax.ShapeDtypeStruct((b, n_slab, t, LANES), BF16),
        compiler_params=_cparams(("arbitrary", "arbitrary")),
        name="norm_mod",
    )(x, modsel, g.reshape(1, d))


def _s5_core_kernel(u_ref, win_ref, toep_ref, wout_ref, a_ref, y_ref, zh_scr, *, m_ctx):
    dr = pl.program_id(2)
    m = u_ref.shape[0]
    rc = _largest_divisor(m, 512, 16)
    half = zh_scr.shape[1] // 2
    for r0 in range(0, m, rc):
        zh_scr[r0:r0 + rc, :] = jnp.dot(u_ref[r0:r0 + rc, :], win_ref[...], preferred_element_type=F32)
    ar = a_ref[0:1, :]
    ai = a_ref[1:2, :]
    rid = lax.broadcasted_iota(jnp.int32, (8, half), 0)

    def visit(blk, carry, reverse):
        sr, si = carry
        base = pl.multiple_of(blk * 8, 8)
        z8 = zh_scr[pl.ds(base, 8), :]
        hr = jnp.zeros((8, half), F32)
        hi = jnp.zeros((8, half), F32)
        for r in (range(7, -1, -1) if reverse else range(8)):
            hr = jnp.where(rid == r, sr, hr)
            hi = jnp.where(rid == r, si, hi)
            zr = z8[r:r + 1, 0:half]
            zi = z8[r:r + 1, half:2 * half]
            sr, si = ar * sr - ai * si + zr, ar * si + ai * sr + zi
        zh_scr[pl.ds(base, 8), 0:half] = hr
        zh_scr[pl.ds(base, 8), half:2 * half] = hi
        return sr, si

    zero = (jnp.zeros((1, half), F32), jnp.zeros((1, half), F32))
    nb, nb_ctx = m // 8, m_ctx // 8

    @pl.when(dr == 0)
    def _():
        lax.fori_loop(0, nb, lambda s, cr: visit(s, cr, False), zero)

    @pl.when(dr == 1)
    def _():
        carry = lax.fori_loop(0, nb_ctx, lambda s, cr: visit(nb_ctx - 1 - s, cr, True), zero)
        lax.fori_loop(0, nb - nb_ctx, lambda s, cr: visit(nb - 1 - s, cr, True), carry)

    for r0 in range(0, m, rc):
        y = (jnp.dot(u_ref[r0:r0 + rc, :], toep_ref[...], preferred_element_type=F32)
             + jnp.dot(zh_scr[r0:r0 + rc, :].astype(BF16), wout_ref[...], preferred_element_type=F32))
        y_ref[r0:r0 + rc, :] = y.astype(y_ref.dtype)


def _s5_weights(lam_re, lam_im, log_dt, b_re, b_im, c_re, c_im):
    sub = S5_SUB
    gs = S5_GROUP_SIZE
    gps = LANES // gs
    outs = []
    for dr in range(2):
        lre = jnp.minimum(lam_re[dr].astype(F32), -S5_MIN_DECAY)
        lim = lam_im[dr].astype(F32)
        dt = jnp.exp(log_dt[dr].astype(F32))[:, None]
        mag = jnp.exp(lre * dt)
        ang = lim * dt
        lbr, lbi = mag * jnp.cos(ang), mag * jnp.sin(ang)
        den = lre * lre + lim * lim
        f_re = ((lbr - 1) * lre + lbi * lim) / den
        f_im = (lbi * lre - (lbr - 1) * lim) / den
        bre, bim = b_re[dr].astype(F32), b_im[dr].astype(F32)
        bbr = f_re[..., None] * bre - f_im[..., None] * bim
        bbi = f_re[..., None] * bim + f_im[..., None] * bre
        cr, ci = c_re[dr].astype(F32), c_im[dr].astype(F32)
        pr, pi = [jnp.ones_like(lbr)], [jnp.zeros_like(lbr)]
        for _ in range(sub):
            pr.append(pr[-1] * lbr - pi[-1] * lbi)
            pi.append(pr[-2] * lbi + pi[-1] * lbr)
        pw_r, pw_i = jnp.stack(pr), jnp.stack(pi)
        pb_r = pw_r[..., None] * bbr - pw_i[..., None] * bbi
        pb_i = pw_r[..., None] * bbi + pw_i[..., None] * bbr
        cp_r = cr[None] * pw_r[:, :, None, :] - ci[None] * pw_i[:, :, None, :]
        cp_i = cr[None] * pw_i[:, :, None, :] + ci[None] * pw_r[:, :, None, :]
        kk = (jnp.einsum('gip,tgpj->tgij', cr, pb_r[:sub], precision=HI)
              - jnp.einsum('gip,tgpj->tgij', ci, pb_i[:sub], precision=HI))
        s_idx = jnp.arange(sub)
        lag = (s_idx[None, :] - s_idx[:, None]) if dr == 0 else (s_idx[:, None] - s_idx[None, :])
        kt = kk[jnp.clip(lag, 0, sub - 1)]
        kt = jnp.where((lag >= 0)[:, :, None, None, None], kt, 0.0)
        g_n = kt.shape[2]
        n_slab = g_n // gps
        eye = jnp.eye(gps, dtype=F32)
        toep = jnp.einsum('stogij,gh->osgjthi', kt.reshape(sub, sub, n_slab, gps, gs, gs), eye)
        toep = toep.reshape(n_slab, sub * LANES, sub * LANES)
        e_in = (sub - 1 - s_idx) if dr == 0 else s_idx

        def in_map(pb):
            w = jnp.einsum('sogpj,gh->osgjhp', pb[e_in].reshape(sub, n_slab, gps, S5_STATE, gs), eye)
            return w.reshape(n_slab, sub * LANES, gps * S5_STATE)

        win = jnp.concatenate([in_map(pb_r), in_map(pb_i)], axis=2)
        e_out = (s_idx + 1) if dr == 0 else (sub - s_idx)

        def out_map(cp):
            w = jnp.einsum('togip,gh->ogpthi', cp[e_out].reshape(sub, n_slab, gps, gs, S5_STATE), eye)
            return w.reshape(n_slab, gps * S5_STATE, sub * LANES)

        wout = jnp.concatenate([out_map(cp_r), -out_map(cp_i)], axis=1)
        a_sub = jnp.stack([pw_r[sub].reshape(n_slab, gps * S5_STATE),
                           pw_i[sub].reshape(n_slab, gps * S5_STATE)], axis=1)
        outs.append((win, toep, wout, a_sub))

    stack = lambda k, dt: jnp.stack([o[k] for o in outs], axis=1).astype(dt)
    return stack(0, BF16), stack(1, BF16), stack(2, BF16), stack(3, F32)


def _s5_core(hs, weights, n_ctx):
    b, n_slab, t, _ = hs.shape
    sub = S5_SUB
    win, toep, wout, a_sub = weights
    m = t // sub
    m_ctx = n_ctx // sub
    assert m % 8 == 0 and m_ctx % 8 == 0
    cw = sub * LANES
    sw = win.shape[-1]
    u = hs.reshape(b, n_slab, m, cw)
    kern = functools.partial(_s5_core_kernel, m_ctx=m_ctx)
    y = pl.pallas_call(
        kern,
        grid=(n_slab, b, 2),
        in_specs=[pl.BlockSpec((None, None, m, cw), lambda o, b, d: (b, o, 0, 0)),
                  pl.BlockSpec((None, None, cw, sw), lambda o, b, d: (o, d, 0, 0)),
                  pl.BlockSpec((None, None, cw, cw), lambda o, b, d: (o, d, 0, 0)),
                  pl.BlockSpec((None, None, sw, cw), lambda o, b, d: (o, d, 0, 0)),
                  pl.BlockSpec((None, None, 2, sw // 2), lambda o, b, d: (o, d, 0, 0))],
        out_specs=pl.BlockSpec((None, None, None, m, cw), lambda o, b, d: (d, b, o, 0, 0)),
        out_shape=jax.ShapeDtypeStruct((2, b, n_slab, m, cw), BF16),
        scratch_shapes=[pltpu.VMEM((m, sw), F32)],
        compiler_params=_cparams(("arbitrary", "arbitrary", "arbitrary")),
        name="s5_core",
    )(u, win, toep, wout, a_sub)
    return y.reshape(2, b, n_slab, t, LANES)


def _ret_in_kernel(x_ref, mod_ref, g_ref, w_ref, cos_ref, sin_ref, z_ref, *, n_rope_cols):
    h = _norm_mod(x_ref[...], g_ref[...], mod_ref[0:1, :], mod_ref[1:2, :]).astype(BF16)
    cos, sin = cos_ref[...], sin_ref[...]
    half = cos.shape[1]
    tn = 4 * half
    for n0 in range(0, w_ref.shape[1], tn):
        z = jnp.dot(h, w_ref[:, n0:n0 + tn], preferred_element_type=F32)
        if n0 < n_rope_cols:
            for c0 in range(0, tn, 2 * half):
                x1 = z[:, c0:c0 + half]
                x2 = z[:, c0 + half:c0 + 2 * half]
                z_ref[:, n0 + c0:n0 + c0 + half] = (x1 * cos - x2 * sin).astype(z_ref.dtype)
                z_ref[:, n0 + c0 + half:n0 + c0 + 2 * half] = (x2 * cos + x1 * sin).astype(z_ref.dtype)
        else:
            z_ref[:, n0:n0 + tn] = z.astype(z_ref.dtype)


def _ret_in(x, modsel, g, w, cos, sin, n_rope_cols):
    b, t, d = x.shape
    n_out = w.shape[1]
    tm = ROW_TILE
    half = cos.shape[1]
    assert n_out % (4 * half) == 0 and n_rope_cols % (4 * half) == 0
    kern = functools.partial(_ret_in_kernel, n_rope_cols=n_rope_cols)
    return pl.pallas_call(
        kern,
        grid=(b, t // tm),
        in_specs=[pl.BlockSpec((None, tm, d), lambda b, i: (b, i, 0)),
                  _mod_spec(d),
                  pl.BlockSpec((1, d), lambda b, i: (0, 0)),
                  pl.BlockSpec((d, n_out), lambda b, i: (0, 0), pipeline_mode=pl.Buffered(1)),
                  pl.BlockSpec((tm, half), lambda b, i: (i, 0)),
                  pl.BlockSpec((tm, half), lambda b, i: (i, 0))],
        out_specs=pl.BlockSpec((None, tm, n_out), lambda b, i: (b, i, 0)),
        out_shape=jax.ShapeDtypeStruct((b, t, n_out), BF16),
        compiler_params=_cparams(("arbitrary", "arbitrary")),
        name="ret_in",
    )(x, modsel, g.reshape(1, d), w, cos, sin)


def _ret_core_kernel(lg_ref, q_ref, k_ref, v_ref, *rest, rev, qk_scale):
    if rev:
        of_ref, g_ref, o_ref, s_scr = rest
    else:
        o_ref, s_scr = rest
    hh = pl.program_id(1)
    c = pl.program_id(2)
    tc = q_ref.shape[0]

    @pl.when(c == 0)
    def _():
        s_scr[...] = jnp.zeros(s_scr.shape, F32)

    lg = jnp.full((1, 1), lg_ref[hh], F32)
    row = lax.broadcasted_iota(jnp.int32, (tc, tc), 0)
    col = lax.broadcasted_iota(jnp.int32, (tc, tc), 1)
    diff = (col - row) if rev else (row - col)
    inner = jnp.where(diff >= 0, jnp.exp(lg * jnp.maximum(diff, 0).astype(F32)), 0.0) * qk_scale
    pos = lax.broadcasted_iota(jnp.int32, (tc, 1), 0).astype(F32)
    if rev:
        q_dec = jnp.exp(lg * (tc - pos))
        k_dec = jnp.exp(lg * pos) * qk_scale
    else:
        q_dec = jnp.exp(lg * (pos + 1.0))
        k_dec = jnp.exp(lg * (tc - 1.0 - pos)) * qk_scale
    q = q_ref[...]
    k = k_ref[...]
    v = v_ref[...]
    s = lax.dot_general(q, k, (((1,), (1,)), ((), ())), preferred_element_type=F32) * inner
    state = s_scr[...]
    o = (jnp.dot(s.astype(BF16), v, preferred_element_type=F32)
         + jnp.dot((q.astype(F32) * q_dec).astype(BF16), state.astype(BF16), preferred_element_type=F32))
    kd_t = (k.astype(F32) * k_dec).T.astype(BF16)
    s_scr[...] = state * jnp.exp(lg * tc) + jnp.dot(kd_t, v, preferred_element_type=F32)
    if rev:
        tot = of_ref[...] + o
        nrm = tot * lax.rsqrt(jnp.mean(tot * tot, axis=-1, keepdims=True) + NORM_EPS)
        gg = g_ref[...].astype(F32)
        o_ref[...] = (gg * jax.nn.sigmoid(gg) * nrm).astype(o_ref.dtype)
    else:
        o_ref[...] = o


def _ret_core(z, log_gamma, o_fwd, rev):
    b, t, _ = z.shape
    nh = RET_HEADS
    dqk = z.shape[2] // (6 * nh)
    dv = 2 * dqk
    tc = ROW_TILE
    nc = t // tc

    if rev:
        def cmap(c):
            return jnp.where(c == 0, 0, nc - c)
    else:
        def cmap(c):
            return c

    kern = functools.partial(_ret_core_kernel, rev=rev, qk_scale=dqk ** -0.5)
    in_specs = [pl.BlockSpec((None, tc, dqk), lambda b, h, c, lg: (b, cmap(c), h)),
                pl.BlockSpec((None, tc, dqk), lambda b, h, c, lg: (b, cmap(c), nh + h)),
                pl.BlockSpec((None, tc, dv), lambda b, h, c, lg: (b, cmap(c), nh + h))]
    args = [z, z, z]
    if rev:
        in_specs += [pl.BlockSpec((None, tc, dv), lambda b, h, c, lg: (b, cmap(c), h)),
                     pl.BlockSpec((None, tc, dv), lambda b, h, c, lg: (b, cmap(c), 2 * nh + h))]
        args += [o_fwd, z]
    out_dtype = BF16 if rev else F32
    return pl.pallas_call(
        kern,
        grid_spec=pltpu.PrefetchScalarGridSpec(
            num_scalar_prefetch=1,
            grid=(b, nh, nc),
            in_specs=in_specs,
            out_specs=pl.BlockSpec((None, tc, dv), lambda b, h, c, lg: (b, cmap(c), h)),
            scratch_shapes=[pltpu.VMEM((dqk, dv), F32)]),
        out_shape=jax.ShapeDtypeStruct((b, t, nh * dv), out_dtype),
        compiler_params=_cparams(("arbitrary", "arbitrary", "arbitrary")),
        name="ret_core_bwd" if rev else "ret_core_fwd",
    )(log_gamma, *args)


def _final_kernel(x_ref, g_ref, o_ref):
    x = x_ref[...]
    o_ref[...] = x * lax.rsqrt(jnp.mean(x * x, axis=-1, keepdims=True) + NORM_EPS) * g_ref[...]


def _final_norm(x, g, n_ctx):
    b, t, d = x.shape
    tm = ROW_TILE
    skip = n_ctx // tm
    return pl.pallas_call(
        _final_kernel,
        grid=(b, (t - n_ctx) // tm),
        in_specs=[pl.BlockSpec((None, tm, d), lambda b, i: (b, i + skip, 0)),
                  pl.BlockSpec((1, d), lambda b, i: (0, 0))],
        out_specs=pl.BlockSpec((None, tm, d), lambda b, i: (b, i, 0)),
        out_shape=jax.ShapeDtypeStruct((b, t - n_ctx, d), F32),
        compiler_params=_cparams(("arbitrary", "arbitrary")),
        name="final_norm",
    )(x, g.reshape(1, d))


def _rope_tables(n_ctx, n_lat, head_dim):
    rows = n_lat // GRID_W
    row = jnp.repeat(jnp.arange(rows, dtype=F32), GRID_W)
    col = jnp.tile(jnp.arange(GRID_W, dtype=F32), rows)
    n_freq = head_dim // 4
    inv_freq = ROPE_THETA ** (-jnp.arange(n_freq, dtype=F32) / n_freq)
    ang = jnp.concatenate([row[:, None] * inv_freq, col[:, None] * inv_freq], axis=-1)
    cos = jnp.concatenate([jnp.ones((n_ctx, head_dim // 2), F32), jnp.cos(ang)], axis=0)
    sin = jnp.concatenate([jnp.zeros((n_ctx, head_dim // 2), F32), jnp.sin(ang)], axis=0)
    return cos, sin


def kernel(x, c, ctx, c_ctx, mod_w, mod_b, norm_g, attn_w_qkv, attn_w_o, attn_q_gain, attn_k_gain,
           s5_lambda_re, s5_lambda_im, s5_log_dt, s5_b_re, s5_b_im, s5_c_re, s5_c_im, s5_d, s5_w_glu, s5_b_glu,
           ret_w_qkvg, ret_w_o, ret_log_decay, moe_w_router, moe_w_up, moe_w_down, final_g):
    bsz, n_lat, d = x.shape
    n_ctx = ctx.shape[1]
    depth = mod_w.shape[0]
    assert n_ctx == ROW_TILE and n_lat % ROW_TILE == 0 and bsz <= 7

    xs = jnp.concatenate([ctx, x], axis=1).astype(F32)
    cvec = jnp.zeros((8, d), F32).at[:bsz].set(c).at[bsz].set(c_ctx)
    mods = _modulation(cvec, mod_w, mod_b)

    cos_a, sin_a = _rope_tables(n_ctx, n_lat, ATTN_HEAD_DIM)
    cos2 = jnp.concatenate([cos_a, cos_a], axis=1)
    sin2 = jnp.concatenate([-sin_a, sin_a], axis=1)
    cos_r, sin_r = _rope_tables(n_ctx, n_lat, d // RET_HEADS)

    tm = ROW_TILE

    def row_spec(width):
        return pl.BlockSpec((None, tm, width), lambda b, i: (b, i, 0))

    for i in range(depth):
        kind, j = i % N_MIXERS, i // N_MIXERS
        m6 = mods[i].reshape(8, 6, d)
        modsel = jnp.stack([jnp.broadcast_to(m6[bsz], (bsz, 6, d)), m6[:bsz]], axis=1)
        wr_t = moe_w_router[i].T.astype(F32)
        if kind == 0:
            q, k, v = _attn_in(xs, modsel, norm_g[i, 0], attn_w_qkv[j].astype(BF16),
                               attn_q_gain[j], attn_k_gain[j], cos2, sin2)
            o = _flash(q, k, v, n_ctx)
            x_mid, h2, lg = _post_call(_post_plain_kernel, "attn_out", [o], [row_spec(o.shape[-1])],
                                       [attn_w_o[j].astype(BF16)], xs, modsel, norm_g[i, 1], wr_t)
        elif kind == 1:
            hs = _norm_slabs(xs, modsel, norm_g[i, 0])
            w5 = _s5_weights(s5_lambda_re[j], s5_lambda_im[j], s5_log_dt[j], s5_b_re[j], s5_b_im[j],
                             s5_c_re[j], s5_c_im[j])
            y = _s5_core(hs, w5, n_ctx)
            n_slab = d // LANES
            slab_specs = [pl.BlockSpec((2, None, n_slab, tm, LANES), lambda b, i: (0, b, 0, i, 0)),
                          pl.BlockSpec((None, n_slab, tm, LANES), lambda b, i: (b, 0, i, 0))]
            x_mid, h2, lg = _post_call(_post_glu_kernel, "s5_out", [y, hs], slab_specs,
                                       [s5_d[j].reshape(1, d).astype(F32), s5_w_glu[j].astype(BF16),
                                        s5_b_glu[j].reshape(1, -1).astype(F32)],
                                       xs, modsel, norm_g[i, 1], wr_t)
        else:
            z = _ret_in(xs, modsel, norm_g[i, 0], ret_w_qkvg[j].astype(BF16), cos_r, sin_r, 2 * d)
            o_f = None
            for dr in range(2):
                log_gamma = -jnp.abs(ret_log_decay[j, dr].astype(F32))
                o_f = _ret_core(z, log_gamma, o_f, rev=(dr == 1))
            x_mid, h2, lg = _post_call(_post_plain_kernel, "ret_out", [o_f], [row_spec(o_f.shape[-1])],
                                       [ret_w_o[j].astype(BF16)], xs, modsel, norm_g[i, 1], wr_t)
        xs = _moe(x_mid, h2, lg, modsel, moe_w_up[i].astype(BF16), moe_w_down[i].astype(BF16), n_ctx)

    return _final_norm(xs, final_g, n_ctx).astype(x.dtype)
```

```python
import functools
import math

import jax
import jax.numpy as jnp
from jax import lax
from jax.experimental import pallas as pl
from jax.experimental.pallas import tpu as pltpu

F32 = jnp.float32
BF16 = jnp.bfloat16

GRID_W = 64
N_MIXERS = 3
NORM_EPS = 1e-6
ROPE_THETA = 10000.0
ATTN_HEAD_DIM = 128
ATTN_GROUP = 4
S5_GROUP_SIZE = 16
S5_STATE = 64
S5_MIN_DECAY = 1e-4
RET_HEADS = 4
N_EXPERTS = 16
EC_CAPACITY_FACTOR = 2

LANES = 128
ROW_ALIGN_BF16 = 16
ROW_TILE = 256
COMBINE_TILE = 128
S5_SUB = 8
VMEM_LIMIT = 48 * 1024 * 1024
HI = lax.Precision.HIGHEST


def _cparams(sem):
    return pltpu.CompilerParams(dimension_semantics=sem, vmem_limit_bytes=VMEM_LIMIT)


def _norm_mod(x, g, shift, scale):
    ms = jnp.mean(x * x, axis=-1, keepdims=True)
    return (x * lax.rsqrt(ms + NORM_EPS) * g) * (1.0 + scale) + shift


def _largest_divisor(n, cap, mult):
    best = None
    for t in range(mult, min(n, cap) + 1, mult):
        if n % t == 0:
            best = t
    assert best is not None, (n, cap, mult)
    return best


def _mod_kernel(c_ref, w_ref, b_ref, o_ref):
    c = c_ref[...]
    s = c * jax.nn.sigmoid(c)
    o_ref[...] = jnp.dot(s, w_ref[...], precision=HI, preferred_element_type=F32) + b_ref[...]


def _modulation(cvec, mod_w, mod_b):
    depth, d, n = mod_w.shape
    tn = _largest_divisor(n, 1536, 128)
    return pl.pallas_call(
        _mod_kernel,
        grid=(depth, n // tn),
        in_specs=[pl.BlockSpec((8, d), lambda l, j: (0, 0)),
                  pl.BlockSpec((None, d, tn), lambda l, j: (l, 0, j)),
                  pl.BlockSpec((None, 1, tn), lambda l, j: (l, 0, j))],
        out_specs=pl.BlockSpec((None, 8, tn), lambda l, j: (l, 0, j)),
        out_shape=jax.ShapeDtypeStruct((depth, 8, n), F32),
        compiler_params=_cparams(("arbitrary", "arbitrary")),
        name="modulation",
    )(cvec, mod_w, mod_b.reshape(depth, 1, n))


def _mod_spec(d):
    return pl.BlockSpec((None, None, 6, d), lambda b, i, *_: (b, jnp.minimum(i, 1), 0, 0))


def _attn_in_kernel(x_ref, mod_ref, g_ref, w_ref, qg_ref, kg_ref, cos_ref, sin_ref,
                    q_ref, k_ref, v_ref, *, n_q, n_kv):
    hd = ATTN_HEAD_DIM
    h = _norm_mod(x_ref[...], g_ref[...], mod_ref[0:1, :], mod_ref[1:2, :]).astype(BF16)
    qkv = jnp.dot(h, w_ref[...], preferred_element_type=F32)
    cos, sin = cos_ref[...], sin_ref[...]

    def norm_rope(t, gain):
        t = t * lax.rsqrt(jnp.mean(t * t, axis=-1, keepdims=True) + NORM_EPS) * gain
        return t * cos + pltpu.roll(t, hd // 2, axis=1) * sin

    scale = hd ** -0.5 * math.log2(math.e)
    for j in range(n_q):
        q_ref[j] = (norm_rope(qkv[:, j * hd:(j + 1) * hd], qg_ref[...]) * scale).astype(BF16)
    lane = lax.broadcasted_iota(jnp.int32, (qkv.shape[0], hd), 1)
    ones_col = jnp.where(lane == 0, 1.0, 0.0).astype(BF16)
    for j in range(n_kv):
        c0 = (n_q + j) * hd
        k_ref[j] = norm_rope(qkv[:, c0:c0 + hd], kg_ref[...]).astype(BF16)
        c1 = (n_q + n_kv + j) * hd
        v_ref[j, :, 0:hd] = qkv[:, c1:c1 + hd].astype(BF16)
        v_ref[j, :, hd:2 * hd] = ones_col


def _attn_in(x, modsel, g, w_qkv, q_gain, k_gain, cos2, sin2):
    b, t, d = x.shape
    hd = ATTN_HEAD_DIM
    n_tot = w_qkv.shape[1] // hd
    n_q = d // hd
    n_kv = (n_tot - n_q) // 2
    tm = ROW_TILE
    kern = functools.partial(_attn_in_kernel, n_q=n_q, n_kv=n_kv)
    return pl.pallas_call(
        kern,
        grid=(b, t // tm),
        in_specs=[pl.BlockSpec((None, tm, d), lambda b, i: (b, i, 0)),
                  _mod_spec(d),
                  pl.BlockSpec((1, d), lambda b, i: (0, 0)),
                  pl.BlockSpec(w_qkv.shape, lambda b, i: (0, 0)),
                  pl.BlockSpec((1, hd), lambda b, i: (0, 0)),
                  pl.BlockSpec((1, hd), lambda b, i: (0, 0)),
                  pl.BlockSpec((tm, hd), lambda b, i: (i, 0)),
                  pl.BlockSpec((tm, hd), lambda b, i: (i, 0))],
        out_specs=[pl.BlockSpec((None, n_q, tm, hd), lambda b, i: (b, 0, i, 0)),
                   pl.BlockSpec((None, n_kv, tm, hd), lambda b, i: (b, 0, i, 0)),
                   pl.BlockSpec((None, n_kv, tm, 2 * hd), lambda b, i: (b, 0, i, 0))],
        out_shape=[jax.ShapeDtypeStruct((b, n_q, t, hd), BF16),
                   jax.ShapeDtypeStruct((b, n_kv, t, hd), BF16),
                   jax.ShapeDtypeStruct((b, n_kv, t, 2 * hd), BF16)],
        compiler_params=_cparams(("arbitrary", "arbitrary")),
        name="attn_in",
    )(x, modsel, g.reshape(1, d), w_qkv, q_gain.reshape(1, hd), k_gain.reshape(1, hd), cos2, sin2)


def _flash_kernel(q_ref, k_ref, v_ref, o_ref, s0_scr, s1_scr, m_scr, acc_scr, *, tq, n_ctx):
    hd = ATTN_HEAD_DIM
    qi = pl.program_id(2)
    j = pl.program_id(3)
    nk = pl.num_programs(3) - 1
    m_init = -1e30

    @pl.when(j == 0)
    def _():
        m_scr[...] = jnp.full(m_scr.shape, m_init, F32)
        acc_scr[...] = jnp.zeros(acc_scr.shape, F32)
        s1_scr[...] = jnp.full(s1_scr.shape, -jnp.inf, F32)

    def softmax_update(s, v):
        m_prev = m_scr[...]
        m_new = jnp.maximum(m_prev, jnp.max(s, axis=1, keepdims=True))
        alpha = jnp.exp2(m_prev - m_new)
        p = jnp.exp2((s - m_new).astype(BF16))
        acc_scr[...] = alpha * acc_scr[...] + jnp.dot(p, v, preferred_element_type=F32)
        m_scr[...] = m_new

    def scores(k):
        q = q_ref[...].reshape(ATTN_GROUP * tq, hd)
        return lax.dot_general(q, k, (((1,), (1,)), ((), ())), preferred_element_type=F32)

    def pipelined(s_new, s_old):
        s_new[...] = scores(k_ref[...])
        softmax_update(s_old[...], v_ref[...])

    latent = qi > 0
    odd = jnp.bitwise_and(j, 1)

    @pl.when(jnp.logical_and(latent, odd == 0))
    def _():
        pipelined(s0_scr, s1_scr)

    @pl.when(jnp.logical_and(latent, odd == 1))
    def _():
        pipelined(s1_scr, s0_scr)

    @pl.when(jnp.logical_and(qi == 0, j == 0))
    def _():
        softmax_update(scores(k_ref[0:n_ctx, :]), v_ref[0:n_ctx, :])

    @pl.when(j == nk)
    def _():
        o = acc_scr[:, 0:hd] / acc_scr[:, hd:hd + 1]
        for g in range(ATTN_GROUP):
            o_ref[:, g * hd:(g + 1) * hd] = o[g * tq:(g + 1) * tq, :].astype(o_ref.dtype)


def _flash(q, k, v, n_ctx):
    b, n_q, t, hd = q.shape
    n_kv = k.shape[1]
    tq = ROW_TILE
    assert n_ctx == tq and n_q == n_kv * ATTN_GROUP
    tk = _largest_divisor(t, 1280, 256)
    nk = t // tk
    gw = ATTN_GROUP * hd
    rows = ATTN_GROUP * tq
    kern = functools.partial(_flash_kernel, tq=tq, n_ctx=n_ctx)

    def k_map(b, h, i, j):
        return (b, h, jnp.where(i == 0, 0, jnp.minimum(j, nk - 1)), 0)

    def v_map(b, h, i, j):
        return (b, h, jnp.where(i == 0, 0, jnp.maximum(j - 1, 0)), 0)

    return pl.pallas_call(
        kern,
        grid=(b, n_kv, t // tq, nk + 1),
        in_specs=[pl.BlockSpec((None, ATTN_GROUP, tq, hd), lambda b, h, i, j: (b, h, i, 0)),
                  pl.BlockSpec((None, None, tk, hd), k_map),
                  pl.BlockSpec((None, None, tk, 2 * hd), v_map)],
        out_specs=pl.BlockSpec((None, tq, gw), lambda b, h, i, j: (b, i, h)),
        out_shape=jax.ShapeDtypeStruct((b, t, n_q * hd), BF16),
        scratch_shapes=[pltpu.VMEM((rows, tk), F32),
                        pltpu.VMEM((rows, tk), F32),
                        pltpu.VMEM((rows, 1), F32),
                        pltpu.VMEM((rows, 2 * hd), F32)],
        compiler_params=_cparams(("arbitrary", "arbitrary", "arbitrary", "arbitrary")),
        name="flash_attn",
    )(q, k, v)


def _post_tail(x_new, mod_ref, g2_ref, wr_ref, x_out, h_out, lg_out):
    x_out[...] = x_new
    h2 = _norm_mod(x_new, g2_ref[...], mod_ref[3:4, :], mod_ref[4:5, :])
    h_out[...] = h2
    lg_out[...] = lax.dot_general(wr_ref[...], h2, (((1,), (1,)), ((), ())),
                                  precision=HI, preferred_element_type=F32)


def _post_plain_kernel(o_ref, w_ref, x_ref, mod_ref, g2_ref, wr_ref, x_out, h_out, lg_out):
    y = jnp.dot(o_ref[...], w_ref[...], preferred_element_type=F32)
    _post_tail(x_ref[...] + mod_ref[2:3, :] * y, mod_ref, g2_ref, wr_ref, x_out, h_out, lg_out)


def _post_glu_kernel(y_ref, hs_ref, dsk_ref, w_ref, b_ref, x_ref, mod_ref, g2_ref, wr_ref,
                     x_out, h_out, lg_out):
    d = x_ref.shape[-1]
    slabs = [hs_ref[o].astype(F32) for o in range(hs_ref.shape[0])]
    ys = [y_ref[0, o].astype(F32) + y_ref[1, o].astype(F32) for o in range(hs_ref.shape[0])]
    yt = dsk_ref[...] * jnp.concatenate(slabs, axis=1) + jnp.concatenate(ys, axis=1)
    z = jnp.dot(jax.nn.gelu(yt).astype(BF16), w_ref[...], preferred_element_type=F32) + b_ref[...]
    y = z[:, :d] * jax.nn.sigmoid(z[:, d:])
    _post_tail(x_ref[...] + mod_ref[2:3, :] * y, mod_ref, g2_ref, wr_ref, x_out, h_out, lg_out)


def _post_call(kern, name, row_inputs, row_specs, const_inputs, x, modsel, g2, w_router_t):
    b, t, d = x.shape
    tm = ROW_TILE
    ne = w_router_t.shape[0]
    const_specs = [pl.BlockSpec(a.shape, lambda b, i: (0, 0)) for a in const_inputs]
    return pl.pallas_call(
        kern,
        grid=(b, t // tm),
        in_specs=row_specs + const_specs + [
            pl.BlockSpec((None, tm, d), lambda b, i: (b, i, 0)),
            _mod_spec(d),
            pl.BlockSpec((1, d), lambda b, i: (0, 0)),
            pl.BlockSpec((ne, d), lambda b, i: (0, 0))],
        out_specs=[pl.BlockSpec((None, tm, d), lambda b, i: (b, i, 0)),
                   pl.BlockSpec((None, tm, d), lambda b, i: (b, i, 0)),
                   pl.BlockSpec((None, ne, tm), lambda b, i: (b, 0, i))],
        out_shape=[jax.ShapeDtypeStruct((b, t, d), F32),
                   jax.ShapeDtypeStruct((b, t, d), F32),
                   jax.ShapeDtypeStruct((b, ne, t), F32)],
        compiler_params=_cparams(("arbitrary", "arbitrary")),
        name=name,
    )(*row_inputs, *const_inputs, x, modsel, g2.reshape(1, d), w_router_t)


def _ffn_kernel(idx_ref, h_hbm, wa_ref, wu_ref, wd_ref, gate_ref, o_ref,
                xf_scr, xb_scr, acc_scr, sem, *, rows_pad, per_step):
    e, m, f = pl.program_id(0), pl.program_id(1), pl.program_id(2)
    nm, nf = pl.num_programs(1), pl.num_programs(2)
    tm = xb_scr.shape[0]
    lin = e * nm + m
    slot = jnp.bitwise_and(lin, 1)
    last_tile = pl.num_programs(0) * nm - 1

    def row_copy(token, s, r):
        return pltpu.make_async_copy(h_hbm.at[pl.ds(token, 1), :], xf_scr.at[s, pl.ds(r, 1), :], sem.at[s])

    def wait_rows(s):
        pltpu.make_async_copy(h_hbm.at[pl.ds(0, rows_pad), :], xf_scr.at[s], sem.at[s]).wait()

    @pl.when(jnp.logical_and(lin == 0, f == 0))
    def _():
        def body(r, c):
            row_copy(idx_ref[r], 0, r).start()
            return c
        lax.fori_loop(0, rows_pad, body, 0)

    @pl.when(f == 0)
    def _():
        wait_rows(slot)
        xb_scr[...] = xf_scr[slot, 0:tm, :].astype(BF16)
        acc_scr[...] = jnp.zeros(acc_scr.shape, F32)

    base = (lin + 1) * rows_pad + f * per_step
    for k in range(per_step):
        row_copy(idx_ref[base + k], 1 - slot, f * per_step + k).start()

    x = xb_scr[...]
    a = jnp.dot(x, wa_ref[...].astype(BF16), preferred_element_type=F32)
    u = jnp.dot(x, wu_ref[...].astype(BF16), preferred_element_type=F32)
    hmid = (a * jax.nn.sigmoid(a) * u).astype(BF16)
    acc_scr[...] += jnp.dot(hmid, wd_ref[...].astype(BF16), preferred_element_type=F32)

    @pl.when(f == nf - 1)
    def _():
        o_ref[...] = (acc_scr[...] * gate_ref[...]).astype(o_ref.dtype)

    @pl.when(jnp.logical_and(lin == last_tile, f == nf - 1))
    def _():
        wait_rows(1 - slot)


def _expert_ffn(h_tokens, idx, w_up, w_down, layer, gate):
    ne, r = idx.shape
    d = h_tokens.shape[1]
    ff = w_down.shape[2]
    tm = _largest_divisor(r, 1040, 16)
    tf = _largest_divisor(ff, 256, 128)
    nm, nf = r // tm, ff // tf
    per_step = pl.cdiv(pl.cdiv(tm, nf), 8) * 8
    rows_pad = per_step * nf
    tiles = idx.reshape(ne * nm, tm)
    tiles = jnp.pad(tiles, ((0, 1), (0, rows_pad - tm)))
    kern = functools.partial(_ffn_kernel, rows_pad=rows_pad, per_step=per_step)
    return pl.pallas_call(
        kern,
        grid_spec=pltpu.PrefetchScalarGridSpec(
            num_scalar_prefetch=1,
            grid=(ne, nm, nf),
            in_specs=[pl.BlockSpec(memory_space=pl.ANY),
                      pl.BlockSpec((None, None, d, tf), lambda e, m, f, ix: (layer, e, 0, f)),
                      pl.BlockSpec((None, None, d, tf), lambda e, m, f, ix: (layer, e, 0, nf + f)),
                      pl.BlockSpec((None, None, tf, d), lambda e, m, f, ix: (layer, e, f, 0)),
                      pl.BlockSpec((None, tm, 1), lambda e, m, f, ix: (e, m, 0))],
            out_specs=pl.BlockSpec((None, tm, d), lambda e, m, f, ix: (e, m, 0)),
            scratch_shapes=[pltpu.VMEM((2, rows_pad, d), F32),
                            pltpu.VMEM((tm, d), BF16),
                            pltpu.VMEM((tm, d), F32),
                            pltpu.SemaphoreType.DMA((2,))]),
        out_shape=jax.ShapeDtypeStruct((ne, r, d), BF16),
        compiler_params=_cparams(("arbitrary", "arbitrary", "arbitrary")),
        name="expert_ffn",
    )(tiles.reshape(-1), h_tokens, w_up, w_up, w_down, gate)


def _combine_kernel(start_ref, rel_ref, x_ref, mod_ref, *refs):
    y_refs, o_ref = refs[:-1], refs[-1]
    tm = x_ref.shape[0]
    win = y_refs[0].shape[1]
    rel = rel_ref[...]
    lane = lax.broadcasted_iota(jnp.int32, (tm, win), 1)
    acc = jnp.zeros(x_ref.shape, F32)
    for e, y_ref in enumerate(y_refs):
        onehot = jnp.where(rel[:, e:e + 1] == lane, 1.0, 0.0).astype(BF16)
        acc = acc + jnp.dot(onehot, y_ref[0], preferred_element_type=F32)
    o_ref[...] = x_ref[...] + mod_ref[5:6, :] * acc


def _combine(x_mid, modsel, y, rel_t, start, n_ctx):
    b, t, d = x_mid.shape
    ne, r, _ = y.shape
    tm = COMBINE_TILE
    win = tm + ROW_ALIGN_BF16
    ctx_tiles = n_ctx // tm
    y_specs = [pl.BlockSpec((pl.Element(1), pl.Element(win), pl.Element(d)),
                            lambda b, i, st, e=e: (e, st[b, e, i] * ROW_ALIGN_BF16, 0))
               for e in range(ne)]
    return pl.pallas_call(
        _combine_kernel,
        grid_spec=pltpu.PrefetchScalarGridSpec(
            num_scalar_prefetch=1,
            grid=(b, t // tm),
            in_specs=[pl.BlockSpec((None, tm, ne), lambda b, i, st: (b, i, 0)),
                      pl.BlockSpec((None, tm, d), lambda b, i, st: (b, i, 0)),
                      pl.BlockSpec((None, None, 6, d),
                                   lambda b, i, st: (b, (i >= ctx_tiles).astype(jnp.int32), 0, 0))] + y_specs,
            out_specs=pl.BlockSpec((None, tm, d), lambda b, i, st: (b, i, 0))),
        out_shape=jax.ShapeDtypeStruct((b, t, d), F32),
        compiler_params=_cparams(("arbitrary", "arbitrary")),
        name="moe_combine",
    )(start, rel_t, x_mid, modsel, *([y] * ne))


def _route_segment(aff, cap):
    gate_u, idx_u = lax.top_k(aff, cap)
    kth = gate_u[..., -1:]
    gt = aff > kth
    eq = aff == kth
    need = cap - jnp.sum(gt, axis=-1, keepdims=True, dtype=jnp.int32)
    mask = gt | (eq & (jnp.cumsum(eq.astype(jnp.int32), axis=-1) <= need))
    m32 = mask.astype(jnp.int32)
    pos = jnp.cumsum(m32, axis=-1) - m32
    idx = jnp.sort(idx_u.astype(jnp.int32), axis=-1)
    gate = jnp.take_along_axis(aff, idx, axis=-1)
    return mask, pos, idx, gate


def _moe(x_mid, h2, logits_t, modsel, w_up, w_down, layer, n_ctx):
    b, t, d = x_mid.shape
    ne = logits_t.shape[1]
    tm = COMBINE_TILE
    win = tm + ROW_ALIGN_BF16
    assert n_ctx % tm == 0 and t % tm == 0
    aff = jax.nn.softmax(logits_t, axis=1)
    cap_l = EC_CAPACITY_FACTOR * (t - n_ctx) // ne
    cap_c = EC_CAPACITY_FACTOR * n_ctx // ne
    r = b * (cap_l + cap_c)
    assert r % ROW_ALIGN_BF16 == 0 and r >= win
    mask_l, pos_l, idx_l, gate_l = _route_segment(aff[:, :, n_ctx:], cap_l)
    mask_c, pos_c, idx_c, gate_c = _route_segment(aff[:, :, :n_ctx], cap_c)
    bi = jnp.arange(b, dtype=jnp.int32)[:, None, None]

    def per_expert(a):
        return a.transpose(1, 0, 2).reshape(ne, -1)

    idx = jnp.concatenate([per_expert(idx_l + bi * t + n_ctx), per_expert(idx_c + bi * t)], axis=1)
    gate = jnp.concatenate([per_expert(gate_l), per_expert(gate_c)], axis=1)
    y = _expert_ffn(h2.reshape(b * t, d), idx, w_up, w_down, layer, gate[..., None])

    row = jnp.concatenate([pos_c + b * cap_l + bi * cap_c, pos_l + bi * cap_l], axis=2)
    mask = jnp.concatenate([mask_c, mask_l], axis=2)
    start = jnp.minimum(row[:, :, ::tm] // ROW_ALIGN_BF16 * ROW_ALIGN_BF16, r - win)
    rel = jnp.where(mask, row - jnp.repeat(start, tm, axis=2), -1)
    return _combine(x_mid, modsel, y, rel.transpose(0, 2, 1), start // ROW_ALIGN_BF16, n_ctx)


def _norm_slab_kernel(x_ref, mod_ref, g_ref, h_ref):
    h = _norm_mod(x_ref[...], g_ref[...], mod_ref[0:1, :], mod_ref[1:2, :]).astype(h_ref.dtype)
    for o in range(h_ref.shape[0]):
        h_ref[o] = h[:, o * LANES:(o + 1) * LANES]


def _norm_slabs(x, modsel, g):
    b, t, d = x.shape
    tm = ROW_TILE
    n_slab = d // LANES
    return pl.pallas_call(
        _norm_slab_kernel,
        grid=(b, t // tm),
        in_specs=[pl.BlockSpec((None, tm, d), lambda b, i: (b, i, 0)),
                  _mod_spec(d),
                  pl.BlockSpec((1, d), lambda b, i: (0, 0))],
        out_specs=pl.BlockSpec((None, n_slab, tm, LANES), lambda b, i: (b, 0, i, 0)),
        out_shape=jax.ShapeDtypeStruct((b, n_slab, t, LANES), BF16),
        compiler_params=_cparams(("arbitrary", "arbitrary")),
        name="norm_mod",
    )(x, modsel, g.reshape(1, d))


def _s5_core_kernel(u_ref, win_ref, toep_ref, wout_ref, a_ref, y_ref, zh_scr, *, m_ctx):
    dr = pl.program_id(2)
    m = u_ref.shape[0]
    rc = _largest_divisor(m, 512, 16)
    half = zh_scr.shape[1] // 2
    for r0 in range(0, m, rc):
        zh_scr[r0:r0 + rc, :] = jnp.dot(u_ref[r0:r0 + rc, :], win_ref[...], preferred_element_type=F32)
    ar = a_ref[0:1, :]
    ai = a_ref[1:2, :]
    rid = lax.broadcasted_iota(jnp.int32, (8, half), 0)

    def visit(blk, carry, reverse):
        sr, si = carry
        base = pl.multiple_of(blk * 8, 8)
        z8 = zh_scr[pl.ds(base, 8), :]
        hr = jnp.zeros((8, half), F32)
        hi = jnp.zeros((8, half), F32)
        for r in (range(7, -1, -1) if reverse else range(8)):
            hr = jnp.where(rid == r, sr, hr)
            hi = jnp.where(rid == r, si, hi)
            zr = z8[r:r + 1, 0:half]
            zi = z8[r:r + 1, half:2 * half]
            sr, si = ar * sr - ai * si + zr, ar * si + ai * sr + zi
        zh_scr[pl.ds(base, 8), 0:half] = hr
        zh_scr[pl.ds(base, 8), half:2 * half] = hi
        return sr, si

    zero = (jnp.zeros((1, half), F32), jnp.zeros((1, half), F32))
    nb, nb_ctx = m // 8, m_ctx // 8

    @pl.when(dr == 0)
    def _():
        lax.fori_loop(0, nb, lambda s, cr: visit(s, cr, False), zero)

    @pl.when(dr == 1)
    def _():
        carry = lax.fori_loop(0, nb_ctx, lambda s, cr: visit(nb_ctx - 1 - s, cr, True), zero)
        lax.fori_loop(0, nb - nb_ctx, lambda s, cr: visit(nb - 1 - s, cr, True), carry)

    for r0 in range(0, m, rc):
        y = (jnp.dot(u_ref[r0:r0 + rc, :], toep_ref[...], preferred_element_type=F32)
             + jnp.dot(zh_scr[r0:r0 + rc, :].astype(BF16), wout_ref[...], preferred_element_type=F32))
        y_ref[r0:r0 + rc, :] = y.astype(y_ref.dtype)


def _s5_weights(lam_re, lam_im, log_dt, b_re, b_im, c_re, c_im):
    sub = S5_SUB
    gs = S5_GROUP_SIZE
    gps = LANES // gs
    outs = []
    for dr in range(2):
        lre = jnp.minimum(lam_re[dr].astype(F32), -S5_MIN_DECAY)
        lim = lam_im[dr].astype(F32)
        dt = jnp.exp(log_dt[dr].astype(F32))[:, None]
        mag = jnp.exp(lre * dt)
        ang = lim * dt
        lbr, lbi = mag * jnp.cos(ang), mag * jnp.sin(ang)
        den = lre * lre + lim * lim
        f_re = ((lbr - 1) * lre + lbi * lim) / den
        f_im = (lbi * lre - (lbr - 1) * lim) / den
        bre, bim = b_re[dr].astype(F32), b_im[dr].astype(F32)
        bbr = f_re[..., None] * bre - f_im[..., None] * bim
        bbi = f_re[..., None] * bim + f_im[..., None] * bre
        cr, ci = c_re[dr].astype(F32), c_im[dr].astype(F32)
        pr, pi = [jnp.ones_like(lbr)], [jnp.zeros_like(lbr)]
        for _ in range(sub):
            pr.append(pr[-1] * lbr - pi[-1] * lbi)
            pi.append(pr[-2] * lbi + pi[-1] * lbr)
        pw_r, pw_i = jnp.stack(pr), jnp.stack(pi)
        pb_r = pw_r[..., None] * bbr - pw_i[..., None] * bbi
        pb_i = pw_r[..., None] * bbi + pw_i[..., None] * bbr
        cp_r = cr[None] * pw_r[:, :, None, :] - ci[None] * pw_i[:, :, None, :]
        cp_i = cr[None] * pw_i[:, :, None, :] + ci[None] * pw_r[:, :, None, :]
        kk = (jnp.einsum('gip,tgpj->tgij', cr, pb_r[:sub], precision=HI)
              - jnp.einsum('gip,tgpj->tgij', ci, pb_i[:sub], precision=HI))
        s_idx = jnp.arange(sub)
        lag = (s_idx[None, :] - s_idx[:, None]) if dr == 0 else (s_idx[:, None] - s_idx[None, :])
        kt = kk[jnp.clip(lag, 0, sub - 1)]
        kt = jnp.where((lag >= 0)[:, :, None, None, None], kt, 0.0)
        g_n = kt.shape[2]
        n_slab = g_n // gps
        eye = jnp.eye(gps, dtype=F32)
        toep = jnp.einsum('stogij,gh->osgjthi', kt.reshape(sub, sub, n_slab, gps, gs, gs), eye)
        toep = toep.reshape(n_slab, sub * LANES, sub * LANES)
        e_in = (sub - 1 - s_idx) if dr == 0 else s_idx

        def in_map(pb):
            w = jnp.einsum('sogpj,gh->osgjhp', pb[e_in].reshape(sub, n_slab, gps, S5_STATE, gs), eye)
            return w.reshape(n_slab, sub * LANES, gps * S5_STATE)

        win = jnp.concatenate([in_map(pb_r), in_map(pb_i)], axis=2)
        e_out = (s_idx + 1) if dr == 0 else (sub - s_idx)

        def out_map(cp):
            w = jnp.einsum('togip,gh->ogpthi', cp[e_out].reshape(sub, n_slab, gps, gs, S5_STATE), eye)
            return w.reshape(n_slab, gps * S5_STATE, sub * LANES)

        wout = jnp.concatenate([out_map(cp_r), -out_map(cp_i)], axis=1)
        a_sub = jnp.stack([pw_r[sub].reshape(n_slab, gps * S5_STATE),
                           pw_i[sub].reshape(n_slab, gps * S5_STATE)], axis=1)
        outs.append((win, toep, wout, a_sub))

    stack = lambda k, dt: jnp.stack([o[k] for o in outs], axis=1).astype(dt)
    return stack(0, BF16), stack(1, BF16), stack(2, BF16), stack(3, F32)


def _s5_core(hs, weights, n_ctx):
    b, n_slab, t, _ = hs.shape
    sub = S5_SUB
    win, toep, wout, a_sub = weights
    m = t // sub
    m_ctx = n_ctx // sub
    assert m % 8 == 0 and m_ctx % 8 == 0
    cw = sub * LANES
    sw = win.shape[-1]
    u = hs.reshape(b, n_slab, m, cw)
    kern = functools.partial(_s5_core_kernel, m_ctx=m_ctx)
    y = pl.pallas_call(
        kern,
        grid=(n_slab, b, 2),
        in_specs=[pl.BlockSpec((None, None, m, cw), lambda o, b, d: (b, o, 0, 0)),
                  pl.BlockSpec((None, None, cw, sw), lambda o, b, d: (o, d, 0, 0)),
                  pl.BlockSpec((None, None, cw, cw), lambda o, b, d: (o, d, 0, 0)),
                  pl.BlockSpec((None, None, sw, cw), lambda o, b, d: (o, d, 0, 0)),
                  pl.BlockSpec((None, None, 2, sw // 2), lambda o, b, d: (o, d, 0, 0))],
        out_specs=pl.BlockSpec((None, None, None, m, cw), lambda o, b, d: (d, b, o, 0, 0)),
        out_shape=jax.ShapeDtypeStruct((2, b, n_slab, m, cw), BF16),
        scratch_shapes=[pltpu.VMEM((m, sw), F32)],
        compiler_params=_cparams(("arbitrary", "arbitrary", "arbitrary")),
        name="s5_core",
    )(u, win, toep, wout, a_sub)
    return y.reshape(2, b, n_slab, t, LANES)


def _ret_in_kernel(x_ref, mod_ref, g_ref, w_ref, cos_ref, sin_ref, z_ref, *, n_rope_cols):
    h = _norm_mod(x_ref[...], g_ref[...], mod_ref[0:1, :], mod_ref[1:2, :]).astype(BF16)
    cos, sin = cos_ref[...], sin_ref[...]
    half = cos.shape[1]
    tn = 4 * half
    for n0 in range(0, w_ref.shape[1], tn):
        z = jnp.dot(h, w_ref[:, n0:n0 + tn], preferred_element_type=F32)
        if n0 < n_rope_cols:
            for c0 in range(0, tn, 2 * half):
                x1 = z[:, c0:c0 + half]
                x2 = z[:, c0 + half:c0 + 2 * half]
                z_ref[:, n0 + c0:n0 + c0 + half] = (x1 * cos - x2 * sin).astype(z_ref.dtype)
                z_ref[:, n0 + c0 + half:n0 + c0 + 2 * half] = (x2 * cos + x1 * sin).astype(z_ref.dtype)
        else:
            z_ref[:, n0:n0 + tn] = z.astype(z_ref.dtype)


def _ret_in(x, modsel, g, w, cos, sin, n_rope_cols):
    b, t, d = x.shape
    n_out = w.shape[1]
    tm = ROW_TILE
    half = cos.shape[1]
    assert n_out % (4 * half) == 0 and n_rope_cols % (4 * half) == 0
    kern = functools.partial(_ret_in_kernel, n_rope_cols=n_rope_cols)
    return pl.pallas_call(
        kern,
        grid=(b, t // tm),
        in_specs=[pl.BlockSpec((None, tm, d), lambda b, i: (b, i, 0)),
                  _mod_spec(d),
                  pl.BlockSpec((1, d), lambda b, i: (0, 0)),
                  pl.BlockSpec((d, n_out), lambda b, i: (0, 0), pipeline_mode=pl.Buffered(1)),
                  pl.BlockSpec((tm, half), lambda b, i: (i, 0)),
                  pl.BlockSpec((tm, half), lambda b, i: (i, 0))],
        out_specs=pl.BlockSpec((None, tm, n_out), lambda b, i: (b, i, 0)),
        out_shape=jax.ShapeDtypeStruct((b, t, n_out), BF16),
        compiler_params=_cparams(("arbitrary", "arbitrary")),
        name="ret_in",
    )(x, modsel, g.reshape(1, d), w, cos, sin)


def _ret_core_kernel(lg_ref, q_ref, k_ref, v_ref, *rest, rev, qk_scale):
    if rev:
        of_ref, g_ref, o_ref, s_scr = rest
    else:
        o_ref, s_scr = rest
    hh = pl.program_id(1)
    c = pl.program_id(2)
    tc = q_ref.shape[0]

    @pl.when(c == 0)
    def _():
        s_scr[...] = jnp.zeros(s_scr.shape, F32)

    lg = jnp.full((1, 1), lg_ref[hh], F32)
    row = lax.broadcasted_iota(jnp.int32, (tc, tc), 0)
    col = lax.broadcasted_iota(jnp.int32, (tc, tc), 1)
    diff = (col - row) if rev else (row - col)
    inner = jnp.where(diff >= 0, jnp.exp(lg * jnp.maximum(diff, 0).astype(F32)), 0.0) * qk_scale
    pos = lax.broadcasted_iota(jnp.int32, (tc, 1), 0).astype(F32)
    if rev:
        q_dec = jnp.exp(lg * (tc - pos))
        k_dec = jnp.exp(lg * pos) * qk_scale
    else:
        q_dec = jnp.exp(lg * (pos + 1.0))
        k_dec = jnp.exp(lg * (tc - 1.0 - pos)) * qk_scale
    q = q_ref[...]
    k = k_ref[...]
    v = v_ref[...]
    s = lax.dot_general(q, k, (((1,), (1,)), ((), ())), preferred_element_type=F32) * inner
    state = s_scr[...]
    o = (jnp.dot(s.astype(BF16), v, preferred_element_type=F32)
         + jnp.dot((q.astype(F32) * q_dec).astype(BF16), state.astype(BF16), preferred_element_type=F32))
    kd_t = (k.astype(F32) * k_dec).T.astype(BF16)
    s_scr[...] = state * jnp.exp(lg * tc) + jnp.dot(kd_t, v, preferred_element_type=F32)
    if rev:
        tot = of_ref[...] + o
        nrm = tot * lax.rsqrt(jnp.mean(tot * tot, axis=-1, keepdims=True) + NORM_EPS)
        gg = g_ref[...].astype(F32)
        o_ref[...] = (gg * jax.nn.sigmoid(gg) * nrm).astype(o_ref.dtype)
    else:
        o_ref[...] = o


def _ret_core(z, log_gamma, o_fwd, rev):
    b, t, _ = z.shape
    nh = RET_HEADS
    dqk = z.shape[2] // (6 * nh)
    dv = 2 * dqk
    tc = ROW_TILE
    nc = t // tc

    if rev:
        def cmap(c):
            return jnp.where(c == 0, 0, nc - c)
    else:
        def cmap(c):
            return c

    kern = functools.partial(_ret_core_kernel, rev=rev, qk_scale=dqk ** -0.5)
    in_specs = [pl.BlockSpec((None, tc, dqk), lambda b, h, c, lg: (b, cmap(c), h)),
                pl.BlockSpec((None, tc, dqk), lambda b, h, c, lg: (b, cmap(c), nh + h)),
                pl.BlockSpec((None, tc, dv), lambda b, h, c, lg: (b, cmap(c), nh + h))]
    args = [z, z, z]
    if rev:
        in_specs += [pl.BlockSpec((None, tc, dv), lambda b, h, c, lg: (b, cmap(c), h)),
                     pl.BlockSpec((None, tc, dv), lambda b, h, c, lg: (b, cmap(c), 2 * nh + h))]
        args += [o_fwd, z]
    out_dtype = BF16 if rev else F32
    return pl.pallas_call(
        kern,
        grid_spec=pltpu.PrefetchScalarGridSpec(
            num_scalar_prefetch=1,
            grid=(b, nh, nc),
            in_specs=in_specs,
            out_specs=pl.BlockSpec((None, tc, dv), lambda b, h, c, lg: (b, cmap(c), h)),
            scratch_shapes=[pltpu.VMEM((dqk, dv), F32)]),
        out_shape=jax.ShapeDtypeStruct((b, t, nh * dv), out_dtype),
        compiler_params=_cparams(("arbitrary", "arbitrary", "arbitrary")),
        name="ret_core_bwd" if rev else "ret_core_fwd",
    )(log_gamma, *args)


def _final_kernel(x_ref, g_ref, o_ref):
    x = x_ref[...]
    o_ref[...] = x * lax.rsqrt(jnp.mean(x * x, axis=-1, keepdims=True) + NORM_EPS) * g_ref[...]


def _final_norm(x, g, n_ctx):
    b, t, d = x.shape
    tm = ROW_TILE
    skip = n_ctx // tm
    return pl.pallas_call(
        _final_kernel,
        grid=(b, (t - n_ctx) // tm),
        in_specs=[pl.BlockSpec((None, tm, d), lambda b, i: (b, i + skip, 0)),
                  pl.BlockSpec((1, d), lambda b, i: (0, 0))],
        out_specs=pl.BlockSpec((None, tm, d), lambda b, i: (b, i, 0)),
        out_shape=jax.ShapeDtypeStruct((b, t - n_ctx, d), F32),
        compiler_params=_cparams(("arbitrary", "arbitrary")),
        name="final_norm",
    )(x, g.reshape(1, d))


def _rope_tables(n_ctx, n_lat, head_dim):
    rows = n_lat // GRID_W
    row = jnp.repeat(jnp.arange(rows, dtype=F32), GRID_W)
    col = jnp.tile(jnp.arange(GRID_W, dtype=F32), rows)
    n_freq = head_dim // 4
    inv_freq = ROPE_THETA ** (-jnp.arange(n_freq, dtype=F32) / n_freq)
    ang = jnp.concatenate([row[:, None] * inv_freq, col[:, None] * inv_freq], axis=-1)
    cos = jnp.concatenate([jnp.ones((n_ctx, head_dim // 2), F32), jnp.cos(ang)], axis=0)
    sin = jnp.concatenate([jnp.zeros((n_ctx, head_dim // 2), F32), jnp.sin(ang)], axis=0)
    return cos, sin


def kernel(x, c, ctx, c_ctx, mod_w, mod_b, norm_g, attn_w_qkv, attn_w_o, attn_q_gain, attn_k_gain,
           s5_lambda_re, s5_lambda_im, s5_log_dt, s5_b_re, s5_b_im, s5_c_re, s5_c_im, s5_d, s5_w_glu, s5_b_glu,
           ret_w_qkvg, ret_w_o, ret_log_decay, moe_w_router, moe_w_up, moe_w_down, final_g):
    bsz, n_lat, d = x.shape
    n_ctx = ctx.shape[1]
    depth = mod_w.shape[0]
    assert n_ctx == ROW_TILE and n_lat % ROW_TILE == 0 and bsz <= 7

    xs = jnp.concatenate([ctx, x], axis=1).astype(F32)
    cvec = jnp.zeros((8, d), F32).at[:bsz].set(c).at[bsz].set(c_ctx)
    mods = _modulation(cvec, mod_w, mod_b)

    cos_a, sin_a = _rope_tables(n_ctx, n_lat, ATTN_HEAD_DIM)
    cos2 = jnp.concatenate([cos_a, cos_a], axis=1)
    sin2 = jnp.concatenate([-sin_a, sin_a], axis=1)
    cos_r, sin_r = _rope_tables(n_ctx, n_lat, d // RET_HEADS)

    tm = ROW_TILE

    def row_spec(width):
        return pl.BlockSpec((None, tm, width), lambda b, i: (b, i, 0))

    for i in range(depth):
        kind, j = i % N_MIXERS, i // N_MIXERS
        m6 = mods[i].reshape(8, 6, d)
        modsel = jnp.stack([jnp.broadcast_to(m6[bsz], (bsz, 6, d)), m6[:bsz]], axis=1)
        wr_t = moe_w_router[i].T.astype(F32)
        if kind == 0:
            q, k, v = _attn_in(xs, modsel, norm_g[i, 0], attn_w_qkv[j].astype(BF16),
                               attn_q_gain[j], attn_k_gain[j], cos2, sin2)
            o = _flash(q, k, v, n_ctx)
            x_mid, h2, lg = _post_call(_post_plain_kernel, "attn_out", [o], [row_spec(o.shape[-1])],
                                       [attn_w_o[j].astype(BF16)], xs, modsel, norm_g[i, 1], wr_t)
        elif kind == 1:
            hs = _norm_slabs(xs, modsel, norm_g[i, 0])
            w5 = _s5_weights(s5_lambda_re[j], s5_lambda_im[j], s5_log_dt[j], s5_b_re[j], s5_b_im[j],
                             s5_c_re[j], s5_c_im[j])
            y = _s5_core(hs, w5, n_ctx)
            n_slab = d // LANES
            slab_specs = [pl.BlockSpec((2, None, n_slab, tm, LANES), lambda b, i: (0, b, 0, i, 0)),
                          pl.BlockSpec((None, n_slab, tm, LANES), lambda b, i: (b, 0, i, 0))]
            x_mid, h2, lg = _post_call(_post_glu_kernel, "s5_out", [y, hs], slab_specs,
                                       [s5_d[j].reshape(1, d).astype(F32), s5_w_glu[j].astype(BF16),
                                        s5_b_glu[j].reshape(1, -1).astype(F32)],
                                       xs, modsel, norm_g[i, 1], wr_t)
        else:
            z = _ret_in(xs, modsel, norm_g[i, 0], ret_w_qkvg[j].astype(BF16), cos_r, sin_r, 2 * d)
            o_f = None
            for dr in range(2):
                log_gamma = -jnp.abs(ret_log_decay[j, dr].astype(F32))
                o_f = _ret_core(z, log_gamma, o_f, rev=(dr == 1))
            x_mid, h2, lg = _post_call(_post_plain_kernel, "ret_out", [o_f], [row_spec(o_f.shape[-1])],
                                       [ret_w_o[j].astype(BF16)], xs, modsel, norm_g[i, 1], wr_t)
        xs = _moe(x_mid, h2, lg, modsel, moe_w_up, moe_w_down, i, n_ctx)

    return _final_norm(xs, final_g, n_ctx).astype(x.dtype)
```

```python
import functools
import math

import jax
import jax.numpy as jnp
from jax import lax
from jax.experimental import pallas as pl
from jax.experimental.pallas import tpu as pltpu

F32 = jnp.float32
BF16 = jnp.bfloat16

GRID_W = 64
N_MIXERS = 3
NORM_EPS = 1e-6
ROPE_THETA = 10000.0
ATTN_HEAD_DIM = 128
ATTN_GROUP = 4
S5_GROUP_SIZE = 16
S5_STATE = 64
S5_MIN_DECAY = 1e-4
RET_HEADS = 4
N_EXPERTS = 16
EC_CAPACITY_FACTOR = 2

LANES = 128
ROW_ALIGN_BF16 = 16
ROW_TILE = 256
RET_HEADS_PER_STEP = 2
POST_PARTS = 2
COMBINE_TILE = 128
S5_SUB = 8
VMEM_LIMIT = 48 * 1024 * 1024
HI = lax.Precision.HIGHEST


def _cparams(sem):
    return pltpu.CompilerParams(dimension_semantics=sem, vmem_limit_bytes=VMEM_LIMIT)


def _norm_mod(x, g, shift, scale):
    ms = jnp.mean(x * x, axis=-1, keepdims=True)
    return (x * lax.rsqrt(ms + NORM_EPS) * g) * (1.0 + scale) + shift


def _largest_divisor(n, cap, mult):
    best = None
    for t in range(mult, min(n, cap) + 1, mult):
        if n % t == 0:
            best = t
    assert best is not None, (n, cap, mult)
    return best


def _mod_kernel(c_ref, w_ref, b_ref, o_ref):
    c = c_ref[...]
    s = c * jax.nn.sigmoid(c)
    o_ref[...] = jnp.dot(s, w_ref[...], precision=HI, preferred_element_type=F32) + b_ref[...]


def _modulation(cvec, mod_w, mod_b):
    depth, d, n = mod_w.shape
    tn = _largest_divisor(n, 1536, 128)
    return pl.pallas_call(
        _mod_kernel,
        grid=(depth, n // tn),
        in_specs=[pl.BlockSpec((8, d), lambda l, j: (0, 0)),
                  pl.BlockSpec((None, d, tn), lambda l, j: (l, 0, j)),
                  pl.BlockSpec((None, 1, tn), lambda l, j: (l, 0, j))],
        out_specs=pl.BlockSpec((None, 8, tn), lambda l, j: (l, 0, j)),
        out_shape=jax.ShapeDtypeStruct((depth, 8, n), F32),
        compiler_params=_cparams(("arbitrary", "arbitrary")),
        name="modulation",
    )(cvec, mod_w, mod_b.reshape(depth, 1, n))


def _mod_spec(d):
    return pl.BlockSpec((None, None, 6, d), lambda b, i, *_: (b, jnp.minimum(i, 1), 0, 0))


def _attn_in_kernel(x_ref, mod_ref, g_ref, w_ref, qg_ref, kg_ref, cos_ref, sin_ref,
                    q_ref, k_ref, v_ref, *, n_q, n_kv):
    hd = ATTN_HEAD_DIM
    h = _norm_mod(x_ref[...], g_ref[...], mod_ref[0:1, :], mod_ref[1:2, :]).astype(BF16)
    qkv = jnp.dot(h, w_ref[...], preferred_element_type=F32)
    cos, sin = cos_ref[...], sin_ref[...]

    def norm_rope(t, gain):
        t = t * lax.rsqrt(jnp.mean(t * t, axis=-1, keepdims=True) + NORM_EPS) * gain
        return t * cos + pltpu.roll(t, hd // 2, axis=1) * sin

    scale = hd ** -0.5 * math.log2(math.e)
    for j in range(n_q):
        q_ref[j] = (norm_rope(qkv[:, j * hd:(j + 1) * hd], qg_ref[...]) * scale).astype(BF16)
    lane = lax.broadcasted_iota(jnp.int32, (qkv.shape[0], hd), 1)
    ones_col = jnp.where(lane == 0, 1.0, 0.0).astype(BF16)
    for j in range(n_kv):
        c0 = (n_q + j) * hd
        k_ref[j] = norm_rope(qkv[:, c0:c0 + hd], kg_ref[...]).astype(BF16)
        c1 = (n_q + n_kv + j) * hd
        v_ref[j, :, 0:hd] = qkv[:, c1:c1 + hd].astype(BF16)
        v_ref[j, :, hd:2 * hd] = ones_col


def _attn_in(x, modsel, g, w_qkv, q_gain, k_gain, cos2, sin2):
    b, t, d = x.shape
    hd = ATTN_HEAD_DIM
    n_tot = w_qkv.shape[1] // hd
    n_q = d // hd
    n_kv = (n_tot - n_q) // 2
    tm = ROW_TILE
    kern = functools.partial(_attn_in_kernel, n_q=n_q, n_kv=n_kv)
    return pl.pallas_call(
        kern,
        grid=(b, t // tm),
        in_specs=[pl.BlockSpec((None, tm, d), lambda b, i: (b, i, 0)),
                  _mod_spec(d),
                  pl.BlockSpec((1, d), lambda b, i: (0, 0)),
                  pl.BlockSpec(w_qkv.shape, lambda b, i: (0, 0)),
                  pl.BlockSpec((1, hd), lambda b, i: (0, 0)),
                  pl.BlockSpec((1, hd), lambda b, i: (0, 0)),
                  pl.BlockSpec((tm, hd), lambda b, i: (i, 0)),
                  pl.BlockSpec((tm, hd), lambda b, i: (i, 0))],
        out_specs=[pl.BlockSpec((None, n_q, tm, hd), lambda b, i: (b, 0, i, 0)),
                   pl.BlockSpec((None, n_kv, tm, hd), lambda b, i: (b, 0, i, 0)),
                   pl.BlockSpec((None, n_kv, tm, 2 * hd), lambda b, i: (b, 0, i, 0))],
        out_shape=[jax.ShapeDtypeStruct((b, n_q, t, hd), BF16),
                   jax.ShapeDtypeStruct((b, n_kv, t, hd), BF16),
                   jax.ShapeDtypeStruct((b, n_kv, t, 2 * hd), BF16)],
        compiler_params=_cparams(("arbitrary", "arbitrary")),
        name="attn_in",
    )(x, modsel, g.reshape(1, d), w_qkv, q_gain.reshape(1, hd), k_gain.reshape(1, hd), cos2, sin2)


def _flash_kernel(q_ref, k_ref, v_ref, o_ref, s0_scr, s1_scr, m_scr, acc_scr, *, tq, n_ctx):
    hd = ATTN_HEAD_DIM
    qi = pl.program_id(2)
    j = pl.program_id(3)
    nk = pl.num_programs(3) - 1
    m_init = -1e30

    @pl.when(j == 0)
    def _():
        m_scr[...] = jnp.full(m_scr.shape, m_init, F32)
        acc_scr[...] = jnp.zeros(acc_scr.shape, F32)
        s1_scr[...] = jnp.full(s1_scr.shape, -jnp.inf, F32)

    def softmax_update(s, v):
        m_prev = m_scr[...]
        m_new = jnp.maximum(m_prev, jnp.max(s, axis=1, keepdims=True))
        alpha = jnp.exp2(m_prev - m_new)
        p = jnp.exp2((s - m_new).astype(BF16))
        acc_scr[...] = alpha * acc_scr[...] + jnp.dot(p, v, preferred_element_type=F32)
        m_scr[...] = m_new

    def scores(k):
        q = q_ref[...].reshape(ATTN_GROUP * tq, hd)
        return lax.dot_general(q, k, (((1,), (1,)), ((), ())), preferred_element_type=F32)

    def pipelined(s_new, s_old):
        s_new[...] = scores(k_ref[...])
        softmax_update(s_old[...], v_ref[...])

    latent = qi > 0
    odd = jnp.bitwise_and(j, 1)

    @pl.when(jnp.logical_and(latent, odd == 0))
    def _():
        pipelined(s0_scr, s1_scr)

    @pl.when(jnp.logical_and(latent, odd == 1))
    def _():
        pipelined(s1_scr, s0_scr)

    @pl.when(jnp.logical_and(qi == 0, j == 0))
    def _():
        softmax_update(scores(k_ref[0:n_ctx, :]), v_ref[0:n_ctx, :])

    @pl.when(j == nk)
    def _():
        o = acc_scr[:, 0:hd] / acc_scr[:, hd:hd + 1]
        for g in range(ATTN_GROUP):
            o_ref[:, g * hd:(g + 1) * hd] = o[g * tq:(g + 1) * tq, :].astype(o_ref.dtype)


def _flash(q, k, v, n_ctx):
    b, n_q, t, hd = q.shape
    n_kv = k.shape[1]
    tq = ROW_TILE
    assert n_ctx == tq and n_q == n_kv * ATTN_GROUP
    tk = _largest_divisor(t, 1280, 256)
    nk = t // tk
    gw = ATTN_GROUP * hd
    rows = ATTN_GROUP * tq
    kern = functools.partial(_flash_kernel, tq=tq, n_ctx=n_ctx)

    def k_map(b, h, i, j):
        return (b, h, jnp.where(i == 0, 0, jnp.minimum(j, nk - 1)), 0)

    def v_map(b, h, i, j):
        return (b, h, jnp.where(i == 0, 0, jnp.maximum(j - 1, 0)), 0)

    return pl.pallas_call(
        kern,
        grid=(b, n_kv, t // tq, nk + 1),
        in_specs=[pl.BlockSpec((None, ATTN_GROUP, tq, hd), lambda b, h, i, j: (b, h, i, 0)),
                  pl.BlockSpec((None, None, tk, hd), k_map),
                  pl.BlockSpec((None, None, tk, 2 * hd), v_map)],
        out_specs=pl.BlockSpec((None, tq, gw), lambda b, h, i, j: (b, i, h)),
        out_shape=jax.ShapeDtypeStruct((b, t, n_q * hd), BF16),
        scratch_shapes=[pltpu.VMEM((rows, tk), F32),
                        pltpu.VMEM((rows, tk), F32),
                        pltpu.VMEM((rows, 1), F32),
                        pltpu.VMEM((rows, 2 * hd), F32)],
        compiler_params=_cparams(("arbitrary", "arbitrary", "arbitrary", "arbitrary")),
        name="flash_attn",
    )(q, k, v)


def _post_tail(rows, x_new, mod_ref, g2_ref, wr_ref, x_out, h_out, lg_out):
    x_out[rows, :] = x_new
    h2 = _norm_mod(x_new, g2_ref[...], mod_ref[3:4, :], mod_ref[4:5, :])
    h_out[rows, :] = h2
    lg_out[:, rows] = lax.dot_general(wr_ref[...], h2, (((1,), (1,)), ((), ())),
                                      precision=HI, preferred_element_type=F32)


def _row_parts(tm):
    part = tm // POST_PARTS
    return [slice(p * part, (p + 1) * part) for p in range(POST_PARTS)]


def _post_plain_kernel(o_ref, w_ref, x_ref, mod_ref, g2_ref, wr_ref, x_out, h_out, lg_out):
    for rows in _row_parts(x_ref.shape[0]):
        y = jnp.dot(o_ref[rows, :], w_ref[...], preferred_element_type=F32)
        _post_tail(rows, x_ref[rows, :] + mod_ref[2:3, :] * y, mod_ref, g2_ref, wr_ref, x_out, h_out, lg_out)


def _post_glu_kernel(y_ref, hs_ref, dsk_ref, w_ref, b_ref, x_ref, mod_ref, g2_ref, wr_ref,
                     x_out, h_out, lg_out):
    d = x_ref.shape[-1]
    for rows in _row_parts(x_ref.shape[0]):
        slabs = [hs_ref[o, rows, :].astype(F32) for o in range(hs_ref.shape[0])]
        ys = [y_ref[0, o, rows, :].astype(F32) + y_ref[1, o, rows, :].astype(F32) for o in range(hs_ref.shape[0])]
        yt = dsk_ref[...] * jnp.concatenate(slabs, axis=1) + jnp.concatenate(ys, axis=1)
        z = jnp.dot(jax.nn.gelu(yt).astype(BF16), w_ref[...], preferred_element_type=F32) + b_ref[...]
        y = z[:, :d] * jax.nn.sigmoid(z[:, d:])
        _post_tail(rows, x_ref[rows, :] + mod_ref[2:3, :] * y, mod_ref, g2_ref, wr_ref, x_out, h_out, lg_out)


def _post_call(kern, name, row_inputs, row_specs, const_inputs, x, modsel, g2, w_router_t):
    b, t, d = x.shape
    tm = ROW_TILE
    ne = w_router_t.shape[0]
    const_specs = [pl.BlockSpec(a.shape, lambda b, i: (0, 0)) for a in const_inputs]
    return pl.pallas_call(
        kern,
        grid=(b, t // tm),
        in_specs=row_specs + const_specs + [
            pl.BlockSpec((None, tm, d), lambda b, i: (b, i, 0)),
            _mod_spec(d),
            pl.BlockSpec((1, d), lambda b, i: (0, 0)),
            pl.BlockSpec((ne, d), lambda b, i: (0, 0))],
        out_specs=[pl.BlockSpec((None, tm, d), lambda b, i: (b, i, 0)),
                   pl.BlockSpec((None, tm, d), lambda b, i: (b, i, 0)),
                   pl.BlockSpec((None, ne, tm), lambda b, i: (b, 0, i))],
        out_shape=[jax.ShapeDtypeStruct((b, t, d), F32),
                   jax.ShapeDtypeStruct((b, t, d), F32),
                   jax.ShapeDtypeStruct((b, ne, t), F32)],
        compiler_params=_cparams(("arbitrary", "arbitrary")),
        name=name,
    )(*row_inputs, *const_inputs, x, modsel, g2.reshape(1, d), w_router_t)


def _ffn_kernel(idx_ref, h_hbm, wa_ref, wu_ref, wd_ref, gate_ref, o_ref,
                xf_scr, xb_scr, acc_scr, sem, *, rows_pad, per_step):
    e, m, f = pl.program_id(0), pl.program_id(1), pl.program_id(2)
    nm, nf = pl.num_programs(1), pl.num_programs(2)
    tm = xb_scr.shape[0]
    lin = e * nm + m
    slot = jnp.bitwise_and(lin, 1)
    last_tile = pl.num_programs(0) * nm - 1

    def row_copy(token, s, r):
        return pltpu.make_async_copy(h_hbm.at[pl.ds(token, 1), :], xf_scr.at[s, pl.ds(r, 1), :], sem.at[s])

    def wait_rows(s):
        pltpu.make_async_copy(h_hbm.at[pl.ds(0, rows_pad), :], xf_scr.at[s], sem.at[s]).wait()

    @pl.when(jnp.logical_and(lin == 0, f == 0))
    def _():
        def body(r, c):
            row_copy(idx_ref[r], 0, r).start()
            return c
        lax.fori_loop(0, rows_pad, body, 0)

    @pl.when(f == 0)
    def _():
        wait_rows(slot)
        xb_scr[...] = xf_scr[slot, 0:tm, :].astype(BF16)
        acc_scr[...] = jnp.zeros(acc_scr.shape, F32)

    base = (lin + 1) * rows_pad + f * per_step
    for k in range(per_step):
        row_copy(idx_ref[base + k], 1 - slot, f * per_step + k).start()

    x = xb_scr[...]
    a = jnp.dot(x, wa_ref[...].astype(BF16), preferred_element_type=F32)
    u = jnp.dot(x, wu_ref[...].astype(BF16), preferred_element_type=F32)
    hmid = (a * jax.nn.sigmoid(a) * u).astype(BF16)
    acc_scr[...] += jnp.dot(hmid, wd_ref[...].astype(BF16), preferred_element_type=F32)

    @pl.when(f == nf - 1)
    def _():
        o_ref[...] = (acc_scr[...] * gate_ref[...]).astype(o_ref.dtype)

    @pl.when(jnp.logical_and(lin == last_tile, f == nf - 1))
    def _():
        wait_rows(1 - slot)


def _expert_ffn(h_tokens, idx, w_up, w_down, layer, gate):
    ne, r = idx.shape
    d = h_tokens.shape[1]
    ff = w_down.shape[2]
    tm = _largest_divisor(r, 1040, 16)
    tf = _largest_divisor(ff, 256, 128)
    nm, nf = r // tm, ff // tf
    per_step = pl.cdiv(pl.cdiv(tm, nf), 8) * 8
    rows_pad = per_step * nf
    tiles = idx.reshape(ne * nm, tm)
    tiles = jnp.pad(tiles, ((0, 1), (0, rows_pad - tm)))
    kern = functools.partial(_ffn_kernel, rows_pad=rows_pad, per_step=per_step)
    return pl.pallas_call(
        kern,
        grid_spec=pltpu.PrefetchScalarGridSpec(
            num_scalar_prefetch=1,
            grid=(ne, nm, nf),
            in_specs=[pl.BlockSpec(memory_space=pl.ANY),
                      pl.BlockSpec((None, None, d, tf), lambda e, m, f, ix: (layer, e, 0, f)),
                      pl.BlockSpec((None, None, d, tf), lambda e, m, f, ix: (layer, e, 0, nf + f)),
                      pl.BlockSpec((None, None, tf, d), lambda e, m, f, ix: (layer, e, f, 0)),
                      pl.BlockSpec((None, tm, 1), lambda e, m, f, ix: (e, m, 0))],
            out_specs=pl.BlockSpec((None, tm, d), lambda e, m, f, ix: (e, m, 0)),
            scratch_shapes=[pltpu.VMEM((2, rows_pad, d), F32),
                            pltpu.VMEM((tm, d), BF16),
                            pltpu.VMEM((tm, d), F32),
                            pltpu.SemaphoreType.DMA((2,))]),
        out_shape=jax.ShapeDtypeStruct((ne, r, d), BF16),
        compiler_params=_cparams(("arbitrary", "arbitrary", "arbitrary")),
        name="expert_ffn",
    )(tiles.reshape(-1), h_tokens, w_up, w_up, w_down, gate)


def _combine_kernel(start_ref, rel_ref, x_ref, mod_ref, *refs):
    y_refs, o_ref = refs[:-1], refs[-1]
    tm = x_ref.shape[0]
    win = y_refs[0].shape[1]
    rel = rel_ref[...]
    lane = lax.broadcasted_iota(jnp.int32, (tm, win), 1)
    acc = jnp.zeros(x_ref.shape, F32)
    for e, y_ref in enumerate(y_refs):
        onehot = jnp.where(rel[:, e:e + 1] == lane, 1.0, 0.0).astype(BF16)
        acc = acc + jnp.dot(onehot, y_ref[0], preferred_element_type=F32)
    o_ref[...] = x_ref[...] + mod_ref[5:6, :] * acc


def _combine(x_mid, modsel, y, rel_t, start, n_ctx):
    b, t, d = x_mid.shape
    ne, r, _ = y.shape
    tm = COMBINE_TILE
    win = tm + ROW_ALIGN_BF16
    ctx_tiles = n_ctx // tm
    y_specs = [pl.BlockSpec((pl.Element(1), pl.Element(win), pl.Element(d)),
                            lambda b, i, st, e=e: (e, st[b, e, i] * ROW_ALIGN_BF16, 0))
               for e in range(ne)]
    return pl.pallas_call(
        _combine_kernel,
        grid_spec=pltpu.PrefetchScalarGridSpec(
            num_scalar_prefetch=1,
            grid=(b, t // tm),
            in_specs=[pl.BlockSpec((None, tm, ne), lambda b, i, st: (b, i, 0)),
                      pl.BlockSpec((None, tm, d), lambda b, i, st: (b, i, 0)),
                      pl.BlockSpec((None, None, 6, d),
                                   lambda b, i, st: (b, (i >= ctx_tiles).astype(jnp.int32), 0, 0))] + y_specs,
            out_specs=pl.BlockSpec((None, tm, d), lambda b, i, st: (b, i, 0))),
        out_shape=jax.ShapeDtypeStruct((b, t, d), F32),
        compiler_params=_cparams(("arbitrary", "arbitrary")),
        name="moe_combine",
    )(start, rel_t, x_mid, modsel, *([y] * ne))


def _route_segment(aff, cap):
    gate_u, idx_u = lax.top_k(aff, cap)
    kth = gate_u[..., -1:]
    gt = aff > kth
    eq = aff == kth
    need = cap - jnp.sum(gt, axis=-1, keepdims=True, dtype=jnp.int32)
    mask = gt | (eq & (jnp.cumsum(eq.astype(jnp.int32), axis=-1) <= need))
    m32 = mask.astype(jnp.int32)
    pos = jnp.cumsum(m32, axis=-1) - m32
    idx = jnp.sort(idx_u.astype(jnp.int32), axis=-1)
    gate = jnp.take_along_axis(aff, idx, axis=-1)
    return mask, pos, idx, gate


def _moe(x_mid, h2, logits_t, modsel, w_up, w_down, layer, n_ctx):
    b, t, d = x_mid.shape
    ne = logits_t.shape[1]
    tm = COMBINE_TILE
    win = tm + ROW_ALIGN_BF16
    assert n_ctx % tm == 0 and t % tm == 0
    aff = jax.nn.softmax(logits_t, axis=1)
    cap_l = EC_CAPACITY_FACTOR * (t - n_ctx) // ne
    cap_c = EC_CAPACITY_FACTOR * n_ctx // ne
    r = b * (cap_l + cap_c)
    assert r % ROW_ALIGN_BF16 == 0 and r >= win
    mask_l, pos_l, idx_l, gate_l = _route_segment(aff[:, :, n_ctx:], cap_l)
    mask_c, pos_c, idx_c, gate_c = _route_segment(aff[:, :, :n_ctx], cap_c)
    bi = jnp.arange(b, dtype=jnp.int32)[:, None, None]

    def per_expert(a):
        return a.transpose(1, 0, 2).reshape(ne, -1)

    idx = jnp.concatenate([per_expert(idx_l + bi * t + n_ctx), per_expert(idx_c + bi * t)], axis=1)
    gate = jnp.concatenate([per_expert(gate_l), per_expert(gate_c)], axis=1)
    y = _expert_ffn(h2.reshape(b * t, d), idx, w_up, w_down, layer, gate[..., None])

    row = jnp.concatenate([pos_c + b * cap_l + bi * cap_c, pos_l + bi * cap_l], axis=2)
    mask = jnp.concatenate([mask_c, mask_l], axis=2)
    start = jnp.minimum(row[:, :, ::tm] // ROW_ALIGN_BF16 * ROW_ALIGN_BF16, r - win)
    rel = jnp.where(mask, row - jnp.repeat(start, tm, axis=2), -1)
    return _combine(x_mid, modsel, y, rel.transpose(0, 2, 1), start // ROW_ALIGN_BF16, n_ctx)


def _norm_slab_kernel(x_ref, mod_ref, g_ref, h_ref):
    h = _norm_mod(x_ref[...], g_ref[...], mod_ref[0:1, :], mod_ref[1:2, :]).astype(h_ref.dtype)
    for o in range(h_ref.shape[0]):
        h_ref[o] = h[:, o * LANES:(o + 1) * LANES]


def _norm_slabs(x, modsel, g):
    b, t, d = x.shape
    tm = ROW_TILE
    n_slab = d // LANES
    return pl.pallas_call(
        _norm_slab_kernel,
        grid=(b, t // tm),
        in_specs=[pl.BlockSpec((None, tm, d), lambda b, i: (b, i, 0)),
                  _mod_spec(d),
                  pl.BlockSpec((1, d), lambda b, i: (0, 0))],
        out_specs=pl.BlockSpec((None, n_slab, tm, LANES), lambda b, i: (b, 0, i, 0)),
        out_shape=jax.ShapeDtypeStruct((b, n_slab, t, LANES), BF16),
        compiler_params=_cparams(("arbitrary", "arbitrary")),
        name="norm_mod",
    )(x, modsel, g.reshape(1, d))


def _s5_core_kernel(u_ref, win_ref, toep_ref, wout_ref, a_ref, y_ref, zh_scr, *, m_ctx):
    dr = pl.program_id(2)
    m = u_ref.shape[0]
    rc = _largest_divisor(m, 512, 16)
    half = zh_scr.shape[1] // 2
    for r0 in range(0, m, rc):
        zh_scr[r0:r0 + rc, :] = jnp.dot(u_ref[r0:r0 + rc, :], win_ref[...], preferred_element_type=F32)
    ar = a_ref[0:1, :]
    ai = a_ref[1:2, :]
    rid = lax.broadcasted_iota(jnp.int32, (8, half), 0)

    def visit(blk, carry, reverse):
        sr, si = carry
        base = pl.multiple_of(blk * 8, 8)
        z8 = zh_scr[pl.ds(base, 8), :]
        hr = jnp.zeros((8, half), F32)
        hi = jnp.zeros((8, half), F32)
        for r in (range(7, -1, -1) if reverse else range(8)):
            hr = jnp.where(rid == r, sr, hr)
            hi = jnp.where(rid == r, si, hi)
            zr = z8[r:r + 1, 0:half]
            zi = z8[r:r + 1, half:2 * half]
            sr, si = ar * sr - ai * si + zr, ar * si + ai * sr + zi
        zh_scr[pl.ds(base, 8), 0:half] = hr
        zh_scr[pl.ds(base, 8), half:2 * half] = hi
        return sr, si

    zero = (jnp.zeros((1, half), F32), jnp.zeros((1, half), F32))
    nb, nb_ctx = m // 8, m_ctx // 8

    @pl.when(dr == 0)
    def _():
        lax.fori_loop(0, nb, lambda s, cr: visit(s, cr, False), zero)

    @pl.when(dr == 1)
    def _():
        carry = lax.fori_loop(0, nb_ctx, lambda s, cr: visit(nb_ctx - 1 - s, cr, True), zero)
        lax.fori_loop(0, nb - nb_ctx, lambda s, cr: visit(nb - 1 - s, cr, True), carry)

    for r0 in range(0, m, rc):
        y = (jnp.dot(u_ref[r0:r0 + rc, :], toep_ref[...], preferred_element_type=F32)
             + jnp.dot(zh_scr[r0:r0 + rc, :].astype(BF16), wout_ref[...], preferred_element_type=F32))
        y_ref[r0:r0 + rc, :] = y.astype(y_ref.dtype)


def _s5_weights(lam_re, lam_im, log_dt, b_re, b_im, c_re, c_im):
    sub = S5_SUB
    gs = S5_GROUP_SIZE
    gps = LANES // gs
    outs = []
    for dr in range(2):
        lre = jnp.minimum(lam_re[dr].astype(F32), -S5_MIN_DECAY)
        lim = lam_im[dr].astype(F32)
        dt = jnp.exp(log_dt[dr].astype(F32))[:, None]
        mag = jnp.exp(lre * dt)
        ang = lim * dt
        lbr, lbi = mag * jnp.cos(ang), mag * jnp.sin(ang)
        den = lre * lre + lim * lim
        f_re = ((lbr - 1) * lre + lbi * lim) / den
        f_im = (lbi * lre - (lbr - 1) * lim) / den
        bre, bim = b_re[dr].astype(F32), b_im[dr].astype(F32)
        bbr = f_re[..., None] * bre - f_im[..., None] * bim
        bbi = f_re[..., None] * bim + f_im[..., None] * bre
        cr, ci = c_re[dr].astype(F32), c_im[dr].astype(F32)
        pr, pi = [jnp.ones_like(lbr)], [jnp.zeros_like(lbr)]
        for _ in range(sub):
            pr.append(pr[-1] * lbr - pi[-1] * lbi)
            pi.append(pr[-2] * lbi + pi[-1] * lbr)
        pw_r, pw_i = jnp.stack(pr), jnp.stack(pi)
        pb_r = pw_r[..., None] * bbr - pw_i[..., None] * bbi
        pb_i = pw_r[..., None] * bbi + pw_i[..., None] * bbr
        cp_r = cr[None] * pw_r[:, :, None, :] - ci[None] * pw_i[:, :, None, :]
        cp_i = cr[None] * pw_i[:, :, None, :] + ci[None] * pw_r[:, :, None, :]
        kk = (jnp.einsum('gip,tgpj->tgij', cr, pb_r[:sub], precision=HI)
              - jnp.einsum('gip,tgpj->tgij', ci, pb_i[:sub], precision=HI))
        s_idx = jnp.arange(sub)
        lag = (s_idx[None, :] - s_idx[:, None]) if dr == 0 else (s_idx[:, None] - s_idx[None, :])
        kt = kk[jnp.clip(lag, 0, sub - 1)]
        kt = jnp.where((lag >= 0)[:, :, None, None, None], kt, 0.0)
        g_n = kt.shape[2]
        n_slab = g_n // gps
        eye = jnp.eye(gps, dtype=F32)
        toep = jnp.einsum('stogij,gh->osgjthi', kt.reshape(sub, sub, n_slab, gps, gs, gs), eye)
        toep = toep.reshape(n_slab, sub * LANES, sub * LANES)
        e_in = (sub - 1 - s_idx) if dr == 0 else s_idx

        def in_map(pb):
            w = jnp.einsum('sogpj,gh->osgjhp', pb[e_in].reshape(sub, n_slab, gps, S5_STATE, gs), eye)
            return w.reshape(n_slab, sub * LANES, gps * S5_STATE)

        win = jnp.concatenate([in_map(pb_r), in_map(pb_i)], axis=2)
        e_out = (s_idx + 1) if dr == 0 else (sub - s_idx)

        def out_map(cp):
            w = jnp.einsum('togip,gh->ogpthi', cp[e_out].reshape(sub, n_slab, gps, gs, S5_STATE), eye)
            return w.reshape(n_slab, gps * S5_STATE, sub * LANES)

        wout = jnp.concatenate([out_map(cp_r), -out_map(cp_i)], axis=1)
        a_sub = jnp.stack([pw_r[sub].reshape(n_slab, gps * S5_STATE),
                           pw_i[sub].reshape(n_slab, gps * S5_STATE)], axis=1)
        outs.append((win, toep, wout, a_sub))

    stack = lambda k, dt: jnp.stack([o[k] for o in outs], axis=1).astype(dt)
    return stack(0, BF16), stack(1, BF16), stack(2, BF16), stack(3, F32)


def _s5_core(hs, weights, n_ctx):
    b, n_slab, t, _ = hs.shape
    sub = S5_SUB
    win, toep, wout, a_sub = weights
    m = t // sub
    m_ctx = n_ctx // sub
    assert m % 8 == 0 and m_ctx % 8 == 0
    cw = sub * LANES
    sw = win.shape[-1]
    u = hs.reshape(b, n_slab, m, cw)
    kern = functools.partial(_s5_core_kernel, m_ctx=m_ctx)
    y = pl.pallas_call(
        kern,
        grid=(n_slab, b, 2),
        in_specs=[pl.BlockSpec((None, None, m, cw), lambda o, b, d: (b, o, 0, 0)),
                  pl.BlockSpec((None, None, cw, sw), lambda o, b, d: (o, d, 0, 0)),
                  pl.BlockSpec((None, None, cw, cw), lambda o, b, d: (o, d, 0, 0)),
                  pl.BlockSpec((None, None, sw, cw), lambda o, b, d: (o, d, 0, 0)),
                  pl.BlockSpec((None, None, 2, sw // 2), lambda o, b, d: (o, d, 0, 0))],
        out_specs=pl.BlockSpec((None, None, None, m, cw), lambda o, b, d: (d, b, o, 0, 0)),
        out_shape=jax.ShapeDtypeStruct((2, b, n_slab, m, cw), BF16),
        scratch_shapes=[pltpu.VMEM((m, sw), F32)],
        compiler_params=_cparams(("arbitrary", "arbitrary", "arbitrary")),
        name="s5_core",
    )(u, win, toep, wout, a_sub)
    return y.reshape(2, b, n_slab, t, LANES)


def _ret_in_kernel(x_ref, mod_ref, g_ref, w_ref, cos_ref, sin_ref, z_ref, *, n_rope_cols):
    h = _norm_mod(x_ref[...], g_ref[...], mod_ref[0:1, :], mod_ref[1:2, :]).astype(BF16)
    cos, sin = cos_ref[...], sin_ref[...]
    half = cos.shape[1]
    tn = 4 * half
    for n0 in range(0, w_ref.shape[1], tn):
        z = jnp.dot(h, w_ref[:, n0:n0 + tn], preferred_element_type=F32)
        if n0 < n_rope_cols:
            for c0 in range(0, tn, 2 * half):
                x1 = z[:, c0:c0 + half]
                x2 = z[:, c0 + half:c0 + 2 * half]
                z_ref[:, n0 + c0:n0 + c0 + half] = (x1 * cos - x2 * sin).astype(z_ref.dtype)
                z_ref[:, n0 + c0 + half:n0 + c0 + 2 * half] = (x2 * cos + x1 * sin).astype(z_ref.dtype)
        else:
            z_ref[:, n0:n0 + tn] = z.astype(z_ref.dtype)


def _ret_in(x, modsel, g, w, cos, sin, n_rope_cols):
    b, t, d = x.shape
    n_out = w.shape[1]
    tm = ROW_TILE
    half = cos.shape[1]
    assert n_out % (4 * half) == 0 and n_rope_cols % (4 * half) == 0
    kern = functools.partial(_ret_in_kernel, n_rope_cols=n_rope_cols)
    return pl.pallas_call(
        kern,
        grid=(b, t // tm),
        in_specs=[pl.BlockSpec((None, tm, d), lambda b, i: (b, i, 0)),
                  _mod_spec(d),
                  pl.BlockSpec((1, d), lambda b, i: (0, 0)),
                  pl.BlockSpec((d, n_out), lambda b, i: (0, 0), pipeline_mode=pl.Buffered(1)),
                  pl.BlockSpec((tm, half), lambda b, i: (i, 0)),
                  pl.BlockSpec((tm, half), lambda b, i: (i, 0))],
        out_specs=pl.BlockSpec((None, tm, n_out), lambda b, i: (b, i, 0)),
        out_shape=jax.ShapeDtypeStruct((b, t, n_out), BF16),
        compiler_params=_cparams(("arbitrary", "arbitrary")),
        name="ret_in",
    )(x, modsel, g.reshape(1, d), w, cos, sin)


def _ret_core_kernel(lg_ref, q_ref, k_ref, v_ref, *rest, rev, qk_scale):
    if rev:
        of_ref, g_ref, o_ref, s_scr = rest
    else:
        o_ref, s_scr = rest
    hg = pl.program_id(1)
    c = pl.program_id(2)
    tc = q_ref.shape[0]
    n_hp, dqk, dv = s_scr.shape

    @pl.when(c == 0)
    def _():
        s_scr[...] = jnp.zeros(s_scr.shape, F32)

    row = lax.broadcasted_iota(jnp.int32, (tc, tc), 0)
    col = lax.broadcasted_iota(jnp.int32, (tc, tc), 1)
    diff = (col - row) if rev else (row - col)
    pos = lax.broadcasted_iota(jnp.int32, (tc, 1), 0).astype(F32)
    for hp in range(n_hp):
        lg = jnp.full((1, 1), lg_ref[hg * n_hp + hp], F32)
        inner = jnp.where(diff >= 0, jnp.exp(lg * jnp.maximum(diff, 0).astype(F32)), 0.0) * qk_scale
        if rev:
            q_dec = jnp.exp(lg * (tc - pos))
            k_dec = jnp.exp(lg * pos) * qk_scale
        else:
            q_dec = jnp.exp(lg * (pos + 1.0))
            k_dec = jnp.exp(lg * (tc - 1.0 - pos)) * qk_scale
        q = q_ref[:, hp * dqk:(hp + 1) * dqk]
        k = k_ref[:, hp * dqk:(hp + 1) * dqk]
        v = v_ref[:, hp * dv:(hp + 1) * dv]
        s = lax.dot_general(q, k, (((1,), (1,)), ((), ())), preferred_element_type=F32) * inner
        state = s_scr[hp]
        o = (jnp.dot(s.astype(BF16), v, preferred_element_type=F32)
             + jnp.dot((q.astype(F32) * q_dec).astype(BF16), state.astype(BF16), preferred_element_type=F32))
        kd_t = (k.astype(F32) * k_dec).T.astype(BF16)
        s_scr[hp] = state * jnp.exp(lg * tc) + jnp.dot(kd_t, v, preferred_element_type=F32)
        if rev:
            tot = of_ref[:, hp * dv:(hp + 1) * dv] + o
            nrm = tot * lax.rsqrt(jnp.mean(tot * tot, axis=-1, keepdims=True) + NORM_EPS)
            gg = g_ref[:, hp * dv:(hp + 1) * dv].astype(F32)
            o_ref[:, hp * dv:(hp + 1) * dv] = (gg * jax.nn.sigmoid(gg) * nrm).astype(o_ref.dtype)
        else:
            o_ref[:, hp * dv:(hp + 1) * dv] = o


def _ret_core(z, log_gamma, o_fwd, rev):
    b, t, _ = z.shape
    nh = RET_HEADS
    dqk = z.shape[2] // (6 * nh)
    dv = 2 * dqk
    tc = ROW_TILE
    nc = t // tc

    if rev:
        def cmap(c):
            return jnp.where(c == 0, 0, nc - c)
    else:
        def cmap(c):
            return c

    hp = RET_HEADS_PER_STEP
    ng = nh // hp
    kern = functools.partial(_ret_core_kernel, rev=rev, qk_scale=dqk ** -0.5)
    in_specs = [pl.BlockSpec((None, tc, hp * dqk), lambda b, h, c, lg: (b, cmap(c), h)),
                pl.BlockSpec((None, tc, hp * dqk), lambda b, h, c, lg: (b, cmap(c), ng + h)),
                pl.BlockSpec((None, tc, hp * dv), lambda b, h, c, lg: (b, cmap(c), ng + h))]
    args = [z, z, z]
    if rev:
        in_specs += [pl.BlockSpec((None, tc, hp * dv), lambda b, h, c, lg: (b, cmap(c), h)),
                     pl.BlockSpec((None, tc, hp * dv), lambda b, h, c, lg: (b, cmap(c), 2 * ng + h))]
        args += [o_fwd, z]
    out_dtype = BF16 if rev else F32
    return pl.pallas_call(
        kern,
        grid_spec=pltpu.PrefetchScalarGridSpec(
            num_scalar_prefetch=1,
            grid=(b, ng, nc),
            in_specs=in_specs,
            out_specs=pl.BlockSpec((None, tc, hp * dv), lambda b, h, c, lg: (b, cmap(c), h)),
            scratch_shapes=[pltpu.VMEM((hp, dqk, dv), F32)]),
        out_shape=jax.ShapeDtypeStruct((b, t, nh * dv), out_dtype),
        compiler_params=_cparams(("arbitrary", "arbitrary", "arbitrary")),
        name="ret_core_bwd" if rev else "ret_core_fwd",
    )(log_gamma, *args)


def _final_kernel(x_ref, g_ref, o_ref):
    x = x_ref[...]
    o_ref[...] = x * lax.rsqrt(jnp.mean(x * x, axis=-1, keepdims=True) + NORM_EPS) * g_ref[...]


def _final_norm(x, g, n_ctx):
    b, t, d = x.shape
    tm = ROW_TILE
    skip = n_ctx // tm
    return pl.pallas_call(
        _final_kernel,
        grid=(b, (t - n_ctx) // tm),
        in_specs=[pl.BlockSpec((None, tm, d), lambda b, i: (b, i + skip, 0)),
                  pl.BlockSpec((1, d), lambda b, i: (0, 0))],
        out_specs=pl.BlockSpec((None, tm, d), lambda b, i: (b, i, 0)),
        out_shape=jax.ShapeDtypeStruct((b, t - n_ctx, d), F32),
        compiler_params=_cparams(("arbitrary", "arbitrary")),
        name="final_norm",
    )(x, g.reshape(1, d))


def _rope_tables(n_ctx, n_lat, head_dim):
    rows = n_lat // GRID_W
    row = jnp.repeat(jnp.arange(rows, dtype=F32), GRID_W)
    col = jnp.tile(jnp.arange(GRID_W, dtype=F32), rows)
    n_freq = head_dim // 4
    inv_freq = ROPE_THETA ** (-jnp.arange(n_freq, dtype=F32) / n_freq)
    ang = jnp.concatenate([row[:, None] * inv_freq, col[:, None] * inv_freq], axis=-1)
    cos = jnp.concatenate([jnp.ones((n_ctx, head_dim // 2), F32), jnp.cos(ang)], axis=0)
    sin = jnp.concatenate([jnp.zeros((n_ctx, head_dim // 2), F32), jnp.sin(ang)], axis=0)
    return cos, sin


def kernel(x, c, ctx, c_ctx, mod_w, mod_b, norm_g, attn_w_qkv, attn_w_o, attn_q_gain, attn_k_gain,
           s5_lambda_re, s5_lambda_im, s5_log_dt, s5_b_re, s5_b_im, s5_c_re, s5_c_im, s5_d, s5_w_glu, s5_b_glu,
           ret_w_qkvg, ret_w_o, ret_log_decay, moe_w_router, moe_w_up, moe_w_down, final_g):
    bsz, n_lat, d = x.shape
    n_ctx = ctx.shape[1]
    depth = mod_w.shape[0]
    assert n_ctx == ROW_TILE and n_lat % ROW_TILE == 0 and bsz <= 7

    xs = jnp.concatenate([ctx, x], axis=1).astype(F32)
    cvec = jnp.zeros((8, d), F32).at[:bsz].set(c).at[bsz].set(c_ctx)
    mods = _modulation(cvec, mod_w, mod_b)

    cos_a, sin_a = _rope_tables(n_ctx, n_lat, ATTN_HEAD_DIM)
    cos2 = jnp.concatenate([cos_a, cos_a], axis=1)
    sin2 = jnp.concatenate([-sin_a, sin_a], axis=1)
    cos_r, sin_r = _rope_tables(n_ctx, n_lat, d // RET_HEADS)

    tm = ROW_TILE

    def row_spec(width):
        return pl.BlockSpec((None, tm, width), lambda b, i: (b, i, 0))

    for i in range(depth):
        kind, j = i % N_MIXERS, i // N_MIXERS
        m6 = mods[i].reshape(8, 6, d)
        modsel = jnp.stack([jnp.broadcast_to(m6[bsz], (bsz, 6, d)), m6[:bsz]], axis=1)
        wr_t = moe_w_router[i].T.astype(F32)
        if kind == 0:
            q, k, v = _attn_in(xs, modsel, norm_g[i, 0], attn_w_qkv[j].astype(BF16),
                               attn_q_gain[j], attn_k_gain[j], cos2, sin2)
            o = _flash(q, k, v, n_ctx)
            x_mid, h2, lg = _post_call(_post_plain_kernel, "attn_out", [o], [row_spec(o.shape[-1])],
                                       [attn_w_o[j].astype(BF16)], xs, modsel, norm_g[i, 1], wr_t)
        elif kind == 1:
            hs = _norm_slabs(xs, modsel, norm_g[i, 0])
            w5 = _s5_weights(s5_lambda_re[j], s5_lambda_im[j], s5_log_dt[j], s5_b_re[j], s5_b_im[j],
                             s5_c_re[j], s5_c_im[j])
            y = _s5_core(hs, w5, n_ctx)
            n_slab = d // LANES
            slab_specs = [pl.BlockSpec((2, None, n_slab, tm, LANES), lambda b, i: (0, b, 0, i, 0)),
                          pl.BlockSpec((None, n_slab, tm, LANES), lambda b, i: (b, 0, i, 0))]
            x_mid, h2, lg = _post_call(_post_glu_kernel, "s5_out", [y, hs], slab_specs,
                                       [s5_d[j].reshape(1, d).astype(F32), s5_w_glu[j].astype(BF16),
                                        s5_b_glu[j].reshape(1, -1).astype(F32)],
                                       xs, modsel, norm_g[i, 1], wr_t)
        else:
            z = _ret_in(xs, modsel, norm_g[i, 0], ret_w_qkvg[j].astype(BF16), cos_r, sin_r, 2 * d)
            o_f = None
            for dr in range(2):
                log_gamma = -jnp.abs(ret_log_decay[j, dr].astype(F32))
                o_f = _ret_core(z, log_gamma, o_f, rev=(dr == 1))
            x_mid, h2, lg = _post_call(_post_plain_kernel, "ret_out", [o_f], [row_spec(o_f.shape[-1])],
                                       [ret_w_o[j].astype(BF16)], xs, modsel, norm_g[i, 1], wr_t)
        xs = _moe(x_mid, h2, lg, modsel, moe_w_up, moe_w_down, i, n_ctx)

    return _final_norm(xs, final_g, n_ctx).astype(x.dtype)
```

```python
import functools
import math

import jax
import jax.numpy as jnp
from jax import lax
from jax.experimental import pallas as pl
from jax.experimental.pallas import tpu as pltpu

F32 = jnp.float32
BF16 = jnp.bfloat16

GRID_W = 64
N_MIXERS = 3
NORM_EPS = 1e-6
ROPE_THETA = 10000.0
ATTN_HEAD_DIM = 128
ATTN_GROUP = 4
S5_GROUP_SIZE = 16
S5_STATE = 64
S5_MIN_DECAY = 1e-4
RET_HEADS = 4
N_EXPERTS = 16
EC_CAPACITY_FACTOR = 2

LANES = 128
ROW_ALIGN_BF16 = 16
ROW_TILE = 256
RET_HEADS_PER_STEP = 2
POST_PARTS = 2
COMBINE_TILE = 128
S5_SUB = 8
VMEM_LIMIT = 48 * 1024 * 1024
HI = lax.Precision.HIGHEST


def _cparams(sem):
    return pltpu.CompilerParams(dimension_semantics=sem, vmem_limit_bytes=VMEM_LIMIT)


def _norm_mod(x, g, shift, scale):
    ms = jnp.mean(x * x, axis=-1, keepdims=True)
    return (x * lax.rsqrt(ms + NORM_EPS) * g) * (1.0 + scale) + shift


def _largest_divisor(n, cap, mult):
    best = None
    for t in range(mult, min(n, cap) + 1, mult):
        if n % t == 0:
            best = t
    assert best is not None, (n, cap, mult)
    return best


def _mod_kernel(c_ref, w_ref, b_ref, o_ref):
    c = c_ref[...]
    s = c * jax.nn.sigmoid(c)
    o_ref[...] = jnp.dot(s, w_ref[...], precision=HI, preferred_element_type=F32) + b_ref[...]


def _modulation(cvec, mod_w, mod_b):
    depth, d, n = mod_w.shape
    tn = _largest_divisor(n, 1536, 128)
    return pl.pallas_call(
        _mod_kernel,
        grid=(depth, n // tn),
        in_specs=[pl.BlockSpec((8, d), lambda l, j: (0, 0)),
                  pl.BlockSpec((None, d, tn), lambda l, j: (l, 0, j)),
                  pl.BlockSpec((None, 1, tn), lambda l, j: (l, 0, j))],
        out_specs=pl.BlockSpec((None, 8, tn), lambda l, j: (l, 0, j)),
        out_shape=jax.ShapeDtypeStruct((depth, 8, n), F32),
        compiler_params=_cparams(("arbitrary", "arbitrary")),
        name="modulation",
    )(cvec, mod_w, mod_b.reshape(depth, 1, n))


def _mod_spec(d):
    return pl.BlockSpec((None, None, 6, d), lambda b, i, *_: (b, jnp.minimum(i, 1), 0, 0))


def _attn_in_kernel(x_ref, mod_ref, g_ref, w_ref, qg_ref, kg_ref, cos_ref, sin_ref,
                    q_ref, k_ref, v_ref, *, n_q, n_kv):
    hd = ATTN_HEAD_DIM
    h = _norm_mod(x_ref[...], g_ref[...], mod_ref[0:1, :], mod_ref[1:2, :]).astype(BF16)
    qkv = jnp.dot(h, w_ref[...], preferred_element_type=F32)
    cos, sin = cos_ref[...], sin_ref[...]

    def norm_rope(t, gain):
        t = t * lax.rsqrt(jnp.mean(t * t, axis=-1, keepdims=True) + NORM_EPS) * gain
        return t * cos + pltpu.roll(t, hd // 2, axis=1) * sin

    scale = hd ** -0.5 * math.log2(math.e)
    for j in range(n_q):
        q_ref[j] = (norm_rope(qkv[:, j * hd:(j + 1) * hd], qg_ref[...]) * scale).astype(BF16)
    lane = lax.broadcasted_iota(jnp.int32, (qkv.shape[0], hd), 1)
    ones_col = jnp.where(lane == 0, 1.0, 0.0).astype(BF16)
    for j in range(n_kv):
        c0 = (n_q + j) * hd
        k_ref[j] = norm_rope(qkv[:, c0:c0 + hd], kg_ref[...]).astype(BF16)
        c1 = (n_q + n_kv + j) * hd
        v_ref[j, :, 0:hd] = qkv[:, c1:c1 + hd].astype(BF16)
        v_ref[j, :, hd:2 * hd] = ones_col


def _attn_in(x, modsel, g, w_qkv, q_gain, k_gain, cos2, sin2):
    b, t, d = x.shape
    hd = ATTN_HEAD_DIM
    n_tot = w_qkv.shape[1] // hd
    n_q = d // hd
    n_kv = (n_tot - n_q) // 2
    tm = ROW_TILE
    kern = functools.partial(_attn_in_kernel, n_q=n_q, n_kv=n_kv)
    return pl.pallas_call(
        kern,
        grid=(b, t // tm),
        in_specs=[pl.BlockSpec((None, tm, d), lambda b, i: (b, i, 0)),
                  _mod_spec(d),
                  pl.BlockSpec((1, d), lambda b, i: (0, 0)),
                  pl.BlockSpec(w_qkv.shape, lambda b, i: (0, 0)),
                  pl.BlockSpec((1, hd), lambda b, i: (0, 0)),
                  pl.BlockSpec((1, hd), lambda b, i: (0, 0)),
                  pl.BlockSpec((tm, hd), lambda b, i: (i, 0)),
                  pl.BlockSpec((tm, hd), lambda b, i: (i, 0))],
        out_specs=[pl.BlockSpec((None, n_q, tm, hd), lambda b, i: (b, 0, i, 0)),
                   pl.BlockSpec((None, n_kv, tm, hd), lambda b, i: (b, 0, i, 0)),
                   pl.BlockSpec((None, n_kv, tm, 2 * hd), lambda b, i: (b, 0, i, 0))],
        out_shape=[jax.ShapeDtypeStruct((b, n_q, t, hd), BF16),
                   jax.ShapeDtypeStruct((b, n_kv, t, hd), BF16),
                   jax.ShapeDtypeStruct((b, n_kv, t, 2 * hd), BF16)],
        compiler_params=_cparams(("arbitrary", "arbitrary")),
        name="attn_in",
    )(x, modsel, g.reshape(1, d), w_qkv, q_gain.reshape(1, hd), k_gain.reshape(1, hd), cos2, sin2)


def _flash_kernel(q_ref, k_ref, v_ref, o_ref, s0_scr, s1_scr, m_scr, acc_scr, *, tq, n_ctx, n_key_tiles):
    hd = ATTN_HEAD_DIM
    qi = pl.program_id(2)
    j = pl.program_id(3)
    nk = n_key_tiles

    @pl.when(j == 0)
    def _():
        m_scr[...] = jnp.full(m_scr.shape, -1e30, F32)
        acc_scr[...] = jnp.zeros(acc_scr.shape, F32)

    def softmax_update(s, v):
        m_prev = m_scr[...]
        m_new = jnp.maximum(m_prev, jnp.max(s, axis=1, keepdims=True))
        alpha = jnp.exp2(m_prev - m_new)
        p = jnp.exp2((s - m_new).astype(BF16))
        acc_scr[...] = alpha * acc_scr[...] + jnp.dot(p, v, preferred_element_type=F32)
        m_scr[...] = m_new

    def scores(k):
        q = q_ref[...].reshape(ATTN_GROUP * tq, hd)
        return lax.dot_general(q, k, (((1,), (1,)), ((), ())), preferred_element_type=F32)

    def pipelined(s_new, s_old):
        s_new[...] = scores(k_ref[...])
        softmax_update(s_old[...], v_ref[...])

    latent = qi > 0
    odd = jnp.bitwise_and(j, 1)
    middle = jnp.logical_and(latent, jnp.logical_and(j > 0, j < nk))

    @pl.when(jnp.logical_and(latent, j == 0))
    def _():
        s0_scr[...] = scores(k_ref[...])

    @pl.when(jnp.logical_and(middle, odd == 0))
    def _():
        pipelined(s0_scr, s1_scr)

    @pl.when(jnp.logical_and(middle, odd == 1))
    def _():
        pipelined(s1_scr, s0_scr)

    @pl.when(jnp.logical_and(latent, j == nk))
    def _():
        last = s1_scr if (nk - 1) % 2 else s0_scr
        softmax_update(last[...], v_ref[...])

    @pl.when(jnp.logical_and(qi == 0, j == 0))
    def _():
        softmax_update(scores(k_ref[0:n_ctx, :]), v_ref[0:n_ctx, :])

    @pl.when(j == nk)
    def _():
        o = acc_scr[:, 0:hd] / acc_scr[:, hd:hd + 1]
        for g in range(ATTN_GROUP):
            o_ref[:, g * hd:(g + 1) * hd] = o[g * tq:(g + 1) * tq, :].astype(o_ref.dtype)


def _flash(q, k, v, n_ctx):
    b, n_q, t, hd = q.shape
    n_kv = k.shape[1]
    tq = ROW_TILE
    assert n_ctx == tq and n_q == n_kv * ATTN_GROUP
    tk = _largest_divisor(t, 1280, 256)
    nk = t // tk
    gw = ATTN_GROUP * hd
    rows = ATTN_GROUP * tq
    kern = functools.partial(_flash_kernel, tq=tq, n_ctx=n_ctx, n_key_tiles=nk)

    def k_map(b, h, i, j):
        return (b, h, jnp.where(i == 0, 0, jnp.minimum(j, nk - 1)), 0)

    def v_map(b, h, i, j):
        return (b, h, jnp.where(i == 0, 0, jnp.maximum(j - 1, 0)), 0)

    return pl.pallas_call(
        kern,
        grid=(b, n_kv, t // tq, nk + 1),
        in_specs=[pl.BlockSpec((None, ATTN_GROUP, tq, hd), lambda b, h, i, j: (b, h, i, 0)),
                  pl.BlockSpec((None, None, tk, hd), k_map),
                  pl.BlockSpec((None, None, tk, 2 * hd), v_map)],
        out_specs=pl.BlockSpec((None, tq, gw), lambda b, h, i, j: (b, i, h)),
        out_shape=jax.ShapeDtypeStruct((b, t, n_q * hd), BF16),
        scratch_shapes=[pltpu.VMEM((rows, tk), F32),
                        pltpu.VMEM((rows, tk), F32),
                        pltpu.VMEM((rows, 1), F32),
                        pltpu.VMEM((rows, 2 * hd), F32)],
        compiler_params=_cparams(("arbitrary", "arbitrary", "arbitrary", "arbitrary")),
        name="flash_attn",
    )(q, k, v)


def _post_tail(rows, x_new, mod_ref, g2_ref, wr_ref, x_out, h_out, lg_out):
    x_out[rows, :] = x_new
    h2 = _norm_mod(x_new, g2_ref[...], mod_ref[3:4, :], mod_ref[4:5, :])
    h_out[rows, :] = h2
    lg_out[:, rows] = lax.dot_general(wr_ref[...], h2, (((1,), (1,)), ((), ())),
                                      precision=HI, preferred_element_type=F32)


def _row_parts(tm):
    part = tm // POST_PARTS
    return [slice(p * part, (p + 1) * part) for p in range(POST_PARTS)]


def _post_plain_kernel(o_ref, w_ref, x_ref, mod_ref, g2_ref, wr_ref, x_out, h_out, lg_out):
    for rows in _row_parts(x_ref.shape[0]):
        y = jnp.dot(o_ref[rows, :], w_ref[...], preferred_element_type=F32)
        _post_tail(rows, x_ref[rows, :] + mod_ref[2:3, :] * y, mod_ref, g2_ref, wr_ref, x_out, h_out, lg_out)


def _post_glu_kernel(y_ref, hs_ref, dsk_ref, w_ref, b_ref, x_ref, mod_ref, g2_ref, wr_ref,
                     x_out, h_out, lg_out):
    d = x_ref.shape[-1]
    for rows in _row_parts(x_ref.shape[0]):
        slabs = [hs_ref[o, rows, :].astype(F32) for o in range(hs_ref.shape[0])]
        ys = [y_ref[0, o, rows, :].astype(F32) + y_ref[1, o, rows, :].astype(F32) for o in range(hs_ref.shape[0])]
        yt = dsk_ref[...] * jnp.concatenate(slabs, axis=1) + jnp.concatenate(ys, axis=1)
        z = jnp.dot(jax.nn.gelu(yt).astype(BF16), w_ref[...], preferred_element_type=F32) + b_ref[...]
        y = z[:, :d] * jax.nn.sigmoid(z[:, d:])
        _post_tail(rows, x_ref[rows, :] + mod_ref[2:3, :] * y, mod_ref, g2_ref, wr_ref, x_out, h_out, lg_out)


def _post_call(kern, name, row_inputs, row_specs, const_inputs, x, modsel, g2, w_router_t):
    b, t, d = x.shape
    tm = ROW_TILE
    ne = w_router_t.shape[0]
    const_specs = [pl.BlockSpec(a.shape, lambda b, i: (0, 0)) for a in const_inputs]
    return pl.pallas_call(
        kern,
        grid=(b, t // tm),
        in_specs=row_specs + const_specs + [
            pl.BlockSpec((None, tm, d), lambda b, i: (b, i, 0)),
            _mod_spec(d),
            pl.BlockSpec((1, d), lambda b, i: (0, 0)),
            pl.BlockSpec((ne, d), lambda b, i: (0, 0))],
        out_specs=[pl.BlockSpec((None, tm, d), lambda b, i: (b, i, 0)),
                   pl.BlockSpec((None, tm, d), lambda b, i: (b, i, 0)),
                   pl.BlockSpec((None, ne, tm), lambda b, i: (b, 0, i))],
        out_shape=[jax.ShapeDtypeStruct((b, t, d), F32),
                   jax.ShapeDtypeStruct((b, t, d), F32),
                   jax.ShapeDtypeStruct((b, ne, t), F32)],
        compiler_params=_cparams(("arbitrary", "arbitrary")),
        name=name,
    )(*row_inputs, *const_inputs, x, modsel, g2.reshape(1, d), w_router_t)


def _ffn_kernel(idx_ref, h_hbm, wa_ref, wu_ref, wd_ref, gate_ref, o_ref,
                xf_scr, xb_scr, acc_scr, sem, *, rows_pad, per_step):
    e, m, f = pl.program_id(0), pl.program_id(1), pl.program_id(2)
    nm, nf = pl.num_programs(1), pl.num_programs(2)
    tm = xb_scr.shape[0]
    lin = e * nm + m
    slot = jnp.bitwise_and(lin, 1)
    last_tile = pl.num_programs(0) * nm - 1

    def row_copy(token, s, r):
        return pltpu.make_async_copy(h_hbm.at[pl.ds(token, 1), :], xf_scr.at[s, pl.ds(r, 1), :], sem.at[s])

    def wait_rows(s):
        pltpu.make_async_copy(h_hbm.at[pl.ds(0, rows_pad), :], xf_scr.at[s], sem.at[s]).wait()

    @pl.when(jnp.logical_and(lin == 0, f == 0))
    def _():
        def body(r, c):
            row_copy(idx_ref[r], 0, r).start()
            return c
        lax.fori_loop(0, rows_pad, body, 0)

    @pl.when(f == 0)
    def _():
        wait_rows(slot)
        xb_scr[...] = xf_scr[slot, 0:tm, :].astype(BF16)
        acc_scr[...] = jnp.zeros(acc_scr.shape, F32)

    base = (lin + 1) * rows_pad + f * per_step
    for k in range(per_step):
        row_copy(idx_ref[base + k], 1 - slot, f * per_step + k).start()

    x = xb_scr[...]
    a = jnp.dot(x, wa_ref[...].astype(BF16), preferred_element_type=F32)
    u = jnp.dot(x, wu_ref[...].astype(BF16), preferred_element_type=F32)
    hmid = (a * jax.nn.sigmoid(a) * u).astype(BF16)
    acc_scr[...] += jnp.dot(hmid, wd_ref[...].astype(BF16), preferred_element_type=F32)

    @pl.when(f == nf - 1)
    def _():
        o_ref[...] = (acc_scr[...] * gate_ref[...]).astype(o_ref.dtype)

    @pl.when(jnp.logical_and(lin == last_tile, f == nf - 1))
    def _():
        wait_rows(1 - slot)


def _expert_ffn(h_tokens, idx, w_up, w_down, layer, gate):
    ne, r = idx.shape
    d = h_tokens.shape[1]
    ff = w_down.shape[2]
    tm = _largest_divisor(r, 1040, 16)
    tf = _largest_divisor(ff, 256, 128)
    nm, nf = r // tm, ff // tf
    per_step = pl.cdiv(pl.cdiv(tm, nf), 8) * 8
    rows_pad = per_step * nf
    tiles = idx.reshape(ne * nm, tm)
    tiles = jnp.pad(tiles, ((0, 1), (0, rows_pad - tm)))
    kern = functools.partial(_ffn_kernel, rows_pad=rows_pad, per_step=per_step)
    return pl.pallas_call(
        kern,
        grid_spec=pltpu.PrefetchScalarGridSpec(
            num_scalar_prefetch=1,
            grid=(ne, nm, nf),
            in_specs=[pl.BlockSpec(memory_space=pl.ANY),
                      pl.BlockSpec((None, None, d, tf), lambda e, m, f, ix: (layer, e, 0, f)),
                      pl.BlockSpec((None, None, d, tf), lambda e, m, f, ix: (layer, e, 0, nf + f)),
                      pl.BlockSpec((None, None, tf, d), lambda e, m, f, ix: (layer, e, f, 0)),
                      pl.BlockSpec((None, tm, 1), lambda e, m, f, ix: (e, m, 0))],
            out_specs=pl.BlockSpec((None, tm, d), lambda e, m, f, ix: (e, m, 0)),
            scratch_shapes=[pltpu.VMEM((2, rows_pad, d), F32),
                            pltpu.VMEM((tm, d), BF16),
                            pltpu.VMEM((tm, d), F32),
                            pltpu.SemaphoreType.DMA((2,))]),
        out_shape=jax.ShapeDtypeStruct((ne, r, d), BF16),
        compiler_params=_cparams(("arbitrary", "arbitrary", "arbitrary")),
        name="expert_ffn",
    )(tiles.reshape(-1), h_tokens, w_up, w_up, w_down, gate)


def _combine_kernel(start_ref, rel_ref, x_ref, mod_ref, *refs):
    y_refs, o_ref = refs[:-1], refs[-1]
    tm = x_ref.shape[0]
    win = y_refs[0].shape[1]
    rel = rel_ref[...]
    lane = lax.broadcasted_iota(jnp.int32, (tm, win), 1)
    acc = jnp.zeros(x_ref.shape, F32)
    for e, y_ref in enumerate(y_refs):
        onehot = jnp.where(rel[:, e:e + 1] == lane, 1.0, 0.0).astype(BF16)
        acc = acc + jnp.dot(onehot, y_ref[0], preferred_element_type=F32)
    o_ref[...] = x_ref[...] + mod_ref[5:6, :] * acc


def _combine(x_mid, modsel, y, rel_t, start, n_ctx):
    b, t, d = x_mid.shape
    ne, r, _ = y.shape
    tm = COMBINE_TILE
    win = tm + ROW_ALIGN_BF16
    ctx_tiles = n_ctx // tm
    y_specs = [pl.BlockSpec((pl.Element(1), pl.Element(win), pl.Element(d)),
                            lambda b, i, st, e=e: (e, st[b, e, i] * ROW_ALIGN_BF16, 0))
               for e in range(ne)]
    return pl.pallas_call(
        _combine_kernel,
        grid_spec=pltpu.PrefetchScalarGridSpec(
            num_scalar_prefetch=1,
            grid=(b, t // tm),
            in_specs=[pl.BlockSpec((None, tm, ne), lambda b, i, st: (b, i, 0)),
                      pl.BlockSpec((None, tm, d), lambda b, i, st: (b, i, 0)),
                      pl.BlockSpec((None, None, 6, d),
                                   lambda b, i, st: (b, (i >= ctx_tiles).astype(jnp.int32), 0, 0))] + y_specs,
            out_specs=pl.BlockSpec((None, tm, d), lambda b, i, st: (b, i, 0))),
        out_shape=jax.ShapeDtypeStruct((b, t, d), F32),
        compiler_params=_cparams(("arbitrary", "arbitrary")),
        name="moe_combine",
    )(start, rel_t, x_mid, modsel, *([y] * ne))


def _route_segment(aff, cap):
    gate_u, idx_u = lax.top_k(aff, cap)
    kth = gate_u[..., -1:]
    gt = aff > kth
    eq = aff == kth
    need = cap - jnp.sum(gt, axis=-1, keepdims=True, dtype=jnp.int32)
    mask = gt | (eq & (jnp.cumsum(eq.astype(jnp.int32), axis=-1) <= need))
    m32 = mask.astype(jnp.int32)
    pos = jnp.cumsum(m32, axis=-1) - m32
    idx = jnp.sort(idx_u.astype(jnp.int32), axis=-1)
    gate = jnp.take_along_axis(aff, idx, axis=-1)
    return mask, pos, idx, gate


def _moe(x_mid, h2, logits_t, modsel, w_up, w_down, layer, n_ctx):
    b, t, d = x_mid.shape
    ne = logits_t.shape[1]
    tm = COMBINE_TILE
    win = tm + ROW_ALIGN_BF16
    assert n_ctx % tm == 0 and t % tm == 0
    aff = jax.nn.softmax(logits_t, axis=1)
    cap_l = EC_CAPACITY_FACTOR * (t - n_ctx) // ne
    cap_c = EC_CAPACITY_FACTOR * n_ctx // ne
    r = b * (cap_l + cap_c)
    assert r % ROW_ALIGN_BF16 == 0 and r >= win
    mask_l, pos_l, idx_l, gate_l = _route_segment(aff[:, :, n_ctx:], cap_l)
    mask_c, pos_c, idx_c, gate_c = _route_segment(aff[:, :, :n_ctx], cap_c)
    bi = jnp.arange(b, dtype=jnp.int32)[:, None, None]

    def per_expert(a):
        return a.transpose(1, 0, 2).reshape(ne, -1)

    idx = jnp.concatenate([per_expert(idx_l + bi * t + n_ctx), per_expert(idx_c + bi * t)], axis=1)
    gate = jnp.concatenate([per_expert(gate_l), per_expert(gate_c)], axis=1)
    y = _expert_ffn(h2.reshape(b * t, d), idx, w_up, w_down, layer, gate[..., None])

    row = jnp.concatenate([pos_c + b * cap_l + bi * cap_c, pos_l + bi * cap_l], axis=2)
    mask = jnp.concatenate([mask_c, mask_l], axis=2)
    start = jnp.minimum(row[:, :, ::tm] // ROW_ALIGN_BF16 * ROW_ALIGN_BF16, r - win)
    rel = jnp.where(mask, row - jnp.repeat(start, tm, axis=2), -1)
    return _combine(x_mid, modsel, y, rel.transpose(0, 2, 1), start // ROW_ALIGN_BF16, n_ctx)


def _norm_slab_kernel(x_ref, mod_ref, g_ref, h_ref):
    h = _norm_mod(x_ref[...], g_ref[...], mod_ref[0:1, :], mod_ref[1:2, :]).astype(h_ref.dtype)
    for o in range(h_ref.shape[0]):
        h_ref[o] = h[:, o * LANES:(o + 1) * LANES]


def _norm_slabs(x, modsel, g):
    b, t, d = x.shape
    tm = ROW_TILE
    n_slab = d // LANES
    return pl.pallas_call(
        _norm_slab_kernel,
        grid=(b, t // tm),
        in_specs=[pl.BlockSpec((None, tm, d), lambda b, i: (b, i, 0)),
                  _mod_spec(d),
                  pl.BlockSpec((1, d), lambda b, i: (0, 0))],
        out_specs=pl.BlockSpec((None, n_slab, tm, LANES), lambda b, i: (b, 0, i, 0)),
        out_shape=jax.ShapeDtypeStruct((b, n_slab, t, LANES), BF16),
        compiler_params=_cparams(("arbitrary", "arbitrary")),
        name="norm_mod",
    )(x, modsel, g.reshape(1, d))


def _s5_core_kernel(u_ref, win_ref, toep_ref, wout_ref, a_ref, y_ref, zh_scr, *, m_ctx):
    dr = pl.program_id(2)
    m = u_ref.shape[0]
    rc = _largest_divisor(m, 512, 16)
    half = zh_scr.shape[1] // 2
    for r0 in range(0, m, rc):
        zh_scr[r0:r0 + rc, :] = jnp.dot(u_ref[r0:r0 + rc, :], win_ref[...], preferred_element_type=F32)
    ar = a_ref[0:1, :]
    ai = a_ref[1:2, :]
    rid = lax.broadcasted_iota(jnp.int32, (8, half), 0)

    def visit(blk, carry, reverse):
        sr, si = carry
        base = pl.multiple_of(blk * 8, 8)
        z8 = zh_scr[pl.ds(base, 8), :]
        hr = jnp.zeros((8, half), F32)
        hi = jnp.zeros((8, half), F32)
        for r in (range(7, -1, -1) if reverse else range(8)):
            hr = jnp.where(rid == r, sr, hr)
            hi = jnp.where(rid == r, si, hi)
            zr = z8[r:r + 1, 0:half]
            zi = z8[r:r + 1, half:2 * half]
            sr, si = ar * sr - ai * si + zr, ar * si + ai * sr + zi
        zh_scr[pl.ds(base, 8), 0:half] = hr
        zh_scr[pl.ds(base, 8), half:2 * half] = hi
        return sr, si

    zero = (jnp.zeros((1, half), F32), jnp.zeros((1, half), F32))
    nb, nb_ctx = m // 8, m_ctx // 8

    @pl.when(dr == 0)
    def _():
        lax.fori_loop(0, nb, lambda s, cr: visit(s, cr, False), zero)

    @pl.when(dr == 1)
    def _():
        carry = lax.fori_loop(0, nb_ctx, lambda s, cr: visit(nb_ctx - 1 - s, cr, True), zero)
        lax.fori_loop(0, nb - nb_ctx, lambda s, cr: visit(nb - 1 - s, cr, True), carry)

    for r0 in range(0, m, rc):
        y = (jnp.dot(u_ref[r0:r0 + rc, :], toep_ref[...], preferred_element_type=F32)
             + jnp.dot(zh_scr[r0:r0 + rc, :].astype(BF16), wout_ref[...], preferred_element_type=F32))
        y_ref[r0:r0 + rc, :] = y.astype(y_ref.dtype)


def _s5_weights(lam_re, lam_im, log_dt, b_re, b_im, c_re, c_im):
    sub = S5_SUB
    gs = S5_GROUP_SIZE
    gps = LANES // gs
    outs = []
    for dr in range(2):
        lre = jnp.minimum(lam_re[dr].astype(F32), -S5_MIN_DECAY)
        lim = lam_im[dr].astype(F32)
        dt = jnp.exp(log_dt[dr].astype(F32))[:, None]
        mag = jnp.exp(lre * dt)
        ang = lim * dt
        lbr, lbi = mag * jnp.cos(ang), mag * jnp.sin(ang)
        den = lre * lre + lim * lim
        f_re = ((lbr - 1) * lre + lbi * lim) / den
        f_im = (lbi * lre - (lbr - 1) * lim) / den
        bre, bim = b_re[dr].astype(F32), b_im[dr].astype(F32)
        bbr = f_re[..., None] * bre - f_im[..., None] * bim
        bbi = f_re[..., None] * bim + f_im[..., None] * bre
        cr, ci = c_re[dr].astype(F32), c_im[dr].astype(F32)
        pr, pi = [jnp.ones_like(lbr)], [jnp.zeros_like(lbr)]
        for _ in range(sub):
            pr.append(pr[-1] * lbr - pi[-1] * lbi)
            pi.append(pr[-2] * lbi + pi[-1] * lbr)
        pw_r, pw_i = jnp.stack(pr), jnp.stack(pi)
        pb_r = pw_r[..., None] * bbr - pw_i[..., None] * bbi
        pb_i = pw_r[..., None] * bbi + pw_i[..., None] * bbr
        cp_r = cr[None] * pw_r[:, :, None, :] - ci[None] * pw_i[:, :, None, :]
        cp_i = cr[None] * pw_i[:, :, None, :] + ci[None] * pw_r[:, :, None, :]
        kk = (jnp.einsum('gip,tgpj->tgij', cr, pb_r[:sub], precision=HI)
              - jnp.einsum('gip,tgpj->tgij', ci, pb_i[:sub], precision=HI))
        s_idx = jnp.arange(sub)
        lag = (s_idx[None, :] - s_idx[:, None]) if dr == 0 else (s_idx[:, None] - s_idx[None, :])
        kt = kk[jnp.clip(lag, 0, sub - 1)]
        kt = jnp.where((lag >= 0)[:, :, None, None, None], kt, 0.0)
        g_n = kt.shape[2]
        n_slab = g_n // gps
        cw, sw = sub * LANES, gps * S5_STATE
        col_tgi = jnp.arange(cw)
        spread_ti = (jnp.arange(sub * gs)[:, None] == (col_tgi // LANES * gs + col_tgi % gs)[None, :]).astype(F32)
        spread_p = (jnp.arange(S5_STATE)[:, None] == (jnp.arange(sw) % S5_STATE)[None, :]).astype(F32)
        grp_tgi = col_tgi // gs % gps
        grp_gp = jnp.arange(sw) // S5_STATE

        def spread(table, spread_mat, row_grp, col_grp):
            rows = table.shape[1]
            full = jnp.dot(table.reshape(n_slab * rows, -1), spread_mat, precision=HI)
            return full.reshape(n_slab, rows, -1) * (row_grp[:, None] == col_grp[None, :]).astype(F32)

        kt6 = kt.reshape(sub, sub, n_slab, gps, gs, gs).transpose(2, 0, 3, 5, 1, 4)
        toep = spread(kt6.reshape(n_slab, cw, sub * gs), spread_ti, grp_tgi, grp_tgi)
        e_in = (sub - 1 - s_idx) if dr == 0 else s_idx

        def in_map(pb):
            t5 = pb[e_in].reshape(sub, n_slab, gps, S5_STATE, gs).transpose(1, 0, 2, 4, 3)
            return spread(t5.reshape(n_slab, cw, S5_STATE), spread_p, grp_tgi, grp_gp)

        win = jnp.concatenate([in_map(pb_r), in_map(pb_i)], axis=2)
        e_out = (s_idx + 1) if dr == 0 else (sub - s_idx)

        def out_map(cp):
            t5 = cp[e_out].reshape(sub, n_slab, gps, gs, S5_STATE).transpose(1, 2, 4, 0, 3)
            return spread(t5.reshape(n_slab, sw, sub * gs), spread_ti, grp_gp, grp_tgi)

        wout = jnp.concatenate([out_map(cp_r), -out_map(cp_i)], axis=1)
        a_sub = jnp.stack([pw_r[sub].reshape(n_slab, gps * S5_STATE),
                           pw_i[sub].reshape(n_slab, gps * S5_STATE)], axis=1)
        outs.append((win, toep, wout, a_sub))

    stack = lambda k, dt: jnp.stack([o[k] for o in outs], axis=1).astype(dt)
    return stack(0, BF16), stack(1, BF16), stack(2, BF16), stack(3, F32)


def _s5_core(hs, weights, n_ctx):
    b, n_slab, t, _ = hs.shape
    sub = S5_SUB
    win, toep, wout, a_sub = weights
    m = t // sub
    m_ctx = n_ctx // sub
    assert m % 8 == 0 and m_ctx % 8 == 0
    cw = sub * LANES
    sw = win.shape[-1]
    u = hs.reshape(b, n_slab, m, cw)
    kern = functools.partial(_s5_core_kernel, m_ctx=m_ctx)
    y = pl.pallas_call(
        kern,
        grid=(n_slab, b, 2),
        in_specs=[pl.BlockSpec((None, None, m, cw), lambda o, b, d: (b, o, 0, 0)),
                  pl.BlockSpec((None, None, cw, sw), lambda o, b, d: (o, d, 0, 0)),
                  pl.BlockSpec((None, None, cw, cw), lambda o, b, d: (o, d, 0, 0)),
                  pl.BlockSpec((None, None, sw, cw), lambda o, b, d: (o, d, 0, 0)),
                  pl.BlockSpec((None, None, 2, sw // 2), lambda o, b, d: (o, d, 0, 0))],
        out_specs=pl.BlockSpec((None, None, None, m, cw), lambda o, b, d: (d, b, o, 0, 0)),
        out_shape=jax.ShapeDtypeStruct((2, b, n_slab, m, cw), BF16),
        scratch_shapes=[pltpu.VMEM((m, sw), F32)],
        compiler_params=_cparams(("arbitrary", "arbitrary", "arbitrary")),
        name="s5_core",
    )(u, win, toep, wout, a_sub)
    return y.reshape(2, b, n_slab, t, LANES)


def _ret_in_kernel(x_ref, mod_ref, g_ref, w_ref, cos_ref, sin_ref, z_ref, *, n_rope_cols):
    h = _norm_mod(x_ref[...], g_ref[...], mod_ref[0:1, :], mod_ref[1:2, :]).astype(BF16)
    cos, sin = cos_ref[...], sin_ref[...]
    half = cos.shape[1]
    tn = 4 * half
    for n0 in range(0, w_ref.shape[1], tn):
        z = jnp.dot(h, w_ref[:, n0:n0 + tn], preferred_element_type=F32)
        if n0 < n_rope_cols:
            for c0 in range(0, tn, 2 * half):
                x1 = z[:, c0:c0 + half]
                x2 = z[:, c0 + half:c0 + 2 * half]
                z_ref[:, n0 + c0:n0 + c0 + half] = (x1 * cos - x2 * sin).astype(z_ref.dtype)
                z_ref[:, n0 + c0 + half:n0 + c0 + 2 * half] = (x2 * cos + x1 * sin).astype(z_ref.dtype)
        else:
            z_ref[:, n0:n0 + tn] = z.astype(z_ref.dtype)


def _ret_in(x, modsel, g, w, cos, sin, n_rope_cols):
    b, t, d = x.shape
    n_out = w.shape[1]
    tm = ROW_TILE
    half = cos.shape[1]
    assert n_out % (4 * half) == 0 and n_rope_cols % (4 * half) == 0
    kern = functools.partial(_ret_in_kernel, n_rope_cols=n_rope_cols)
    return pl.pallas_call(
        kern,
        grid=(b, t // tm),
        in_specs=[pl.BlockSpec((None, tm, d), lambda b, i: (b, i, 0)),
                  _mod_spec(d),
                  pl.BlockSpec((1, d), lambda b, i: (0, 0)),
                  pl.BlockSpec((d, n_out), lambda b, i: (0, 0), pipeline_mode=pl.Buffered(1)),
                  pl.BlockSpec((tm, half), lambda b, i: (i, 0)),
                  pl.BlockSpec((tm, half), lambda b, i: (i, 0))],
        out_specs=pl.BlockSpec((None, tm, n_out), lambda b, i: (b, i, 0)),
        out_shape=jax.ShapeDtypeStruct((b, t, n_out), BF16),
        compiler_params=_cparams(("arbitrary", "arbitrary")),
        name="ret_in",
    )(x, modsel, g.reshape(1, d), w, cos, sin)


def _ret_core_kernel(lg_ref, q_ref, k_ref, v_ref, *rest, rev, qk_scale):
    if rev:
        of_ref, g_ref, o_ref, s_scr = rest
    else:
        o_ref, s_scr = rest
    hg = pl.program_id(1)
    c = pl.program_id(2)
    tc = q_ref.shape[0]
    n_hp, dqk, dv = s_scr.shape

    @pl.when(c == 0)
    def _():
        s_scr[...] = jnp.zeros(s_scr.shape, F32)

    row = lax.broadcasted_iota(jnp.int32, (tc, tc), 0)
    col = lax.broadcasted_iota(jnp.int32, (tc, tc), 1)
    diff = (col - row) if rev else (row - col)
    pos = lax.broadcasted_iota(jnp.int32, (tc, 1), 0).astype(F32)
    for hp in range(n_hp):
        lg = jnp.full((1, 1), lg_ref[hg * n_hp + hp], F32)
        inner = jnp.where(diff >= 0, jnp.exp(lg * jnp.maximum(diff, 0).astype(F32)), 0.0) * qk_scale
        if rev:
            q_dec = jnp.exp(lg * (tc - pos))
            k_dec = jnp.exp(lg * pos) * qk_scale
        else:
            q_dec = jnp.exp(lg * (pos + 1.0))
            k_dec = jnp.exp(lg * (tc - 1.0 - pos)) * qk_scale
        q = q_ref[:, hp * dqk:(hp + 1) * dqk]
        k = k_ref[:, hp * dqk:(hp + 1) * dqk]
        v = v_ref[:, hp * dv:(hp + 1) * dv]
        s = lax.dot_general(q, k, (((1,), (1,)), ((), ())), preferred_element_type=F32) * inner
        state = s_scr[hp]
        o = (jnp.dot(s.astype(BF16), v, preferred_element_type=F32)
             + jnp.dot((q.astype(F32) * q_dec).astype(BF16), state.astype(BF16), preferred_element_type=F32))
        kd_t = (k.astype(F32) * k_dec).T.astype(BF16)
        s_scr[hp] = state * jnp.exp(lg * tc) + jnp.dot(kd_t, v, preferred_element_type=F32)
        if rev:
            tot = of_ref[:, hp * dv:(hp + 1) * dv] + o
            nrm = tot * lax.rsqrt(jnp.mean(tot * tot, axis=-1, keepdims=True) + NORM_EPS)
            gg = g_ref[:, hp * dv:(hp + 1) * dv].astype(F32)
            o_ref[:, hp * dv:(hp + 1) * dv] = (gg * jax.nn.sigmoid(gg) * nrm).astype(o_ref.dtype)
        else:
            o_ref[:, hp * dv:(hp + 1) * dv] = o


def _ret_core(z, log_gamma, o_fwd, rev):
    b, t, _ = z.shape
    nh = RET_HEADS
    dqk = z.shape[2] // (6 * nh)
    dv = 2 * dqk
    tc = ROW_TILE
    nc = t // tc

    if rev:
        def cmap(c):
            return jnp.where(c == 0, 0, nc - c)
    else:
        def cmap(c):
            return c

    hp = RET_HEADS_PER_STEP
    ng = nh // hp
    kern = functools.partial(_ret_core_kernel, rev=rev, qk_scale=dqk ** -0.5)
    in_specs = [pl.BlockSpec((None, tc, hp * dqk), lambda b, h, c, lg: (b, cmap(c), h)),
                pl.BlockSpec((None, tc, hp * dqk), lambda b, h, c, lg: (b, cmap(c), ng + h)),
                pl.BlockSpec((None, tc, hp * dv), lambda b, h, c, lg: (b, cmap(c), ng + h))]
    args = [z, z, z]
    if rev:
        in_specs += [pl.BlockSpec((None, tc, hp * dv), lambda b, h, c, lg: (b, cmap(c), h)),
                     pl.BlockSpec((None, tc, hp * dv), lambda b, h, c, lg: (b, cmap(c), 2 * ng + h))]
        args += [o_fwd, z]
    out_dtype = BF16 if rev else F32
    return pl.pallas_call(
        kern,
        grid_spec=pltpu.PrefetchScalarGridSpec(
            num_scalar_prefetch=1,
            grid=(b, ng, nc),
            in_specs=in_specs,
            out_specs=pl.BlockSpec((None, tc, hp * dv), lambda b, h, c, lg: (b, cmap(c), h)),
            scratch_shapes=[pltpu.VMEM((hp, dqk, dv), F32)]),
        out_shape=jax.ShapeDtypeStruct((b, t, nh * dv), out_dtype),
        compiler_params=_cparams(("arbitrary", "arbitrary", "arbitrary")),
        name="ret_core_bwd" if rev else "ret_core_fwd",
    )(log_gamma, *args)


def _final_kernel(x_ref, g_ref, o_ref):
    x = x_ref[...]
    o_ref[...] = x * lax.rsqrt(jnp.mean(x * x, axis=-1, keepdims=True) + NORM_EPS) * g_ref[...]


def _final_norm(x, g, n_ctx):
    b, t, d = x.shape
    tm = ROW_TILE
    skip = n_ctx // tm
    return pl.pallas_call(
        _final_kernel,
        grid=(b, (t - n_ctx) // tm),
        in_specs=[pl.BlockSpec((None, tm, d), lambda b, i: (b, i + skip, 0)),
                  pl.BlockSpec((1, d), lambda b, i: (0, 0))],
        out_specs=pl.BlockSpec((None, tm, d), lambda b, i: (b, i, 0)),
        out_shape=jax.ShapeDtypeStruct((b, t - n_ctx, d), F32),
        compiler_params=_cparams(("arbitrary", "arbitrary")),
        name="final_norm",
    )(x, g.reshape(1, d))


def _rope_tables(n_ctx, n_lat, head_dim):
    rows = n_lat // GRID_W
    row = jnp.repeat(jnp.arange(rows, dtype=F32), GRID_W)
    col = jnp.tile(jnp.arange(GRID_W, dtype=F32), rows)
    n_freq = head_dim // 4
    inv_freq = ROPE_THETA ** (-jnp.arange(n_freq, dtype=F32) / n_freq)
    ang = jnp.concatenate([row[:, None] * inv_freq, col[:, None] * inv_freq], axis=-1)
    cos = jnp.concatenate([jnp.ones((n_ctx, head_dim // 2), F32), jnp.cos(ang)], axis=0)
    sin = jnp.concatenate([jnp.zeros((n_ctx, head_dim // 2), F32), jnp.sin(ang)], axis=0)
    return cos, sin


def kernel(x, c, ctx, c_ctx, mod_w, mod_b, norm_g, attn_w_qkv, attn_w_o, attn_q_gain, attn_k_gain,
           s5_lambda_re, s5_lambda_im, s5_log_dt, s5_b_re, s5_b_im, s5_c_re, s5_c_im, s5_d, s5_w_glu, s5_b_glu,
           ret_w_qkvg, ret_w_o, ret_log_decay, moe_w_router, moe_w_up, moe_w_down, final_g):
    bsz, n_lat, d = x.shape
    n_ctx = ctx.shape[1]
    depth = mod_w.shape[0]
    assert n_ctx == ROW_TILE and n_lat % ROW_TILE == 0 and bsz <= 7

    xs = jnp.concatenate([ctx, x], axis=1).astype(F32)
    cvec = jnp.zeros((8, d), F32).at[:bsz].set(c).at[bsz].set(c_ctx)
    mods = _modulation(cvec, mod_w, mod_b)

    cos_a, sin_a = _rope_tables(n_ctx, n_lat, ATTN_HEAD_DIM)
    cos2 = jnp.concatenate([cos_a, cos_a], axis=1)
    sin2 = jnp.concatenate([-sin_a, sin_a], axis=1)
    cos_r, sin_r = _rope_tables(n_ctx, n_lat, d // RET_HEADS)

    tm = ROW_TILE

    def row_spec(width):
        return pl.BlockSpec((None, tm, width), lambda b, i: (b, i, 0))

    for i in range(depth):
        kind, j = i % N_MIXERS, i // N_MIXERS
        m6 = mods[i].reshape(8, 6, d)
        modsel = jnp.stack([jnp.broadcast_to(m6[bsz], (bsz, 6, d)), m6[:bsz]], axis=1)
        wr_t = moe_w_router[i].T.astype(F32)
        if kind == 0:
            q, k, v = _attn_in(xs, modsel, norm_g[i, 0], attn_w_qkv[j].astype(BF16),
                               attn_q_gain[j], attn_k_gain[j], cos2, sin2)
            o = _flash(q, k, v, n_ctx)
            x_mid, h2, lg = _post_call(_post_plain_kernel, "attn_out", [o], [row_spec(o.shape[-1])],
                                       [attn_w_o[j].astype(BF16)], xs, modsel, norm_g[i, 1], wr_t)
        elif kind == 1:
            hs = _norm_slabs(xs, modsel, norm_g[i, 0])
            w5 = _s5_weights(s5_lambda_re[j], s5_lambda_im[j], s5_log_dt[j], s5_b_re[j], s5_b_im[j],
                             s5_c_re[j], s5_c_im[j])
            y = _s5_core(hs, w5, n_ctx)
            n_slab = d // LANES
            slab_specs = [pl.BlockSpec((2, None, n_slab, tm, LANES), lambda b, i: (0, b, 0, i, 0)),
                          pl.BlockSpec((None, n_slab, tm, LANES), lambda b, i: (b, 0, i, 0))]
            x_mid, h2, lg = _post_call(_post_glu_kernel, "s5_out", [y, hs], slab_specs,
                                       [s5_d[j].reshape(1, d).astype(F32), s5_w_glu[j].astype(BF16),
                                        s5_b_glu[j].reshape(1, -1).astype(F32)],
                                       xs, modsel, norm_g[i, 1], wr_t)
        else:
            z = _ret_in(xs, modsel, norm_g[i, 0], ret_w_qkvg[j].astype(BF16), cos_r, sin_r, 2 * d)
            o_f = None
            for dr in range(2):
                log_gamma = -jnp.abs(ret_log_decay[j, dr].astype(F32))
                o_f = _ret_core(z, log_gamma, o_f, rev=(dr == 1))
            x_mid, h2, lg = _post_call(_post_plain_kernel, "ret_out", [o_f], [row_spec(o_f.shape[-1])],
                                       [ret_w_o[j].astype(BF16)], xs, modsel, norm_g[i, 1], wr_t)
        xs = _moe(x_mid, h2, lg, modsel, moe_w_up, moe_w_down, i, n_ctx)

    return _final_norm(xs, final_g, n_ctx).astype(x.dtype)
```

```python
import functools
import math

import jax
import jax.numpy as jnp
from jax import lax
from jax.experimental import pallas as pl
from jax.experimental.pallas import tpu as pltpu

F32 = jnp.float32
BF16 = jnp.bfloat16

GRID_W = 64
N_MIXERS = 3
NORM_EPS = 1e-6
ROPE_THETA = 10000.0
ATTN_HEAD_DIM = 128
ATTN_GROUP = 4
S5_GROUP_SIZE = 16
S5_STATE = 64
S5_MIN_DECAY = 1e-4
RET_HEADS = 4
N_EXPERTS = 16
EC_CAPACITY_FACTOR = 2

LANES = 128
ROW_ALIGN_BF16 = 16
ROW_TILE = 256
FLASH_KEY_TILE = 3328
FLASH_CHUNK = 256
RET_HEADS_PER_STEP = 2
POST_PARTS = 2
COMBINE_TILE = 128
S5_SUB = 8
VMEM_LIMIT = 48 * 1024 * 1024
HI = lax.Precision.HIGHEST


def _cparams(sem):
    return pltpu.CompilerParams(dimension_semantics=sem, vmem_limit_bytes=VMEM_LIMIT)


def _norm_mod(x, g, shift, scale):
    ms = jnp.mean(x * x, axis=-1, keepdims=True)
    return (x * lax.rsqrt(ms + NORM_EPS) * g) * (1.0 + scale) + shift


def _largest_divisor(n, cap, mult):
    best = None
    for t in range(mult, min(n, cap) + 1, mult):
        if n % t == 0:
            best = t
    assert best is not None, (n, cap, mult)
    return best


def _mod_kernel(c_ref, w_ref, b_ref, o_ref):
    c = c_ref[...]
    s = c * jax.nn.sigmoid(c)
    o_ref[...] = jnp.dot(s, w_ref[...], precision=HI, preferred_element_type=F32) + b_ref[...]


def _modulation(cvec, mod_w, mod_b):
    depth, d, n = mod_w.shape
    tn = _largest_divisor(n, 1536, 128)
    return pl.pallas_call(
        _mod_kernel,
        grid=(depth, n // tn),
        in_specs=[pl.BlockSpec((8, d), lambda l, j: (0, 0)),
                  pl.BlockSpec((None, d, tn), lambda l, j: (l, 0, j)),
                  pl.BlockSpec((None, 1, tn), lambda l, j: (l, 0, j))],
        out_specs=pl.BlockSpec((None, 8, tn), lambda l, j: (l, 0, j)),
        out_shape=jax.ShapeDtypeStruct((depth, 8, n), F32),
        compiler_params=_cparams(("arbitrary", "arbitrary")),
        name="modulation",
    )(cvec, mod_w, mod_b.reshape(depth, 1, n))


def _mod_spec(d):
    return pl.BlockSpec((None, None, 6, d), lambda b, i, *_: (b, jnp.minimum(i, 1), 0, 0))


def _attn_in_kernel(x_ref, mod_ref, g_ref, w_ref, qg_ref, kg_ref, cos_ref, sin_ref,
                    q_ref, k_ref, v_ref, *, n_q, n_kv):
    hd = ATTN_HEAD_DIM
    h = _norm_mod(x_ref[...], g_ref[...], mod_ref[0:1, :], mod_ref[1:2, :]).astype(BF16)
    qkv = jnp.dot(h, w_ref[...], preferred_element_type=F32)
    cos, sin = cos_ref[...], sin_ref[...]

    def norm_rope(t, gain):
        t = t * lax.rsqrt(jnp.mean(t * t, axis=-1, keepdims=True) + NORM_EPS) * gain
        return t * cos + pltpu.roll(t, hd // 2, axis=1) * sin

    scale = hd ** -0.5 * math.log2(math.e)
    for j in range(n_q):
        q_ref[j] = (norm_rope(qkv[:, j * hd:(j + 1) * hd], qg_ref[...]) * scale).astype(BF16)
    lane = lax.broadcasted_iota(jnp.int32, (qkv.shape[0], hd), 1)
    ones_col = jnp.where(lane == 0, 1.0, 0.0).astype(BF16)
    for j in range(n_kv):
        c0 = (n_q + j) * hd
        k_ref[j] = norm_rope(qkv[:, c0:c0 + hd], kg_ref[...]).astype(BF16)
        c1 = (n_q + n_kv + j) * hd
        v_ref[j, :, 0:hd] = qkv[:, c1:c1 + hd].astype(BF16)
        v_ref[j, :, hd:2 * hd] = ones_col


def _attn_in(x, modsel, g, w_qkv, q_gain, k_gain, cos2, sin2):
    b, t, d = x.shape
    hd = ATTN_HEAD_DIM
    n_tot = w_qkv.shape[1] // hd
    n_q = d // hd
    n_kv = (n_tot - n_q) // 2
    tm = ROW_TILE
    kern = functools.partial(_attn_in_kernel, n_q=n_q, n_kv=n_kv)
    return pl.pallas_call(
        kern,
        grid=(b, t // tm),
        in_specs=[pl.BlockSpec((None, tm, d), lambda b, i: (b, i, 0)),
                  _mod_spec(d),
                  pl.BlockSpec((1, d), lambda b, i: (0, 0)),
                  pl.BlockSpec(w_qkv.shape, lambda b, i: (0, 0)),
                  pl.BlockSpec((1, hd), lambda b, i: (0, 0)),
                  pl.BlockSpec((1, hd), lambda b, i: (0, 0)),
                  pl.BlockSpec((tm, hd), lambda b, i: (i, 0)),
                  pl.BlockSpec((tm, hd), lambda b, i: (i, 0))],
        out_specs=[pl.BlockSpec((None, n_q, tm, hd), lambda b, i: (b, 0, i, 0)),
                   pl.BlockSpec((None, n_kv, tm, hd), lambda b, i: (b, 0, i, 0)),
                   pl.BlockSpec((None, n_kv, tm, 2 * hd), lambda b, i: (b, 0, i, 0))],
        out_shape=[jax.ShapeDtypeStruct((b, n_q, t, hd), BF16),
                   jax.ShapeDtypeStruct((b, n_kv, t, hd), BF16),
                   jax.ShapeDtypeStruct((b, n_kv, t, 2 * hd), BF16)],
        compiler_params=_cparams(("arbitrary", "arbitrary")),
        name="attn_in",
    )(x, modsel, g.reshape(1, d), w_qkv, q_gain.reshape(1, hd), k_gain.reshape(1, hd), cos2, sin2)


def _flash_kernel(q_ref, k_ref, v_ref, o_ref, s0_scr, s1_scr, m_scr, acc_scr, *, tq, n_ctx, n_key_tiles):
    hd = ATTN_HEAD_DIM
    qi = pl.program_id(2)
    j = pl.program_id(3)
    nk = n_key_tiles

    @pl.when(j == 0)
    def _():
        m_scr[...] = jnp.full(m_scr.shape, -1e30, F32)
        acc_scr[...] = jnp.zeros(acc_scr.shape, F32)

    def softmax_update(s_ref, v_ref_, width):
        m_prev = m_scr[...]
        m_new = jnp.maximum(m_prev, jnp.max(s_ref[:, 0:width], axis=1, keepdims=True))
        alpha = jnp.exp2(m_prev - m_new)
        pv = None
        for c0 in range(0, width, FLASH_CHUNK):
            c1 = min(c0 + FLASH_CHUNK, width)
            p = jnp.exp2((s_ref[:, c0:c1] - m_new).astype(BF16))
            part = jnp.dot(p, v_ref_[c0:c1, :], preferred_element_type=F32)
            pv = part if pv is None else pv + part
        acc_scr[...] = alpha * acc_scr[...] + pv
        m_scr[...] = m_new

    def scores(k):
        q = q_ref[...].reshape(ATTN_GROUP * tq, hd)
        return lax.dot_general(q, k, (((1,), (1,)), ((), ())), preferred_element_type=F32)

    tk = k_ref.shape[0]

    def pipelined(s_new, s_old):
        s_new[...] = scores(k_ref[...])
        softmax_update(s_old, v_ref, tk)

    latent = qi > 0
    odd = jnp.bitwise_and(j, 1)
    middle = jnp.logical_and(latent, jnp.logical_and(j > 0, j < nk))

    @pl.when(jnp.logical_and(latent, j == 0))
    def _():
        s0_scr[...] = scores(k_ref[...])

    @pl.when(jnp.logical_and(middle, odd == 0))
    def _():
        pipelined(s0_scr, s1_scr)

    @pl.when(jnp.logical_and(middle, odd == 1))
    def _():
        pipelined(s1_scr, s0_scr)

    @pl.when(jnp.logical_and(latent, j == nk))
    def _():
        softmax_update(s1_scr if (nk - 1) % 2 else s0_scr, v_ref, tk)

    @pl.when(jnp.logical_and(qi == 0, j == 0))
    def _():
        s0_scr[:, 0:n_ctx] = scores(k_ref[0:n_ctx, :])
        softmax_update(s0_scr, v_ref, n_ctx)

    @pl.when(j == nk)
    def _():
        o = acc_scr[:, 0:hd] / acc_scr[:, hd:hd + 1]
        for g in range(ATTN_GROUP):
            o_ref[:, g * hd:(g + 1) * hd] = o[g * tq:(g + 1) * tq, :].astype(o_ref.dtype)


def _flash(q, k, v, n_ctx):
    b, n_q, t, hd = q.shape
    n_kv = k.shape[1]
    tq = ROW_TILE
    assert n_ctx == tq and n_q == n_kv * ATTN_GROUP
    tk = _largest_divisor(t, FLASH_KEY_TILE, 256)
    nk = t // tk
    gw = ATTN_GROUP * hd
    rows = ATTN_GROUP * tq
    kern = functools.partial(_flash_kernel, tq=tq, n_ctx=n_ctx, n_key_tiles=nk)

    def k_map(b, h, i, j):
        return (b, h, jnp.where(i == 0, 0, jnp.minimum(j, nk - 1)), 0)

    def v_map(b, h, i, j):
        return (b, h, jnp.where(i == 0, 0, jnp.maximum(j - 1, 0)), 0)

    return pl.pallas_call(
        kern,
        grid=(b, n_kv, t // tq, nk + 1),
        in_specs=[pl.BlockSpec((None, ATTN_GROUP, tq, hd), lambda b, h, i, j: (b, h, i, 0)),
                  pl.BlockSpec((None, None, tk, hd), k_map),
                  pl.BlockSpec((None, None, tk, 2 * hd), v_map)],
        out_specs=pl.BlockSpec((None, tq, gw), lambda b, h, i, j: (b, i, h)),
        out_shape=jax.ShapeDtypeStruct((b, t, n_q * hd), BF16),
        scratch_shapes=[pltpu.VMEM((rows, tk), F32),
                        pltpu.VMEM((rows, tk), F32),
                        pltpu.VMEM((rows, 1), F32),
                        pltpu.VMEM((rows, 2 * hd), F32)],
        compiler_params=_cparams(("arbitrary", "arbitrary", "arbitrary", "arbitrary")),
        name="flash_attn",
    )(q, k, v)


def _post_tail(rows, x_new, mod_ref, g2_ref, wr_ref, x_out, h_out, lg_out):
    x_out[rows, :] = x_new
    h2 = _norm_mod(x_new, g2_ref[...], mod_ref[3:4, :], mod_ref[4:5, :])
    h_out[rows, :] = h2
    lg_out[:, rows] = lax.dot_general(wr_ref[...], h2, (((1,), (1,)), ((), ())),
                                      precision=HI, preferred_element_type=F32)


def _row_parts(tm):
    part = tm // POST_PARTS
    return [slice(p * part, (p + 1) * part) for p in range(POST_PARTS)]


def _post_plain_kernel(o_ref, w_ref, x_ref, mod_ref, g2_ref, wr_ref, x_out, h_out, lg_out):
    for rows in _row_parts(x_ref.shape[0]):
        y = jnp.dot(o_ref[rows, :], w_ref[...], preferred_element_type=F32)
        _post_tail(rows, x_ref[rows, :] + mod_ref[2:3, :] * y, mod_ref, g2_ref, wr_ref, x_out, h_out, lg_out)


def _post_glu_kernel(y_ref, hs_ref, dsk_ref, w_ref, b_ref, x_ref, mod_ref, g2_ref, wr_ref,
                     x_out, h_out, lg_out):
    d = x_ref.shape[-1]
    for rows in _row_parts(x_ref.shape[0]):
        slabs = [hs_ref[o, rows, :].astype(F32) for o in range(hs_ref.shape[0])]
        ys = [y_ref[0, o, rows, :].astype(F32) + y_ref[1, o, rows, :].astype(F32) for o in range(hs_ref.shape[0])]
        yt = dsk_ref[...] * jnp.concatenate(slabs, axis=1) + jnp.concatenate(ys, axis=1)
        z = jnp.dot(jax.nn.gelu(yt).astype(BF16), w_ref[...], preferred_element_type=F32) + b_ref[...]
        y = z[:, :d] * jax.nn.sigmoid(z[:, d:])
        _post_tail(rows, x_ref[rows, :] + mod_ref[2:3, :] * y, mod_ref, g2_ref, wr_ref, x_out, h_out, lg_out)


def _post_call(kern, name, row_inputs, row_specs, const_inputs, x, modsel, g2, w_router_t):
    b, t, d = x.shape
    tm = ROW_TILE
    ne = w_router_t.shape[0]
    const_specs = [pl.BlockSpec(a.shape, lambda b, i: (0, 0)) for a in const_inputs]
    return pl.pallas_call(
        kern,
        grid=(b, t // tm),
        in_specs=row_specs + const_specs + [
            pl.BlockSpec((None, tm, d), lambda b, i: (b, i, 0)),
            _mod_spec(d),
            pl.BlockSpec((1, d), lambda b, i: (0, 0)),
            pl.BlockSpec((ne, d), lambda b, i: (0, 0))],
        out_specs=[pl.BlockSpec((None, tm, d), lambda b, i: (b, i, 0)),
                   pl.BlockSpec((None, tm, d), lambda b, i: (b, i, 0)),
                   pl.BlockSpec((None, ne, tm), lambda b, i: (b, 0, i))],
        out_shape=[jax.ShapeDtypeStruct((b, t, d), F32),
                   jax.ShapeDtypeStruct((b, t, d), F32),
                   jax.ShapeDtypeStruct((b, ne, t), F32)],
        compiler_params=_cparams(("arbitrary", "arbitrary")),
        name=name,
    )(*row_inputs, *const_inputs, x, modsel, g2.reshape(1, d), w_router_t)


def _ffn_kernel(idx_ref, h_hbm, wa_ref, wu_ref, wd_ref, gate_ref, o_ref,
                xf_scr, xb_scr, acc_scr, sem, *, rows_pad, per_step):
    e, m, f = pl.program_id(0), pl.program_id(1), pl.program_id(2)
    nm, nf = pl.num_programs(1), pl.num_programs(2)
    tm = xb_scr.shape[0]
    lin = e * nm + m
    slot = jnp.bitwise_and(lin, 1)
    last_tile = pl.num_programs(0) * nm - 1

    def row_copy(token, s, r):
        return pltpu.make_async_copy(h_hbm.at[pl.ds(token, 1), :], xf_scr.at[s, pl.ds(r, 1), :], sem.at[s])

    def wait_rows(s):
        pltpu.make_async_copy(h_hbm.at[pl.ds(0, rows_pad), :], xf_scr.at[s], sem.at[s]).wait()

    @pl.when(jnp.logical_and(lin == 0, f == 0))
    def _():
        def body(r, c):
            row_copy(idx_ref[r], 0, r).start()
            return c
        lax.fori_loop(0, rows_pad, body, 0)

    @pl.when(f == 0)
    def _():
        wait_rows(slot)
        xb_scr[...] = xf_scr[slot, 0:tm, :].astype(BF16)
        acc_scr[...] = jnp.zeros(acc_scr.shape, F32)

    base = (lin + 1) * rows_pad + f * per_step
    for k in range(per_step):
        row_copy(idx_ref[base + k], 1 - slot, f * per_step + k).start()

    x = xb_scr[...]
    a = jnp.dot(x, wa_ref[...].astype(BF16), preferred_element_type=F32)
    u = jnp.dot(x, wu_ref[...].astype(BF16), preferred_element_type=F32)
    hmid = (a * jax.nn.sigmoid(a) * u).astype(BF16)
    acc_scr[...] += jnp.dot(hmid, wd_ref[...].astype(BF16), preferred_element_type=F32)

    @pl.when(f == nf - 1)
    def _():
        o_ref[...] = (acc_scr[...] * gate_ref[...]).astype(o_ref.dtype)

    @pl.when(jnp.logical_and(lin == last_tile, f == nf - 1))
    def _():
        wait_rows(1 - slot)


def _expert_ffn(h_tokens, idx, w_up, w_down, layer, gate):
    ne, r = idx.shape
    d = h_tokens.shape[1]
    ff = w_down.shape[2]
    tm = _largest_divisor(r, 1040, 16)
    tf = _largest_divisor(ff, 256, 128)
    nm, nf = r // tm, ff // tf
    per_step = pl.cdiv(pl.cdiv(tm, nf), 8) * 8
    rows_pad = per_step * nf
    tiles = idx.reshape(ne * nm, tm)
    tiles = jnp.pad(tiles, ((0, 1), (0, rows_pad - tm)))
    kern = functools.partial(_ffn_kernel, rows_pad=rows_pad, per_step=per_step)
    return pl.pallas_call(
        kern,
        grid_spec=pltpu.PrefetchScalarGridSpec(
            num_scalar_prefetch=1,
            grid=(ne, nm, nf),
            in_specs=[pl.BlockSpec(memory_space=pl.ANY),
                      pl.BlockSpec((None, None, d, tf), lambda e, m, f, ix: (layer, e, 0, f)),
                      pl.BlockSpec((None, None, d, tf), lambda e, m, f, ix: (layer, e, 0, nf + f)),
                      pl.BlockSpec((None, None, tf, d), lambda e, m, f, ix: (layer, e, f, 0)),
                      pl.BlockSpec((None, tm, 1), lambda e, m, f, ix: (e, m, 0))],
            out_specs=pl.BlockSpec((None, tm, d), lambda e, m, f, ix: (e, m, 0)),
            scratch_shapes=[pltpu.VMEM((2, rows_pad, d), F32),
                            pltpu.VMEM((tm, d), BF16),
                            pltpu.VMEM((tm, d), F32),
                            pltpu.SemaphoreType.DMA((2,))]),
        out_shape=jax.ShapeDtypeStruct((ne, r, d), BF16),
        compiler_params=_cparams(("arbitrary", "arbitrary", "arbitrary")),
        name="expert_ffn",
    )(tiles.reshape(-1), h_tokens, w_up, w_up, w_down, gate)


def _combine_kernel(start_ref, rel_ref, x_ref, mod_ref, *refs):
    y_refs, o_ref = refs[:-1], refs[-1]
    tm = x_ref.shape[0]
    win = y_refs[0].shape[1]
    rel = rel_ref[...]
    lane = lax.broadcasted_iota(jnp.int32, (tm, win), 1)
    acc = jnp.zeros(x_ref.shape, F32)
    for e, y_ref in enumerate(y_refs):
        onehot = jnp.where(rel[:, e:e + 1] == lane, 1.0, 0.0).astype(BF16)
        acc = acc + jnp.dot(onehot, y_ref[0], preferred_element_type=F32)
    o_ref[...] = x_ref[...] + mod_ref[5:6, :] * acc


def _combine(x_mid, modsel, y, rel_t, start, n_ctx):
    b, t, d = x_mid.shape
    ne, r, _ = y.shape
    tm = COMBINE_TILE
    win = tm + ROW_ALIGN_BF16
    ctx_tiles = n_ctx // tm
    y_specs = [pl.BlockSpec((pl.Element(1), pl.Element(win), pl.Element(d)),
                            lambda b, i, st, e=e: (e, st[b, e, i] * ROW_ALIGN_BF16, 0))
               for e in range(ne)]
    return pl.pallas_call(
        _combine_kernel,
        grid_spec=pltpu.PrefetchScalarGridSpec(
            num_scalar_prefetch=1,
            grid=(b, t // tm),
            in_specs=[pl.BlockSpec((None, tm, ne), lambda b, i, st: (b, i, 0)),
                      pl.BlockSpec((None, tm, d), lambda b, i, st: (b, i, 0)),
                      pl.BlockSpec((None, None, 6, d),
                                   lambda b, i, st: (b, (i >= ctx_tiles).astype(jnp.int32), 0, 0))] + y_specs,
            out_specs=pl.BlockSpec((None, tm, d), lambda b, i, st: (b, i, 0))),
        out_shape=jax.ShapeDtypeStruct((b, t, d), F32),
        compiler_params=_cparams(("arbitrary", "arbitrary")),
        name="moe_combine",
    )(start, rel_t, x_mid, modsel, *([y] * ne))


def _route_segment(aff, cap):
    gate_u, idx_u = lax.top_k(aff, cap)
    kth = gate_u[..., -1:]
    gt = aff > kth
    eq = aff == kth
    need = cap - jnp.sum(gt, axis=-1, keepdims=True, dtype=jnp.int32)
    mask = gt | (eq & (jnp.cumsum(eq.astype(jnp.int32), axis=-1) <= need))
    m32 = mask.astype(jnp.int32)
    pos = jnp.cumsum(m32, axis=-1) - m32
    idx = jnp.sort(idx_u.astype(jnp.int32), axis=-1)
    gate = jnp.take_along_axis(aff, idx, axis=-1)
    return mask, pos, idx, gate


def _moe(x_mid, h2, logits_t, modsel, w_up, w_down, layer, n_ctx):
    b, t, d = x_mid.shape
    ne = logits_t.shape[1]
    tm = COMBINE_TILE
    win = tm + ROW_ALIGN_BF16
    assert n_ctx % tm == 0 and t % tm == 0
    aff = jax.nn.softmax(logits_t, axis=1)
    cap_l = EC_CAPACITY_FACTOR * (t - n_ctx) // ne
    cap_c = EC_CAPACITY_FACTOR * n_ctx // ne
    r = b * (cap_l + cap_c)
    assert r % ROW_ALIGN_BF16 == 0 and r >= win
    mask_l, pos_l, idx_l, gate_l = _route_segment(aff[:, :, n_ctx:], cap_l)
    mask_c, pos_c, idx_c, gate_c = _route_segment(aff[:, :, :n_ctx], cap_c)
    bi = jnp.arange(b, dtype=jnp.int32)[:, None, None]

    def per_expert(a):
        return a.transpose(1, 0, 2).reshape(ne, -1)

    idx = jnp.concatenate([per_expert(idx_l + bi * t + n_ctx), per_expert(idx_c + bi * t)], axis=1)
    gate = jnp.concatenate([per_expert(gate_l), per_expert(gate_c)], axis=1)
    y = _expert_ffn(h2.reshape(b * t, d), idx, w_up, w_down, layer, gate[..., None])

    row = jnp.concatenate([pos_c + b * cap_l + bi * cap_c, pos_l + bi * cap_l], axis=2)
    mask = jnp.concatenate([mask_c, mask_l], axis=2)
    start = jnp.minimum(row[:, :, ::tm] // ROW_ALIGN_BF16 * ROW_ALIGN_BF16, r - win)
    rel = jnp.where(mask, row - jnp.repeat(start, tm, axis=2), -1)
    return _combine(x_mid, modsel, y, rel.transpose(0, 2, 1), start // ROW_ALIGN_BF16, n_ctx)


def _norm_slab_kernel(x_ref, mod_ref, g_ref, h_ref):
    h = _norm_mod(x_ref[...], g_ref[...], mod_ref[0:1, :], mod_ref[1:2, :]).astype(h_ref.dtype)
    for o in range(h_ref.shape[0]):
        h_ref[o] = h[:, o * LANES:(o + 1) * LANES]


def _norm_slabs(x, modsel, g):
    b, t, d = x.shape
    tm = ROW_TILE
    n_slab = d // LANES
    return pl.pallas_call(
        _norm_slab_kernel,
        grid=(b, t // tm),
        in_specs=[pl.BlockSpec((None, tm, d), lambda b, i: (b, i, 0)),
                  _mod_spec(d),
                  pl.BlockSpec((1, d), lambda b, i: (0, 0))],
        out_specs=pl.BlockSpec((None, n_slab, tm, LANES), lambda b, i: (b, 0, i, 0)),
        out_shape=jax.ShapeDtypeStruct((b, n_slab, t, LANES), BF16),
        compiler_params=_cparams(("arbitrary", "arbitrary")),
        name="norm_mod",
    )(x, modsel, g.reshape(1, d))


def _s5_core_kernel(u_ref, win_ref, toep_ref, wout_ref, a_ref, y_ref, zh_scr, *, m_ctx):
    dr = pl.program_id(2)
    m = u_ref.shape[0]
    rc = _largest_divisor(m, 512, 16)
    half = zh_scr.shape[1] // 2
    for r0 in range(0, m, rc):
        zh_scr[r0:r0 + rc, :] = jnp.dot(u_ref[r0:r0 + rc, :], win_ref[...], preferred_element_type=F32)
    ar = a_ref[0:1, :]
    ai = a_ref[1:2, :]
    rid = lax.broadcasted_iota(jnp.int32, (8, half), 0)

    def visit(blk, carry, reverse):
        sr, si = carry
        base = pl.multiple_of(blk * 8, 8)
        z8 = zh_scr[pl.ds(base, 8), :]
        hr = jnp.zeros((8, half), F32)
        hi = jnp.zeros((8, half), F32)
        for r in (range(7, -1, -1) if reverse else range(8)):
            hr = jnp.where(rid == r, sr, hr)
            hi = jnp.where(rid == r, si, hi)
            zr = z8[r:r + 1, 0:half]
            zi = z8[r:r + 1, half:2 * half]
            sr, si = ar * sr - ai * si + zr, ar * si + ai * sr + zi
        zh_scr[pl.ds(base, 8), 0:half] = hr
        zh_scr[pl.ds(base, 8), half:2 * half] = hi
        return sr, si

    zero = (jnp.zeros((1, half), F32), jnp.zeros((1, half), F32))
    nb, nb_ctx = m // 8, m_ctx // 8

    @pl.when(dr == 0)
    def _():
        lax.fori_loop(0, nb, lambda s, cr: visit(s, cr, False), zero)

    @pl.when(dr == 1)
    def _():
        carry = lax.fori_loop(0, nb_ctx, lambda s, cr: visit(nb_ctx - 1 - s, cr, True), zero)
        lax.fori_loop(0, nb - nb_ctx, lambda s, cr: visit(nb - 1 - s, cr, True), carry)

    for r0 in range(0, m, rc):
        y = (jnp.dot(u_ref[r0:r0 + rc, :], toep_ref[...], preferred_element_type=F32)
             + jnp.dot(zh_scr[r0:r0 + rc, :].astype(BF16), wout_ref[...], preferred_element_type=F32))
        y_ref[r0:r0 + rc, :] = y.astype(y_ref.dtype)


def _s5_weights(lam_re, lam_im, log_dt, b_re, b_im, c_re, c_im):
    sub = S5_SUB
    gs = S5_GROUP_SIZE
    gps = LANES // gs
    outs = []
    for dr in range(2):
        lre = jnp.minimum(lam_re[dr].astype(F32), -S5_MIN_DECAY)
        lim = lam_im[dr].astype(F32)
        dt = jnp.exp(log_dt[dr].astype(F32))[:, None]
        mag = jnp.exp(lre * dt)
        ang = lim * dt
        lbr, lbi = mag * jnp.cos(ang), mag * jnp.sin(ang)
        den = lre * lre + lim * lim
        f_re = ((lbr - 1) * lre + lbi * lim) / den
        f_im = (lbi * lre - (lbr - 1) * lim) / den
        bre, bim = b_re[dr].astype(F32), b_im[dr].astype(F32)
        bbr = f_re[..., None] * bre - f_im[..., None] * bim
        bbi = f_re[..., None] * bim + f_im[..., None] * bre
        cr, ci = c_re[dr].astype(F32), c_im[dr].astype(F32)
        pr, pi = [jnp.ones_like(lbr)], [jnp.zeros_like(lbr)]
        for _ in range(sub):
            pr.append(pr[-1] * lbr - pi[-1] * lbi)
            pi.append(pr[-2] * lbi + pi[-1] * lbr)
        pw_r, pw_i = jnp.stack(pr), jnp.stack(pi)
        pb_r = pw_r[..., None] * bbr - pw_i[..., None] * bbi
        pb_i = pw_r[..., None] * bbi + pw_i[..., None] * bbr
        cp_r = cr[None] * pw_r[:, :, None, :] - ci[None] * pw_i[:, :, None, :]
        cp_i = cr[None] * pw_i[:, :, None, :] + ci[None] * pw_r[:, :, None, :]
        kk = (jnp.einsum('gip,tgpj->tgij', cr, pb_r[:sub], precision=HI)
              - jnp.einsum('gip,tgpj->tgij', ci, pb_i[:sub], precision=HI))
        s_idx = jnp.arange(sub)
        lag = (s_idx[None, :] - s_idx[:, None]) if dr == 0 else (s_idx[:, None] - s_idx[None, :])
        kt = kk[jnp.clip(lag, 0, sub - 1)]
        kt = jnp.where((lag >= 0)[:, :, None, None, None], kt, 0.0)
        g_n = kt.shape[2]
        n_slab = g_n // gps
        cw, sw = sub * LANES, gps * S5_STATE
        col_tgi = jnp.arange(cw)
        spread_ti = (jnp.arange(sub * gs)[:, None] == (col_tgi // LANES * gs + col_tgi % gs)[None, :]).astype(F32)
        spread_p = (jnp.arange(S5_STATE)[:, None] == (jnp.arange(sw) % S5_STATE)[None, :]).astype(F32)
        grp_tgi = col_tgi // gs % gps
        grp_gp = jnp.arange(sw) // S5_STATE

        def spread(table, spread_mat, row_grp, col_grp):
            rows = table.shape[1]
            full = jnp.dot(table.reshape(n_slab * rows, -1), spread_mat, precision=HI)
            return full.reshape(n_slab, rows, -1) * (row_grp[:, None] == col_grp[None, :]).astype(F32)

        kt6 = kt.reshape(sub, sub, n_slab, gps, gs, gs).transpose(2, 0, 3, 5, 1, 4)
        toep = spread(kt6.reshape(n_slab, cw, sub * gs), spread_ti, grp_tgi, grp_tgi)
        e_in = (sub - 1 - s_idx) if dr == 0 else s_idx

        def in_map(pb):
            t5 = pb[e_in].reshape(sub, n_slab, gps, S5_STATE, gs).transpose(1, 0, 2, 4, 3)
            return spread(t5.reshape(n_slab, cw, S5_STATE), spread_p, grp_tgi, grp_gp)

        win = jnp.concatenate([in_map(pb_r), in_map(pb_i)], axis=2)
        e_out = (s_idx + 1) if dr == 0 else (sub - s_idx)

        def out_map(cp):
            t5 = cp[e_out].reshape(sub, n_slab, gps, gs, S5_STATE).transpose(1, 2, 4, 0, 3)
            return spread(t5.reshape(n_slab, sw, sub * gs), spread_ti, grp_gp, grp_tgi)

        wout = jnp.concatenate([out_map(cp_r), -out_map(cp_i)], axis=1)
        a_sub = jnp.stack([pw_r[sub].reshape(n_slab, gps * S5_STATE),
                           pw_i[sub].reshape(n_slab, gps * S5_STATE)], axis=1)
        outs.append((win, toep, wout, a_sub))

    stack = lambda k, dt: jnp.stack([o[k] for o in outs], axis=1).astype(dt)
    return stack(0, BF16), stack(1, BF16), stack(2, BF16), stack(3, F32)


def _s5_core(hs, weights, n_ctx):
    b, n_slab, t, _ = hs.shape
    sub = S5_SUB
    win, toep, wout, a_sub = weights
    m = t // sub
    m_ctx = n_ctx // sub
    assert m % 8 == 0 and m_ctx % 8 == 0
    cw = sub * LANES
    sw = win.shape[-1]
    u = hs.reshape(b, n_slab, m, cw)
    kern = functools.partial(_s5_core_kernel, m_ctx=m_ctx)
    y = pl.pallas_call(
        kern,
        grid=(n_slab, b, 2),
        in_specs=[pl.BlockSpec((None, None, m, cw), lambda o, b, d: (b, o, 0, 0)),
                  pl.BlockSpec((None, None, cw, sw), lambda o, b, d: (o, d, 0, 0)),
                  pl.BlockSpec((None, None, cw, cw), lambda o, b, d: (o, d, 0, 0)),
                  pl.BlockSpec((None, None, sw, cw), lambda o, b, d: (o, d, 0, 0)),
                  pl.BlockSpec((None, None, 2, sw // 2), lambda o, b, d: (o, d, 0, 0))],
        out_specs=pl.BlockSpec((None, None, None, m, cw), lambda o, b, d: (d, b, o, 0, 0)),
        out_shape=jax.ShapeDtypeStruct((2, b, n_slab, m, cw), BF16),
        scratch_shapes=[pltpu.VMEM((m, sw), F32)],
        compiler_params=_cparams(("arbitrary", "arbitrary", "arbitrary")),
        name="s5_core",
    )(u, win, toep, wout, a_sub)
    return y.reshape(2, b, n_slab, t, LANES)


def _ret_in_kernel(x_ref, mod_ref, g_ref, w_ref, cos_ref, sin_ref, z_ref, *, n_rope_cols):
    h = _norm_mod(x_ref[...], g_ref[...], mod_ref[0:1, :], mod_ref[1:2, :]).astype(BF16)
    cos, sin = cos_ref[...], sin_ref[...]
    half = cos.shape[1]
    tn = 4 * half
    for n0 in range(0, w_ref.shape[1], tn):
        z = jnp.dot(h, w_ref[:, n0:n0 + tn], preferred_element_type=F32)
        if n0 < n_rope_cols:
            for c0 in range(0, tn, 2 * half):
                x1 = z[:, c0:c0 + half]
                x2 = z[:, c0 + half:c0 + 2 * half]
                z_ref[:, n0 + c0:n0 + c0 + half] = (x1 * cos - x2 * sin).astype(z_ref.dtype)
                z_ref[:, n0 + c0 + half:n0 + c0 + 2 * half] = (x2 * cos + x1 * sin).astype(z_ref.dtype)
        else:
            z_ref[:, n0:n0 + tn] = z.astype(z_ref.dtype)


def _ret_in(x, modsel, g, w, cos, sin, n_rope_cols):
    b, t, d = x.shape
    n_out = w.shape[1]
    tm = ROW_TILE
    half = cos.shape[1]
    assert n_out % (4 * half) == 0 and n_rope_cols % (4 * half) == 0
    kern = functools.partial(_ret_in_kernel, n_rope_cols=n_rope_cols)
    return pl.pallas_call(
        kern,
        grid=(b, t // tm),
        in_specs=[pl.BlockSpec((None, tm, d), lambda b, i: (b, i, 0)),
                  _mod_spec(d),
                  pl.BlockSpec((1, d), lambda b, i: (0, 0)),
                  pl.BlockSpec((d, n_out), lambda b, i: (0, 0), pipeline_mode=pl.Buffered(1)),
                  pl.BlockSpec((tm, half), lambda b, i: (i, 0)),
                  pl.BlockSpec((tm, half), lambda b, i: (i, 0))],
        out_specs=pl.BlockSpec((None, tm, n_out), lambda b, i: (b, i, 0)),
        out_shape=jax.ShapeDtypeStruct((b, t, n_out), BF16),
        compiler_params=_cparams(("arbitrary", "arbitrary")),
        name="ret_in",
    )(x, modsel, g.reshape(1, d), w, cos, sin)


def _ret_core_kernel(lg_ref, q_ref, k_ref, v_ref, *rest, rev, qk_scale):
    if rev:
        of_ref, g_ref, o_ref, s_scr = rest
    else:
        o_ref, s_scr = rest
    hg = pl.program_id(1)
    c = pl.program_id(2)
    tc = q_ref.shape[0]
    n_hp, dqk, dv = s_scr.shape

    @pl.when(c == 0)
    def _():
        s_scr[...] = jnp.zeros(s_scr.shape, F32)

    row = lax.broadcasted_iota(jnp.int32, (tc, tc), 0)
    col = lax.broadcasted_iota(jnp.int32, (tc, tc), 1)
    diff = (col - row) if rev else (row - col)
    pos = lax.broadcasted_iota(jnp.int32, (tc, 1), 0).astype(F32)
    for hp in range(n_hp):
        lg = jnp.full((1, 1), lg_ref[hg * n_hp + hp], F32)
        inner = jnp.where(diff >= 0, jnp.exp(lg * jnp.maximum(diff, 0).astype(F32)), 0.0) * qk_scale
        if rev:
            q_dec = jnp.exp(lg * (tc - pos))
            k_dec = jnp.exp(lg * pos) * qk_scale
        else:
            q_dec = jnp.exp(lg * (pos + 1.0))
            k_dec = jnp.exp(lg * (tc - 1.0 - pos)) * qk_scale
        q = q_ref[:, hp * dqk:(hp + 1) * dqk]
        k = k_ref[:, hp * dqk:(hp + 1) * dqk]
        v = v_ref[:, hp * dv:(hp + 1) * dv]
        s = lax.dot_general(q, k, (((1,), (1,)), ((), ())), preferred_element_type=F32) * inner
        state = s_scr[hp]
        o = (jnp.dot(s.astype(BF16), v, preferred_element_type=F32)
             + jnp.dot((q.astype(F32) * q_dec).astype(BF16), state.astype(BF16), preferred_element_type=F32))
        kd_t = (k.astype(F32) * k_dec).T.astype(BF16)
        s_scr[hp] = state * jnp.exp(lg * tc) + jnp.dot(kd_t, v, preferred_element_type=F32)
        if rev:
            tot = of_ref[:, hp * dv:(hp + 1) * dv] + o
            nrm = tot * lax.rsqrt(jnp.mean(tot * tot, axis=-1, keepdims=True) + NORM_EPS)
            gg = g_ref[:, hp * dv:(hp + 1) * dv].astype(F32)
            o_ref[:, hp * dv:(hp + 1) * dv] = (gg * jax.nn.sigmoid(gg) * nrm).astype(o_ref.dtype)
        else:
            o_ref[:, hp * dv:(hp + 1) * dv] = o


def _ret_core(z, log_gamma, o_fwd, rev):
    b, t, _ = z.shape
    nh = RET_HEADS
    dqk = z.shape[2] // (6 * nh)
    dv = 2 * dqk
    tc = ROW_TILE
    nc = t // tc

    if rev:
        def cmap(c):
            return jnp.where(c == 0, 0, nc - c)
    else:
        def cmap(c):
            return c

    hp = RET_HEADS_PER_STEP
    ng = nh // hp
    kern = functools.partial(_ret_core_kernel, rev=rev, qk_scale=dqk ** -0.5)
    in_specs = [pl.BlockSpec((None, tc, hp * dqk), lambda b, h, c, lg: (b, cmap(c), h)),
                pl.BlockSpec((None, tc, hp * dqk), lambda b, h, c, lg: (b, cmap(c), ng + h)),
                pl.BlockSpec((None, tc, hp * dv), lambda b, h, c, lg: (b, cmap(c), ng + h))]
    args = [z, z, z]
    if rev:
        in_specs += [pl.BlockSpec((None, tc, hp * dv), lambda b, h, c, lg: (b, cmap(c), h)),
                     pl.BlockSpec((None, tc, hp * dv), lambda b, h, c, lg: (b, cmap(c), 2 * ng + h))]
        args += [o_fwd, z]
    out_dtype = BF16 if rev else F32
    return pl.pallas_call(
        kern,
        grid_spec=pltpu.PrefetchScalarGridSpec(
            num_scalar_prefetch=1,
            grid=(b, ng, nc),
            in_specs=in_specs,
            out_specs=pl.BlockSpec((None, tc, hp * dv), lambda b, h, c, lg: (b, cmap(c), h)),
            scratch_shapes=[pltpu.VMEM((hp, dqk, dv), F32)]),
        out_shape=jax.ShapeDtypeStruct((b, t, nh * dv), out_dtype),
        compiler_params=_cparams(("arbitrary", "arbitrary", "arbitrary")),
        name="ret_core_bwd" if rev else "ret_core_fwd",
    )(log_gamma, *args)


def _final_kernel(x_ref, g_ref, o_ref):
    x = x_ref[...]
    o_ref[...] = x * lax.rsqrt(jnp.mean(x * x, axis=-1, keepdims=True) + NORM_EPS) * g_ref[...]


def _final_norm(x, g, n_ctx):
    b, t, d = x.shape
    tm = ROW_TILE
    skip = n_ctx // tm
    return pl.pallas_call(
        _final_kernel,
        grid=(b, (t - n_ctx) // tm),
        in_specs=[pl.BlockSpec((None, tm, d), lambda b, i: (b, i + skip, 0)),
                  pl.BlockSpec((1, d), lambda b, i: (0, 0))],
        out_specs=pl.BlockSpec((None, tm, d), lambda b, i: (b, i, 0)),
        out_shape=jax.ShapeDtypeStruct((b, t - n_ctx, d), F32),
        compiler_params=_cparams(("arbitrary", "arbitrary")),
        name="final_norm",
    )(x, g.reshape(1, d))


def _rope_tables(n_ctx, n_lat, head_dim):
    rows = n_lat // GRID_W
    row = jnp.repeat(jnp.arange(rows, dtype=F32), GRID_W)
    col = jnp.tile(jnp.arange(GRID_W, dtype=F32), rows)
    n_freq = head_dim // 4
    inv_freq = ROPE_THETA ** (-jnp.arange(n_freq, dtype=F32) / n_freq)
    ang = jnp.concatenate([row[:, None] * inv_freq, col[:, None] * inv_freq], axis=-1)
    cos = jnp.concatenate([jnp.ones((n_ctx, head_dim // 2), F32), jnp.cos(ang)], axis=0)
    sin = jnp.concatenate([jnp.zeros((n_ctx, head_dim // 2), F32), jnp.sin(ang)], axis=0)
    return cos, sin


def kernel(x, c, ctx, c_ctx, mod_w, mod_b, norm_g, attn_w_qkv, attn_w_o, attn_q_gain, attn_k_gain,
           s5_lambda_re, s5_lambda_im, s5_log_dt, s5_b_re, s5_b_im, s5_c_re, s5_c_im, s5_d, s5_w_glu, s5_b_glu,
           ret_w_qkvg, ret_w_o, ret_log_decay, moe_w_router, moe_w_up, moe_w_down, final_g):
    bsz, n_lat, d = x.shape
    n_ctx = ctx.shape[1]
    depth = mod_w.shape[0]
    assert n_ctx == ROW_TILE and n_lat % ROW_TILE == 0 and bsz <= 7

    xs = jnp.concatenate([ctx, x], axis=1).astype(F32)
    cvec = jnp.zeros((8, d), F32).at[:bsz].set(c).at[bsz].set(c_ctx)
    mods = _modulation(cvec, mod_w, mod_b)

    cos_a, sin_a = _rope_tables(n_ctx, n_lat, ATTN_HEAD_DIM)
    cos2 = jnp.concatenate([cos_a, cos_a], axis=1)
    sin2 = jnp.concatenate([-sin_a, sin_a], axis=1)
    cos_r, sin_r = _rope_tables(n_ctx, n_lat, d // RET_HEADS)

    tm = ROW_TILE

    def row_spec(width):
        return pl.BlockSpec((None, tm, width), lambda b, i: (b, i, 0))

    for i in range(depth):
        kind, j = i % N_MIXERS, i // N_MIXERS
        m6 = mods[i].reshape(8, 6, d)
        modsel = jnp.stack([jnp.broadcast_to(m6[bsz], (bsz, 6, d)), m6[:bsz]], axis=1)
        wr_t = moe_w_router[i].T.astype(F32)
        if kind == 0:
            q, k, v = _attn_in(xs, modsel, norm_g[i, 0], attn_w_qkv[j].astype(BF16),
                               attn_q_gain[j], attn_k_gain[j], cos2, sin2)
            o = _flash(q, k, v, n_ctx)
            x_mid, h2, lg = _post_call(_post_plain_kernel, "attn_out", [o], [row_spec(o.shape[-1])],
                                       [attn_w_o[j].astype(BF16)], xs, modsel, norm_g[i, 1], wr_t)
        elif kind == 1:
            hs = _norm_slabs(xs, modsel, norm_g[i, 0])
            w5 = _s5_weights(s5_lambda_re[j], s5_lambda_im[j], s5_log_dt[j], s5_b_re[j], s5_b_im[j],
                             s5_c_re[j], s5_c_im[j])
            y = _s5_core(hs, w5, n_ctx)
            n_slab = d // LANES
            slab_specs = [pl.BlockSpec((2, None, n_slab, tm, LANES), lambda b, i: (0, b, 0, i, 0)),
                          pl.BlockSpec((None, n_slab, tm, LANES), lambda b, i: (b, 0, i, 0))]
            x_mid, h2, lg = _post_call(_post_glu_kernel, "s5_out", [y, hs], slab_specs,
                                       [s5_d[j].reshape(1, d).astype(F32), s5_w_glu[j].astype(BF16),
                                        s5_b_glu[j].reshape(1, -1).astype(F32)],
                                       xs, modsel, norm_g[i, 1], wr_t)
        else:
            z = _ret_in(xs, modsel, norm_g[i, 0], ret_w_qkvg[j].astype(BF16), cos_r, sin_r, 2 * d)
            o_f = None
            for dr in range(2):
                log_gamma = -jnp.abs(ret_log_decay[j, dr].astype(F32))
                o_f = _ret_core(z, log_gamma, o_f, rev=(dr == 1))
            x_mid, h2, lg = _post_call(_post_plain_kernel, "ret_out", [o_f], [row_spec(o_f.shape[-1])],
                                       [ret_w_o[j].astype(BF16)], xs, modsel, norm_g[i, 1], wr_t)
        xs = _moe(x_mid, h2, lg, modsel, moe_w_up, moe_w_down, i, n_ctx)

    return _final_norm(xs, final_g, n_ctx).astype(x.dtype)
```

```python
import functools
import math

import jax
import jax.numpy as jnp
from jax import lax
from jax.experimental import pallas as pl
from jax.experimental.pallas import tpu as pltpu

F32 = jnp.float32
BF16 = jnp.bfloat16

GRID_W = 64
N_MIXERS = 3
NORM_EPS = 1e-6
ROPE_THETA = 10000.0
ATTN_HEAD_DIM = 128
ATTN_GROUP = 4
S5_GROUP_SIZE = 16
S5_STATE = 64
S5_MIN_DECAY = 1e-4
RET_HEADS = 4
N_EXPERTS = 16
EC_CAPACITY_FACTOR = 2

LANES = 128
ROW_ALIGN_BF16 = 16
ROW_TILE = 256
FLASH_KEY_TILE = 3328
FLASH_CHUNK = 256
RET_HEADS_PER_STEP = 4
POST_PARTS = 2
COMBINE_TILE = 128
S5_SUB = 8
VMEM_LIMIT = 48 * 1024 * 1024
HI = lax.Precision.HIGHEST


def _cparams(sem):
    return pltpu.CompilerParams(dimension_semantics=sem, vmem_limit_bytes=VMEM_LIMIT)


def _norm_mod(x, g, shift, scale):
    ms = jnp.mean(x * x, axis=-1, keepdims=True)
    return (x * lax.rsqrt(ms + NORM_EPS) * g) * (1.0 + scale) + shift


def _largest_divisor(n, cap, mult):
    best = None
    for t in range(mult, min(n, cap) + 1, mult):
        if n % t == 0:
            best = t
    assert best is not None, (n, cap, mult)
    return best


def _mod_kernel(c_ref, w_ref, b_ref, o_ref):
    c = c_ref[...]
    s = c * jax.nn.sigmoid(c)
    o_ref[...] = jnp.dot(s, w_ref[...], precision=HI, preferred_element_type=F32) + b_ref[...]


def _modulation(cvec, mod_w, mod_b):
    depth, d, n = mod_w.shape
    tn = _largest_divisor(n, 1536, 128)
    return pl.pallas_call(
        _mod_kernel,
        grid=(depth, n // tn),
        in_specs=[pl.BlockSpec((8, d), lambda l, j: (0, 0)),
                  pl.BlockSpec((None, d, tn), lambda l, j: (l, 0, j)),
                  pl.BlockSpec((None, 1, tn), lambda l, j: (l, 0, j))],
        out_specs=pl.BlockSpec((None, 8, tn), lambda l, j: (l, 0, j)),
        out_shape=jax.ShapeDtypeStruct((depth, 8, n), F32),
        compiler_params=_cparams(("arbitrary", "arbitrary")),
        name="modulation",
    )(cvec, mod_w, mod_b.reshape(depth, 1, n))


def _mod_spec(d):
    return pl.BlockSpec((None, None, 6, d), lambda b, i, *_: (b, jnp.minimum(i, 1), 0, 0))


def _attn_in_kernel(x_ref, mod_ref, g_ref, w_ref, qg_ref, kg_ref, cos_ref, sin_ref,
                    q_ref, k_ref, v_ref, *, n_q, n_kv):
    hd = ATTN_HEAD_DIM
    h = _norm_mod(x_ref[...], g_ref[...], mod_ref[0:1, :], mod_ref[1:2, :]).astype(BF16)
    qkv = jnp.dot(h, w_ref[...], preferred_element_type=F32)
    cos, sin = cos_ref[...], sin_ref[...]

    def norm_rope(t, gain):
        t = t * lax.rsqrt(jnp.mean(t * t, axis=-1, keepdims=True) + NORM_EPS) * gain
        return t * cos + pltpu.roll(t, hd // 2, axis=1) * sin

    scale = hd ** -0.5 * math.log2(math.e)
    for j in range(n_q):
        q_ref[j] = (norm_rope(qkv[:, j * hd:(j + 1) * hd], qg_ref[...]) * scale).astype(BF16)
    lane = lax.broadcasted_iota(jnp.int32, (qkv.shape[0], hd), 1)
    ones_col = jnp.where(lane == 0, 1.0, 0.0).astype(BF16)
    for j in range(n_kv):
        c0 = (n_q + j) * hd
        k_ref[j] = norm_rope(qkv[:, c0:c0 + hd], kg_ref[...]).astype(BF16)
        c1 = (n_q + n_kv + j) * hd
        v_ref[j, :, 0:hd] = qkv[:, c1:c1 + hd].astype(BF16)
        v_ref[j, :, hd:2 * hd] = ones_col


def _attn_in(x, modsel, g, w_qkv, q_gain, k_gain, cos2, sin2):
    b, t, d = x.shape
    hd = ATTN_HEAD_DIM
    n_tot = w_qkv.shape[1] // hd
    n_q = d // hd
    n_kv = (n_tot - n_q) // 2
    tm = ROW_TILE
    kern = functools.partial(_attn_in_kernel, n_q=n_q, n_kv=n_kv)
    return pl.pallas_call(
        kern,
        grid=(b, t // tm),
        in_specs=[pl.BlockSpec((None, tm, d), lambda b, i: (b, i, 0)),
                  _mod_spec(d),
                  pl.BlockSpec((1, d), lambda b, i: (0, 0)),
                  pl.BlockSpec(w_qkv.shape, lambda b, i: (0, 0)),
                  pl.BlockSpec((1, hd), lambda b, i: (0, 0)),
                  pl.BlockSpec((1, hd), lambda b, i: (0, 0)),
                  pl.BlockSpec((tm, hd), lambda b, i: (i, 0)),
                  pl.BlockSpec((tm, hd), lambda b, i: (i, 0))],
        out_specs=[pl.BlockSpec((None, n_q, tm, hd), lambda b, i: (b, 0, i, 0)),
                   pl.BlockSpec((None, n_kv, tm, hd), lambda b, i: (b, 0, i, 0)),
                   pl.BlockSpec((None, n_kv, tm, 2 * hd), lambda b, i: (b, 0, i, 0))],
        out_shape=[jax.ShapeDtypeStruct((b, n_q, t, hd), BF16),
                   jax.ShapeDtypeStruct((b, n_kv, t, hd), BF16),
                   jax.ShapeDtypeStruct((b, n_kv, t, 2 * hd), BF16)],
        compiler_params=_cparams(("arbitrary", "arbitrary")),
        name="attn_in",
    )(x, modsel, g.reshape(1, d), w_qkv, q_gain.reshape(1, hd), k_gain.reshape(1, hd), cos2, sin2)


def _flash_kernel(q_ref, k_ref, v_ref, o_ref, s0_scr, s1_scr, m_scr, acc_scr, *, tq, n_ctx, n_key_tiles):
    hd = ATTN_HEAD_DIM
    qi = pl.program_id(2)
    j = pl.program_id(3)
    nk = n_key_tiles

    @pl.when(j == 0)
    def _():
        m_scr[...] = jnp.full(m_scr.shape, -1e30, F32)
        acc_scr[...] = jnp.zeros(acc_scr.shape, F32)

    def softmax_update(s_ref, v_ref_, width):
        m_prev = m_scr[...]
        m_new = jnp.maximum(m_prev, jnp.max(s_ref[:, 0:width], axis=1, keepdims=True))
        alpha = jnp.exp2(m_prev - m_new)
        pv = None
        for c0 in range(0, width, FLASH_CHUNK):
            c1 = min(c0 + FLASH_CHUNK, width)
            p = jnp.exp2((s_ref[:, c0:c1] - m_new).astype(BF16))
            part = jnp.dot(p, v_ref_[c0:c1, :], preferred_element_type=F32)
            pv = part if pv is None else pv + part
        acc_scr[...] = alpha * acc_scr[...] + pv
        m_scr[...] = m_new

    def scores(k):
        q = q_ref[...].reshape(ATTN_GROUP * tq, hd)
        return lax.dot_general(q, k, (((1,), (1,)), ((), ())), preferred_element_type=F32)

    tk = k_ref.shape[0]

    def pipelined(s_new, s_old):
        s_new[...] = scores(k_ref[...])
        softmax_update(s_old, v_ref, tk)

    latent = qi > 0
    odd = jnp.bitwise_and(j, 1)
    middle = jnp.logical_and(latent, jnp.logical_and(j > 0, j < nk))

    @pl.when(jnp.logical_and(latent, j == 0))
    def _():
        s0_scr[...] = scores(k_ref[...])

    @pl.when(jnp.logical_and(middle, odd == 0))
    def _():
        pipelined(s0_scr, s1_scr)

    @pl.when(jnp.logical_and(middle, odd == 1))
    def _():
        pipelined(s1_scr, s0_scr)

    @pl.when(jnp.logical_and(latent, j == nk))
    def _():
        softmax_update(s1_scr if (nk - 1) % 2 else s0_scr, v_ref, tk)

    @pl.when(jnp.logical_and(qi == 0, j == 0))
    def _():
        s0_scr[:, 0:n_ctx] = scores(k_ref[0:n_ctx, :])
        softmax_update(s0_scr, v_ref, n_ctx)

    @pl.when(j == nk)
    def _():
        o = acc_scr[:, 0:hd] / acc_scr[:, hd:hd + 1]
        for g in range(ATTN_GROUP):
            o_ref[:, g * hd:(g + 1) * hd] = o[g * tq:(g + 1) * tq, :].astype(o_ref.dtype)


def _flash(q, k, v, n_ctx):
    b, n_q, t, hd = q.shape
    n_kv = k.shape[1]
    tq = ROW_TILE
    assert n_ctx == tq and n_q == n_kv * ATTN_GROUP
    tk = _largest_divisor(t, FLASH_KEY_TILE, 256)
    nk = t // tk
    gw = ATTN_GROUP * hd
    rows = ATTN_GROUP * tq
    kern = functools.partial(_flash_kernel, tq=tq, n_ctx=n_ctx, n_key_tiles=nk)

    def k_map(b, h, i, j):
        return (b, h, jnp.where(i == 0, 0, jnp.minimum(j, nk - 1)), 0)

    def v_map(b, h, i, j):
        return (b, h, jnp.where(i == 0, 0, jnp.maximum(j - 1, 0)), 0)

    return pl.pallas_call(
        kern,
        grid=(b, n_kv, t // tq, nk + 1),
        in_specs=[pl.BlockSpec((None, ATTN_GROUP, tq, hd), lambda b, h, i, j: (b, h, i, 0)),
                  pl.BlockSpec((None, None, tk, hd), k_map),
                  pl.BlockSpec((None, None, tk, 2 * hd), v_map)],
        out_specs=pl.BlockSpec((None, tq, gw), lambda b, h, i, j: (b, i, h)),
        out_shape=jax.ShapeDtypeStruct((b, t, n_q * hd), BF16),
        scratch_shapes=[pltpu.VMEM((rows, tk), F32),
                        pltpu.VMEM((rows, tk), F32),
                        pltpu.VMEM((rows, 1), F32),
                        pltpu.VMEM((rows, 2 * hd), F32)],
        compiler_params=_cparams(("arbitrary", "arbitrary", "arbitrary", "arbitrary")),
        name="flash_attn",
    )(q, k, v)


def _post_tail(rows, x_new, mod_ref, g2_ref, wr_ref, x_out, h_out, lg_out):
    x_out[rows, :] = x_new
    h2 = _norm_mod(x_new, g2_ref[...], mod_ref[3:4, :], mod_ref[4:5, :])
    h_out[rows, :] = h2
    lg_out[:, rows] = lax.dot_general(wr_ref[...], h2, (((1,), (1,)), ((), ())),
                                      precision=HI, preferred_element_type=F32)


def _row_parts(tm):
    part = tm // POST_PARTS
    return [slice(p * part, (p + 1) * part) for p in range(POST_PARTS)]


def _post_plain_kernel(o_ref, w_ref, x_ref, mod_ref, g2_ref, wr_ref, x_out, h_out, lg_out):
    for rows in _row_parts(x_ref.shape[0]):
        y = jnp.dot(o_ref[rows, :], w_ref[...], preferred_element_type=F32)
        _post_tail(rows, x_ref[rows, :] + mod_ref[2:3, :] * y, mod_ref, g2_ref, wr_ref, x_out, h_out, lg_out)


def _post_glu_kernel(y_ref, hs_ref, dsk_ref, w_ref, b_ref, x_ref, mod_ref, g2_ref, wr_ref,
                     x_out, h_out, lg_out):
    d = x_ref.shape[-1]
    for rows in _row_parts(x_ref.shape[0]):
        slabs = [hs_ref[o, rows, :].astype(F32) for o in range(hs_ref.shape[0])]
        ys = [y_ref[0, o, rows, :].astype(F32) + y_ref[1, o, rows, :].astype(F32) for o in range(hs_ref.shape[0])]
        yt = dsk_ref[...] * jnp.concatenate(slabs, axis=1) + jnp.concatenate(ys, axis=1)
        z = jnp.dot(jax.nn.gelu(yt).astype(BF16), w_ref[...], preferred_element_type=F32) + b_ref[...]
        y = z[:, :d] * jax.nn.sigmoid(z[:, d:])
        _post_tail(rows, x_ref[rows, :] + mod_ref[2:3, :] * y, mod_ref, g2_ref, wr_ref, x_out, h_out, lg_out)


def _post_call(kern, name, row_inputs, row_specs, const_inputs, x, modsel, g2, w_router_t):
    b, t, d = x.shape
    tm = ROW_TILE
    ne = w_router_t.shape[0]
    const_specs = [pl.BlockSpec(a.shape, lambda b, i: (0, 0)) for a in const_inputs]
    return pl.pallas_call(
        kern,
        grid=(b, t // tm),
        in_specs=row_specs + const_specs + [
            pl.BlockSpec((None, tm, d), lambda b, i: (b, i, 0)),
            _mod_spec(d),
            pl.BlockSpec((1, d), lambda b, i: (0, 0)),
            pl.BlockSpec((ne, d), lambda b, i: (0, 0))],
        out_specs=[pl.BlockSpec((None, tm, d), lambda b, i: (b, i, 0)),
                   pl.BlockSpec((None, tm, d), lambda b, i: (b, i, 0)),
                   pl.BlockSpec((None, ne, tm), lambda b, i: (b, 0, i))],
        out_shape=[jax.ShapeDtypeStruct((b, t, d), F32),
                   jax.ShapeDtypeStruct((b, t, d), F32),
                   jax.ShapeDtypeStruct((b, ne, t), F32)],
        compiler_params=_cparams(("arbitrary", "arbitrary")),
        name=name,
    )(*row_inputs, *const_inputs, x, modsel, g2.reshape(1, d), w_router_t)


def _ffn_kernel(idx_ref, h_hbm, wa_ref, wu_ref, wd_ref, gate_ref, o_ref,
                xf_scr, xb_scr, acc_scr, sem, *, rows_pad, per_step):
    e, m, f = pl.program_id(0), pl.program_id(1), pl.program_id(2)
    nm, nf = pl.num_programs(1), pl.num_programs(2)
    tm = xb_scr.shape[0]
    lin = e * nm + m
    slot = jnp.bitwise_and(lin, 1)
    last_tile = pl.num_programs(0) * nm - 1

    def row_copy(token, s, r):
        return pltpu.make_async_copy(h_hbm.at[pl.ds(token, 1), :], xf_scr.at[s, pl.ds(r, 1), :], sem.at[s])

    def wait_rows(s):
        pltpu.make_async_copy(h_hbm.at[pl.ds(0, rows_pad), :], xf_scr.at[s], sem.at[s]).wait()

    @pl.when(jnp.logical_and(lin == 0, f == 0))
    def _():
        def body(r, c):
            row_copy(idx_ref[r], 0, r).start()
            return c
        lax.fori_loop(0, rows_pad, body, 0)

    @pl.when(f == 0)
    def _():
        wait_rows(slot)
        xb_scr[...] = xf_scr[slot, 0:tm, :].astype(BF16)
        acc_scr[...] = jnp.zeros(acc_scr.shape, F32)

    base = (lin + 1) * rows_pad + f * per_step
    for k in range(per_step):
        row_copy(idx_ref[base + k], 1 - slot, f * per_step + k).start(priority=k % 2)

    x = xb_scr[...]
    a = jnp.dot(x, wa_ref[...].astype(BF16), preferred_element_type=F32)
    u = jnp.dot(x, wu_ref[...].astype(BF16), preferred_element_type=F32)
    hmid = (a * jax.nn.sigmoid(a) * u).astype(BF16)
    acc_scr[...] += jnp.dot(hmid, wd_ref[...].astype(BF16), preferred_element_type=F32)

    @pl.when(f == nf - 1)
    def _():
        o_ref[...] = (acc_scr[...] * gate_ref[...]).astype(o_ref.dtype)

    @pl.when(jnp.logical_and(lin == last_tile, f == nf - 1))
    def _():
        wait_rows(1 - slot)


def _expert_ffn(h_tokens, idx, w_up, w_down, layer, gate):
    ne, r = idx.shape
    d = h_tokens.shape[1]
    ff = w_down.shape[2]
    tm = _largest_divisor(r, 1040, 16)
    tf = _largest_divisor(ff, 256, 128)
    nm, nf = r // tm, ff // tf
    per_step = pl.cdiv(pl.cdiv(tm, nf), 8) * 8
    rows_pad = per_step * nf
    tiles = idx.reshape(ne * nm, tm)
    tiles = jnp.pad(tiles, ((0, 1), (0, rows_pad - tm)))
    kern = functools.partial(_ffn_kernel, rows_pad=rows_pad, per_step=per_step)
    return pl.pallas_call(
        kern,
        grid_spec=pltpu.PrefetchScalarGridSpec(
            num_scalar_prefetch=1,
            grid=(ne, nm, nf),
            in_specs=[pl.BlockSpec(memory_space=pl.ANY),
                      pl.BlockSpec((None, None, d, tf), lambda e, m, f, ix: (layer, e, 0, f)),
                      pl.BlockSpec((None, None, d, tf), lambda e, m, f, ix: (layer, e, 0, nf + f)),
                      pl.BlockSpec((None, None, tf, d), lambda e, m, f, ix: (layer, e, f, 0)),
                      pl.BlockSpec((None, tm, 1), lambda e, m, f, ix: (e, m, 0))],
            out_specs=pl.BlockSpec((None, tm, d), lambda e, m, f, ix: (e, m, 0)),
            scratch_shapes=[pltpu.VMEM((2, rows_pad, d), F32),
                            pltpu.VMEM((tm, d), BF16),
                            pltpu.VMEM((tm, d), F32),
                            pltpu.SemaphoreType.DMA((2,))]),
        out_shape=jax.ShapeDtypeStruct((ne, r, d), BF16),
        compiler_params=_cparams(("arbitrary", "arbitrary", "arbitrary")),
        name="expert_ffn",
    )(tiles.reshape(-1), h_tokens, w_up, w_up, w_down, gate)


def _combine_kernel(start_ref, rel_ref, x_ref, mod_ref, *refs):
    y_refs, o_ref = refs[:-1], refs[-1]
    tm = x_ref.shape[0]
    win = y_refs[0].shape[1]
    rel = rel_ref[...]
    lane = lax.broadcasted_iota(jnp.int32, (tm, win), 1)
    acc = jnp.zeros(x_ref.shape, F32)
    for e, y_ref in enumerate(y_refs):
        onehot = jnp.where(rel[:, e:e + 1] == lane, 1.0, 0.0).astype(BF16)
        acc = acc + jnp.dot(onehot, y_ref[0], preferred_element_type=F32)
    o_ref[...] = x_ref[...] + mod_ref[5:6, :] * acc


def _combine(x_mid, modsel, y, rel_t, start, n_ctx):
    b, t, d = x_mid.shape
    ne, r, _ = y.shape
    tm = COMBINE_TILE
    win = tm + ROW_ALIGN_BF16
    ctx_tiles = n_ctx // tm
    y_specs = [pl.BlockSpec((pl.Element(1), pl.Element(win), pl.Element(d)),
                            lambda b, i, st, e=e: (e, st[b, e, i] * ROW_ALIGN_BF16, 0))
               for e in range(ne)]
    return pl.pallas_call(
        _combine_kernel,
        grid_spec=pltpu.PrefetchScalarGridSpec(
            num_scalar_prefetch=1,
            grid=(b, t // tm),
            in_specs=[pl.BlockSpec((None, tm, ne), lambda b, i, st: (b, i, 0)),
                      pl.BlockSpec((None, tm, d), lambda b, i, st: (b, i, 0)),
                      pl.BlockSpec((None, None, 6, d),
                                   lambda b, i, st: (b, (i >= ctx_tiles).astype(jnp.int32), 0, 0))] + y_specs,
            out_specs=pl.BlockSpec((None, tm, d), lambda b, i, st: (b, i, 0))),
        out_shape=jax.ShapeDtypeStruct((b, t, d), F32),
        compiler_params=_cparams(("arbitrary", "arbitrary")),
        name="moe_combine",
    )(start, rel_t, x_mid, modsel, *([y] * ne))


def _route_segment(aff, cap):
    gate_u, idx_u = lax.top_k(aff, cap)
    kth = gate_u[..., -1:]
    gt = aff > kth
    eq = aff == kth
    need = cap - jnp.sum(gt, axis=-1, keepdims=True, dtype=jnp.int32)
    mask = gt | (eq & (jnp.cumsum(eq.astype(jnp.int32), axis=-1) <= need))
    m32 = mask.astype(jnp.int32)
    pos = jnp.cumsum(m32, axis=-1) - m32
    idx = jnp.sort(idx_u.astype(jnp.int32), axis=-1)
    gate = jnp.take_along_axis(aff, idx, axis=-1)
    return mask, pos, idx, gate


def _moe(x_mid, h2, logits_t, modsel, w_up, w_down, layer, n_ctx):
    b, t, d = x_mid.shape
    ne = logits_t.shape[1]
    tm = COMBINE_TILE
    win = tm + ROW_ALIGN_BF16
    assert n_ctx % tm == 0 and t % tm == 0
    aff = jax.nn.softmax(logits_t, axis=1)
    cap_l = EC_CAPACITY_FACTOR * (t - n_ctx) // ne
    cap_c = EC_CAPACITY_FACTOR * n_ctx // ne
    r = b * (cap_l + cap_c)
    assert r % ROW_ALIGN_BF16 == 0 and r >= win
    mask_l, pos_l, idx_l, gate_l = _route_segment(aff[:, :, n_ctx:], cap_l)
    mask_c, pos_c, idx_c, gate_c = _route_segment(aff[:, :, :n_ctx], cap_c)
    bi = jnp.arange(b, dtype=jnp.int32)[:, None, None]

    def per_expert(a):
        return a.transpose(1, 0, 2).reshape(ne, -1)

    idx = jnp.concatenate([per_expert(idx_l + bi * t + n_ctx), per_expert(idx_c + bi * t)], axis=1)
    gate = jnp.concatenate([per_expert(gate_l), per_expert(gate_c)], axis=1)
    y = _expert_ffn(h2.reshape(b * t, d), idx, w_up, w_down, layer, gate[..., None])

    row = jnp.concatenate([pos_c + b * cap_l + bi * cap_c, pos_l + bi * cap_l], axis=2)
    mask = jnp.concatenate([mask_c, mask_l], axis=2)
    start = jnp.minimum(row[:, :, ::tm] // ROW_ALIGN_BF16 * ROW_ALIGN_BF16, r - win)
    rel = jnp.where(mask, row - jnp.repeat(start, tm, axis=2), -1)
    return _combine(x_mid, modsel, y, rel.transpose(0, 2, 1), start // ROW_ALIGN_BF16, n_ctx)


def _norm_slab_kernel(x_ref, mod_ref, g_ref, h_ref):
    h = _norm_mod(x_ref[...], g_ref[...], mod_ref[0:1, :], mod_ref[1:2, :]).astype(h_ref.dtype)
    for o in range(h_ref.shape[0]):
        h_ref[o] = h[:, o * LANES:(o + 1) * LANES]


def _norm_slabs(x, modsel, g):
    b, t, d = x.shape
    tm = ROW_TILE
    n_slab = d // LANES
    return pl.pallas_call(
        _norm_slab_kernel,
        grid=(b, t // tm),
        in_specs=[pl.BlockSpec((None, tm, d), lambda b, i: (b, i, 0)),
                  _mod_spec(d),
                  pl.BlockSpec((1, d), lambda b, i: (0, 0))],
        out_specs=pl.BlockSpec((None, n_slab, tm, LANES), lambda b, i: (b, 0, i, 0)),
        out_shape=jax.ShapeDtypeStruct((b, n_slab, t, LANES), BF16),
        compiler_params=_cparams(("arbitrary", "arbitrary")),
        name="norm_mod",
    )(x, modsel, g.reshape(1, d))


def _s5_core_kernel(u_ref, win_ref, toep_ref, wout_ref, a_ref, y_ref, zh_scr, *, m_ctx):
    dr = pl.program_id(2)
    m = u_ref.shape[0]
    rc = _largest_divisor(m, 512, 16)
    half = zh_scr.shape[1] // 2
    for r0 in range(0, m, rc):
        zh_scr[r0:r0 + rc, :] = jnp.dot(u_ref[r0:r0 + rc, :], win_ref[...], preferred_element_type=F32)
    ar = a_ref[0:1, :]
    ai = a_ref[1:2, :]
    rid = lax.broadcasted_iota(jnp.int32, (8, half), 0)

    def visit(blk, carry, reverse):
        sr, si = carry
        base = pl.multiple_of(blk * 8, 8)
        z8 = zh_scr[pl.ds(base, 8), :]
        hr = jnp.zeros((8, half), F32)
        hi = jnp.zeros((8, half), F32)
        for r in (range(7, -1, -1) if reverse else range(8)):
            hr = jnp.where(rid == r, sr, hr)
            hi = jnp.where(rid == r, si, hi)
            zr = z8[r:r + 1, 0:half]
            zi = z8[r:r + 1, half:2 * half]
            sr, si = ar * sr - ai * si + zr, ar * si + ai * sr + zi
        zh_scr[pl.ds(base, 8), 0:half] = hr
        zh_scr[pl.ds(base, 8), half:2 * half] = hi
        return sr, si

    zero = (jnp.zeros((1, half), F32), jnp.zeros((1, half), F32))
    nb, nb_ctx = m // 8, m_ctx // 8

    @pl.when(dr == 0)
    def _():
        lax.fori_loop(0, nb, lambda s, cr: visit(s, cr, False), zero)

    @pl.when(dr == 1)
    def _():
        carry = lax.fori_loop(0, nb_ctx, lambda s, cr: visit(nb_ctx - 1 - s, cr, True), zero)
        lax.fori_loop(0, nb - nb_ctx, lambda s, cr: visit(nb - 1 - s, cr, True), carry)

    for r0 in range(0, m, rc):
        y = (jnp.dot(u_ref[r0:r0 + rc, :], toep_ref[...], preferred_element_type=F32)
             + jnp.dot(zh_scr[r0:r0 + rc, :].astype(BF16), wout_ref[...], preferred_element_type=F32))
        y_ref[r0:r0 + rc, :] = y.astype(y_ref.dtype)


def _s5_weights(lam_re, lam_im, log_dt, b_re, b_im, c_re, c_im):
    sub = S5_SUB
    gs = S5_GROUP_SIZE
    gps = LANES // gs
    outs = []
    for dr in range(2):
        lre = jnp.minimum(lam_re[dr].astype(F32), -S5_MIN_DECAY)
        lim = lam_im[dr].astype(F32)
        dt = jnp.exp(log_dt[dr].astype(F32))[:, None]
        mag = jnp.exp(lre * dt)
        ang = lim * dt
        lbr, lbi = mag * jnp.cos(ang), mag * jnp.sin(ang)
        den = lre * lre + lim * lim
        f_re = ((lbr - 1) * lre + lbi * lim) / den
        f_im = (lbi * lre - (lbr - 1) * lim) / den
        bre, bim = b_re[dr].astype(F32), b_im[dr].astype(F32)
        bbr = f_re[..., None] * bre - f_im[..., None] * bim
        bbi = f_re[..., None] * bim + f_im[..., None] * bre
        cr, ci = c_re[dr].astype(F32), c_im[dr].astype(F32)
        pr, pi = [jnp.ones_like(lbr)], [jnp.zeros_like(lbr)]
        for _ in range(sub):
            pr.append(pr[-1] * lbr - pi[-1] * lbi)
            pi.append(pr[-2] * lbi + pi[-1] * lbr)
        pw_r, pw_i = jnp.stack(pr), jnp.stack(pi)
        pb_r = pw_r[..., None] * bbr - pw_i[..., None] * bbi
        pb_i = pw_r[..., None] * bbi + pw_i[..., None] * bbr
        cp_r = cr[None] * pw_r[:, :, None, :] - ci[None] * pw_i[:, :, None, :]
        cp_i = cr[None] * pw_i[:, :, None, :] + ci[None] * pw_r[:, :, None, :]
        kk = (jnp.einsum('gip,tgpj->tgij', cr, pb_r[:sub], precision=HI)
              - jnp.einsum('gip,tgpj->tgij', ci, pb_i[:sub], precision=HI))
        s_idx = jnp.arange(sub)
        lag = (s_idx[None, :] - s_idx[:, None]) if dr == 0 else (s_idx[:, None] - s_idx[None, :])
        kt = kk[jnp.clip(lag, 0, sub - 1)]
        kt = jnp.where((lag >= 0)[:, :, None, None, None], kt, 0.0)
        g_n = kt.shape[2]
        n_slab = g_n // gps
        cw, sw = sub * LANES, gps * S5_STATE
        col_tgi = jnp.arange(cw)
        spread_ti = (jnp.arange(sub * gs)[:, None] == (col_tgi // LANES * gs + col_tgi % gs)[None, :]).astype(F32)
        spread_p = (jnp.arange(S5_STATE)[:, None] == (jnp.arange(sw) % S5_STATE)[None, :]).astype(F32)
        grp_tgi = col_tgi // gs % gps
        grp_gp = jnp.arange(sw) // S5_STATE

        def spread(table, spread_mat, row_grp, col_grp):
            rows = table.shape[1]
            full = jnp.dot(table.reshape(n_slab * rows, -1), spread_mat, precision=HI)
            return full.reshape(n_slab, rows, -1) * (row_grp[:, None] == col_grp[None, :]).astype(F32)

        kt6 = kt.reshape(sub, sub, n_slab, gps, gs, gs).transpose(2, 0, 3, 5, 1, 4)
        toep = spread(kt6.reshape(n_slab, cw, sub * gs), spread_ti, grp_tgi, grp_tgi)
        e_in = (sub - 1 - s_idx) if dr == 0 else s_idx

        def in_map(pb):
            t5 = pb[e_in].reshape(sub, n_slab, gps, S5_STATE, gs).transpose(1, 0, 2, 4, 3)
            return spread(t5.reshape(n_slab, cw, S5_STATE), spread_p, grp_tgi, grp_gp)

        win = jnp.concatenate([in_map(pb_r), in_map(pb_i)], axis=2)
        e_out = (s_idx + 1) if dr == 0 else (sub - s_idx)

        def out_map(cp):
            t5 = cp[e_out].reshape(sub, n_slab, gps, gs, S5_STATE).transpose(1, 2, 4, 0, 3)
            return spread(t5.reshape(n_slab, sw, sub * gs), spread_ti, grp_gp, grp_tgi)

        wout = jnp.concatenate([out_map(cp_r), -out_map(cp_i)], axis=1)
        a_sub = jnp.stack([pw_r[sub].reshape(n_slab, gps * S5_STATE),
                           pw_i[sub].reshape(n_slab, gps * S5_STATE)], axis=1)
        outs.append((win, toep, wout, a_sub))

    stack = lambda k, dt: jnp.stack([o[k] for o in outs], axis=1).astype(dt)
    return stack(0, BF16), stack(1, BF16), stack(2, BF16), stack(3, F32)


def _s5_core(hs, weights, n_ctx):
    b, n_slab, t, _ = hs.shape
    sub = S5_SUB
    win, toep, wout, a_sub = weights
    m = t // sub
    m_ctx = n_ctx // sub
    assert m % 8 == 0 and m_ctx % 8 == 0
    cw = sub * LANES
    sw = win.shape[-1]
    u = hs.reshape(b, n_slab, m, cw)
    kern = functools.partial(_s5_core_kernel, m_ctx=m_ctx)
    y = pl.pallas_call(
        kern,
        grid=(n_slab, b, 2),
        in_specs=[pl.BlockSpec((None, None, m, cw), lambda o, b, d: (b, o, 0, 0)),
                  pl.BlockSpec((None, None, cw, sw), lambda o, b, d: (o, d, 0, 0)),
                  pl.BlockSpec((None, None, cw, cw), lambda o, b, d: (o, d, 0, 0)),
                  pl.BlockSpec((None, None, sw, cw), lambda o, b, d: (o, d, 0, 0)),
                  pl.BlockSpec((None, None, 2, sw // 2), lambda o, b, d: (o, d, 0, 0))],
        out_specs=pl.BlockSpec((None, None, None, m, cw), lambda o, b, d: (d, b, o, 0, 0)),
        out_shape=jax.ShapeDtypeStruct((2, b, n_slab, m, cw), BF16),
        scratch_shapes=[pltpu.VMEM((m, sw), F32)],
        compiler_params=_cparams(("arbitrary", "arbitrary", "arbitrary")),
        name="s5_core",
    )(u, win, toep, wout, a_sub)
    return y.reshape(2, b, n_slab, t, LANES)


def _ret_in_kernel(x_ref, mod_ref, g_ref, w_ref, cos_ref, sin_ref, z_ref, *, n_rope_cols):
    h = _norm_mod(x_ref[...], g_ref[...], mod_ref[0:1, :], mod_ref[1:2, :]).astype(BF16)
    cos, sin = cos_ref[...], sin_ref[...]
    half = cos.shape[1]
    tn = 4 * half
    for n0 in range(0, w_ref.shape[1], tn):
        z = jnp.dot(h, w_ref[:, n0:n0 + tn], preferred_element_type=F32)
        if n0 < n_rope_cols:
            for c0 in range(0, tn, 2 * half):
                x1 = z[:, c0:c0 + half]
                x2 = z[:, c0 + half:c0 + 2 * half]
                z_ref[:, n0 + c0:n0 + c0 + half] = (x1 * cos - x2 * sin).astype(z_ref.dtype)
                z_ref[:, n0 + c0 + half:n0 + c0 + 2 * half] = (x2 * cos + x1 * sin).astype(z_ref.dtype)
        else:
            z_ref[:, n0:n0 + tn] = z.astype(z_ref.dtype)


def _ret_in(x, modsel, g, w, cos, sin, n_rope_cols):
    b, t, d = x.shape
    n_out = w.shape[1]
    tm = ROW_TILE
    half = cos.shape[1]
    assert n_out % (4 * half) == 0 and n_rope_cols % (4 * half) == 0
    kern = functools.partial(_ret_in_kernel, n_rope_cols=n_rope_cols)
    return pl.pallas_call(
        kern,
        grid=(b, t // tm),
        in_specs=[pl.BlockSpec((None, tm, d), lambda b, i: (b, i, 0)),
                  _mod_spec(d),
                  pl.BlockSpec((1, d), lambda b, i: (0, 0)),
                  pl.BlockSpec((d, n_out), lambda b, i: (0, 0), pipeline_mode=pl.Buffered(1)),
                  pl.BlockSpec((tm, half), lambda b, i: (i, 0)),
                  pl.BlockSpec((tm, half), lambda b, i: (i, 0))],
        out_specs=pl.BlockSpec((None, tm, n_out), lambda b, i: (b, i, 0)),
        out_shape=jax.ShapeDtypeStruct((b, t, n_out), BF16),
        compiler_params=_cparams(("arbitrary", "arbitrary")),
        name="ret_in",
    )(x, modsel, g.reshape(1, d), w, cos, sin)


def _ret_core_kernel(lg_ref, q_ref, k_ref, v_ref, *rest, rev, qk_scale):
    if rev:
        of_ref, g_ref, o_ref, s_scr = rest
    else:
        o_ref, s_scr = rest
    hg = pl.program_id(1)
    c = pl.program_id(2)
    tc = q_ref.shape[0]
    n_hp, dqk, dv = s_scr.shape

    @pl.when(c == 0)
    def _():
        s_scr[...] = jnp.zeros(s_scr.shape, F32)

    row = lax.broadcasted_iota(jnp.int32, (tc, tc), 0)
    col = lax.broadcasted_iota(jnp.int32, (tc, tc), 1)
    diff = (col - row) if rev else (row - col)
    pos = lax.broadcasted_iota(jnp.int32, (tc, 1), 0).astype(F32)
    for hp in range(n_hp):
        lg = jnp.full((1, 1), lg_ref[hg * n_hp + hp], F32)
        inner = jnp.where(diff >= 0, jnp.exp(lg * jnp.maximum(diff, 0).astype(F32)), 0.0) * qk_scale
        if rev:
            q_dec = jnp.exp(lg * (tc - pos))
            k_dec = jnp.exp(lg * pos) * qk_scale
        else:
            q_dec = jnp.exp(lg * (pos + 1.0))
            k_dec = jnp.exp(lg * (tc - 1.0 - pos)) * qk_scale
        q = q_ref[:, hp * dqk:(hp + 1) * dqk]
        k = k_ref[:, hp * dqk:(hp + 1) * dqk]
        v = v_ref[:, hp * dv:(hp + 1) * dv]
        s = lax.dot_general(q, k, (((1,), (1,)), ((), ())), preferred_element_type=F32) * inner
        state = s_scr[hp]
        o = (jnp.dot(s.astype(BF16), v, preferred_element_type=F32)
             + jnp.dot((q.astype(F32) * q_dec).astype(BF16), state.astype(BF16), preferred_element_type=F32))
        kd_t = (k.astype(F32) * k_dec).T.astype(BF16)
        s_scr[hp] = state * jnp.exp(lg * tc) + jnp.dot(kd_t, v, preferred_element_type=F32)
        if rev:
            tot = of_ref[:, hp * dv:(hp + 1) * dv] + o
            nrm = tot * lax.rsqrt(jnp.mean(tot * tot, axis=-1, keepdims=True) + NORM_EPS)
            gg = g_ref[:, hp * dv:(hp + 1) * dv].astype(F32)
            o_ref[:, hp * dv:(hp + 1) * dv] = (gg * jax.nn.sigmoid(gg) * nrm).astype(o_ref.dtype)
        else:
            o_ref[:, hp * dv:(hp + 1) * dv] = o


def _ret_core(z, log_gamma, o_fwd, rev):
    b, t, _ = z.shape
    nh = RET_HEADS
    dqk = z.shape[2] // (6 * nh)
    dv = 2 * dqk
    tc = ROW_TILE
    nc = t // tc

    if rev:
        def cmap(c):
            return jnp.where(c == 0, 0, nc - c)
    else:
        def cmap(c):
            return c

    hp = RET_HEADS_PER_STEP
    ng = nh // hp
    kern = functools.partial(_ret_core_kernel, rev=rev, qk_scale=dqk ** -0.5)
    in_specs = [pl.BlockSpec((None, tc, hp * dqk), lambda b, h, c, lg: (b, cmap(c), h)),
                pl.BlockSpec((None, tc, hp * dqk), lambda b, h, c, lg: (b, cmap(c), ng + h)),
                pl.BlockSpec((None, tc, hp * dv), lambda b, h, c, lg: (b, cmap(c), ng + h))]
    args = [z, z, z]
    if rev:
        in_specs += [pl.BlockSpec((None, tc, hp * dv), lambda b, h, c, lg: (b, cmap(c), h)),
                     pl.BlockSpec((None, tc, hp * dv), lambda b, h, c, lg: (b, cmap(c), 2 * ng + h))]
        args += [o_fwd, z]
    out_dtype = BF16 if rev else F32
    return pl.pallas_call(
        kern,
        grid_spec=pltpu.PrefetchScalarGridSpec(
            num_scalar_prefetch=1,
            grid=(b, ng, nc),
            in_specs=in_specs,
            out_specs=pl.BlockSpec((None, tc, hp * dv), lambda b, h, c, lg: (b, cmap(c), h)),
            scratch_shapes=[pltpu.VMEM((hp, dqk, dv), F32)]),
        out_shape=jax.ShapeDtypeStruct((b, t, nh * dv), out_dtype),
        compiler_params=_cparams(("arbitrary", "arbitrary", "arbitrary")),
        name="ret_core_bwd" if rev else "ret_core_fwd",
    )(log_gamma, *args)


def _final_kernel(x_ref, g_ref, o_ref):
    x = x_ref[...]
    o_ref[...] = x * lax.rsqrt(jnp.mean(x * x, axis=-1, keepdims=True) + NORM_EPS) * g_ref[...]


def _final_norm(x, g, n_ctx):
    b, t, d = x.shape
    tm = ROW_TILE
    skip = n_ctx // tm
    return pl.pallas_call(
        _final_kernel,
        grid=(b, (t - n_ctx) // tm),
        in_specs=[pl.BlockSpec((None, tm, d), lambda b, i: (b, i + skip, 0)),
                  pl.BlockSpec((1, d), lambda b, i: (0, 0))],
        out_specs=pl.BlockSpec((None, tm, d), lambda b, i: (b, i, 0)),
        out_shape=jax.ShapeDtypeStruct((b, t - n_ctx, d), F32),
        compiler_params=_cparams(("arbitrary", "arbitrary")),
        name="final_norm",
    )(x, g.reshape(1, d))


def _rope_tables(n_ctx, n_lat, head_dim):
    rows = n_lat // GRID_W
    row = jnp.repeat(jnp.arange(rows, dtype=F32), GRID_W)
    col = jnp.tile(jnp.arange(GRID_W, dtype=F32), rows)
    n_freq = head_dim // 4
    inv_freq = ROPE_THETA ** (-jnp.arange(n_freq, dtype=F32) / n_freq)
    ang = jnp.concatenate([row[:, None] * inv_freq, col[:, None] * inv_freq], axis=-1)
    cos = jnp.concatenate([jnp.ones((n_ctx, head_dim // 2), F32), jnp.cos(ang)], axis=0)
    sin = jnp.concatenate([jnp.zeros((n_ctx, head_dim // 2), F32), jnp.sin(ang)], axis=0)
    return cos, sin


def kernel(x, c, ctx, c_ctx, mod_w, mod_b, norm_g, attn_w_qkv, attn_w_o, attn_q_gain, attn_k_gain,
           s5_lambda_re, s5_lambda_im, s5_log_dt, s5_b_re, s5_b_im, s5_c_re, s5_c_im, s5_d, s5_w_glu, s5_b_glu,
           ret_w_qkvg, ret_w_o, ret_log_decay, moe_w_router, moe_w_up, moe_w_down, final_g):
    bsz, n_lat, d = x.shape
    n_ctx = ctx.shape[1]
    depth = mod_w.shape[0]
    assert n_ctx == ROW_TILE and n_lat % ROW_TILE == 0 and bsz <= 7

    xs = jnp.concatenate([ctx, x], axis=1).astype(F32)
    cvec = jnp.zeros((8, d), F32).at[:bsz].set(c).at[bsz].set(c_ctx)
    mods = _modulation(cvec, mod_w, mod_b)

    cos_a, sin_a = _rope_tables(n_ctx, n_lat, ATTN_HEAD_DIM)
    cos2 = jnp.concatenate([cos_a, cos_a], axis=1)
    sin2 = jnp.concatenate([-sin_a, sin_a], axis=1)
    cos_r, sin_r = _rope_tables(n_ctx, n_lat, d // RET_HEADS)

    tm = ROW_TILE

    def row_spec(width):
        return pl.BlockSpec((None, tm, width), lambda b, i: (b, i, 0))

    for i in range(depth):
        kind, j = i % N_MIXERS, i // N_MIXERS
        m6 = mods[i].reshape(8, 6, d)
        modsel = jnp.stack([jnp.broadcast_to(m6[bsz], (bsz, 6, d)), m6[:bsz]], axis=1)
        wr_t = moe_w_router[i].T.astype(F32)
        if kind == 0:
            q, k, v = _attn_in(xs, modsel, norm_g[i, 0], attn_w_qkv[j].astype(BF16),
                               attn_q_gain[j], attn_k_gain[j], cos2, sin2)
            o = _flash(q, k, v, n_ctx)
            x_mid, h2, lg = _post_call(_post_plain_kernel, "attn_out", [o], [row_spec(o.shape[-1])],
                                       [attn_w_o[j].astype(BF16)], xs, modsel, norm_g[i, 1], wr_t)
        elif kind == 1:
            hs = _norm_slabs(xs, modsel, norm_g[i, 0])
            w5 = _s5_weights(s5_lambda_re[j], s5_lambda_im[j], s5_log_dt[j], s5_b_re[j], s5_b_im[j],
                             s5_c_re[j], s5_c_im[j])
            y = _s5_core(hs, w5, n_ctx)
            n_slab = d // LANES
            slab_specs = [pl.BlockSpec((2, None, n_slab, tm, LANES), lambda b, i: (0, b, 0, i, 0)),
                          pl.BlockSpec((None, n_slab, tm, LANES), lambda b, i: (b, 0, i, 0))]
            x_mid, h2, lg = _post_call(_post_glu_kernel, "s5_out", [y, hs], slab_specs,
                                       [s5_d[j].reshape(1, d).astype(F32), s5_w_glu[j].astype(BF16),
                                        s5_b_glu[j].reshape(1, -1).astype(F32)],
                                       xs, modsel, norm_g[i, 1], wr_t)
        else:
            z = _ret_in(xs, modsel, norm_g[i, 0], ret_w_qkvg[j].astype(BF16), cos_r, sin_r, 2 * d)
            o_f = None
            for dr in range(2):
                log_gamma = -jnp.abs(ret_log_decay[j, dr].astype(F32))
                o_f = _ret_core(z, log_gamma, o_f, rev=(dr == 1))
            x_mid, h2, lg = _post_call(_post_plain_kernel, "ret_out", [o_f], [row_spec(o_f.shape[-1])],
                                       [ret_w_o[j].astype(BF16)], xs, modsel, norm_g[i, 1], wr_t)
        xs = _moe(x_mid, h2, lg, modsel, moe_w_up, moe_w_down, i, n_ctx)

    return _final_norm(xs, final_g, n_ctx).astype(x.dtype)
```
